```python
import math
import jax, jax.numpy as jnp
from jax import lax
import numpy as np

D_MODEL = 1024
BATCH = 1
SEQ = 16384
DEPTH = 2
DEC_BATCH = 32
DEC_SEQ = 8
PAST_LEN = 16384
PAGE_SIZE = 128

HEAD_DIM = 64
HG_DK = 128
HG_DV = 128
HG_HEADS = (D_MODEL // 2) // HG_DV
SB_HEADS = (D_MODEL // 4) // HEAD_DIM
DSA_HEADS = (D_MODEL // 4) // HEAD_DIM
IDX_HEADS = DSA_HEADS
IDX_DIM = HEAD_DIM
MIX_WIDTH = HG_HEADS * HG_DV + (SB_HEADS + DSA_HEADS) * HEAD_DIM
TOPK_MAX = 256
D_FF = ((8 * D_MODEL // 3 + 127) // 128) * 128
CONV_W = 3
HG_CHUNK = 64
Q_BLOCK = 128
ROPE_THETA = 10000.0
EPS = 1e-6
NEG = -1e30
_IN_SIZES = (HG_HEADS * HG_DK, HG_HEADS * HG_DK, HG_HEADS * HG_DV, HG_HEADS * HG_DV,
             SB_HEADS * HEAD_DIM, SB_HEADS * HEAD_DIM, SB_HEADS * HEAD_DIM,
             DSA_HEADS * HEAD_DIM, DSA_HEADS * HEAD_DIM, DSA_HEADS * HEAD_DIM,
             IDX_HEADS * IDX_DIM, IDX_DIM, IDX_HEADS)
IN_DIM = sum(_IN_SIZES)

kernel_name = 'hymba_hgrn2_stickbreak_dsa_convffn_step'


def rmsnorm(x, g):
    xf = x.astype(jnp.float32)
    y = xf * lax.rsqrt(jnp.mean(xf * xf, axis=-1, keepdims=True) + EPS)
    return (y * g.astype(jnp.float32)).astype(x.dtype)


def rope(x, pos):
    half = x.shape[-1] // 2
    inv = jnp.exp(-math.log(ROPE_THETA) * jnp.arange(half, dtype=jnp.float32) / half)
    ang = pos.astype(jnp.float32)[:, None] * inv[None, :]
    cos, sin = jnp.cos(ang)[:, None, :], jnp.sin(ang)[:, None, :]
    xf = x.astype(jnp.float32)
    x1, x2 = xf[..., :half], xf[..., half:]
    return jnp.concatenate([x1 * cos - x2 * sin, x2 * cos + x1 * sin], axis=-1).astype(x.dtype)


def _split_in(p):
    offs = [int(o) for o in np.cumsum(_IN_SIZES)[:-1]]
    return jnp.split(p, offs, axis=-1)


def _over_query_blocks(fn, q_pos, *qs):
    T = q_pos.shape[0]
    blk = T if T <= Q_BLOCK else math.gcd(Q_BLOCK, T)
    if blk == T:
        return fn(q_pos, *qs)
    nb = T // blk
    split = lambda a: jnp.moveaxis(a.reshape(a.shape[0], nb, blk, *a.shape[2:]), 1, 0)
    out = lax.map(lambda args: fn(*args), (q_pos.reshape(nb, blk),) + tuple(split(a) for a in qs))
    out = jnp.moveaxis(out, 0, 1)
    return out.reshape(out.shape[0], T, *out.shape[3:])


def _hgrn2(q, f_pre, i, g, lb, norm_g, S0):
    f32 = jnp.float32
    B, T, H, _ = q.shape
    qf = jax.nn.silu(q.astype(f32))
    xf = f_pre.astype(f32)
    log_f = jnp.logaddexp(jnp.log(lb), jnp.log1p(-lb) + jax.nn.log_sigmoid(xf))
    kf = (1.0 - lb) * jax.nn.sigmoid(-xf)
    vf = i.astype(f32)
    C = math.gcd(HG_CHUNK, T)
    nc = T // C
    chunks = lambda a: jnp.moveaxis(a.reshape(B, nc, C, *a.shape[2:]), 1, 0)
    causal = jnp.tril(jnp.ones((C, C), dtype=bool))[None, :, :, None, None]

    def step(S, blk):
        qc, kc, vc, lc = blk
        b = jnp.cumsum(lc, axis=1)
        o_inter = jnp.einsum('bthk,bhkv->bthv', qc * jnp.exp(b), S)
        decay = jnp.exp(jnp.where(causal, b[:, :, None] - b[:, None, :], -jnp.inf))
        attn = jnp.einsum('bthk,bshk,btshk->btsh', qc, kc, decay)
        o = o_inter + jnp.einsum('btsh,bshv->bthv', attn, vc)
        b_last = b[:, -1]
        S = jnp.exp(b_last)[..., None] * S + jnp.einsum('bshk,bshv->bhkv', kc * jnp.exp(b_last[:, None] - b), vc)
        return S, o

    S, o = lax.scan(step, S0.astype(f32), tuple(chunks(a) for a in (qf, kf, vf, log_f)))
    o = jnp.moveaxis(o, 0, 1).reshape(B, T, H, HG_DV)
    o = o * lax.rsqrt(jnp.mean(o * o, axis=-1, keepdims=True) + EPS) * norm_g.astype(f32)
    o = o * jax.nn.silu(g.astype(f32))
    return o.astype(q.dtype), S


def _sb_block(q_pos, q, k, v):
    f32 = jnp.float32
    z = jnp.einsum('bthd,bshd->bhts', q, k, preferred_element_type=f32) * HEAD_DIM ** -0.5
    visible = jnp.arange(k.shape[1])[None, :] < q_pos[:, None]
    log_keep = jnp.where(visible, jax.nn.log_sigmoid(-z), 0.0)
    log_after = lax.cumsum(log_keep, axis=3, reverse=True) - log_keep
    a = jnp.where(visible, jnp.exp(jax.nn.log_sigmoid(z) + log_after), 0.0)
    return jnp.einsum('bhts,bshd->bthd', a, v.astype(f32)).astype(q.dtype)


def _stick_breaking(q, k, v, pos):
    return _over_query_blocks(lambda qp, qb: _sb_block(qp, qb, k, v), pos, q)


def _dsa_block(q_pos, q, iq, iw, ik, gather, topk):
    f32 = jnp.float32
    L = ik.shape[1]
    dots = jnp.einsum('bthd,bsd->bths', iq, ik, preferred_element_type=f32) * IDX_DIM ** -0.5
    score = jnp.einsum('bths,bth->bts', jax.nn.relu(dots), iw.astype(f32))
    visible = jnp.arange(L)[None, :] <= q_pos[:, None]
    score = jnp.where(visible[None], score, NEG)
    _, idx = lax.top_k(score, topk)
    valid = idx <= q_pos[None, :, None]
    kg, vg = gather(idx)
    logits = jnp.einsum('bthd,btkhd->bhtk', q, kg, preferred_element_type=f32) * HEAD_DIM ** -0.5
    logits = jnp.where(valid[:, None], logits, NEG)
    p = jax.nn.softmax(logits, axis=-1)
    return jnp.einsum('bhtk,btkhd->bthd', p, vg.astype(f32)).astype(q.dtype)


def _sparse_attention(q, iq, iw, ik, gather, pos, topk):
    fn = lambda qp, qb, iqb, iwb: _dsa_block(qp, qb, iqb, iwb, ik, gather, topk)
    return _over_query_blocks(fn, pos, q, iq, iw)


def _prompt_sb(q, k, v, pos):
    return _stick_breaking(q, k, v, pos)


def _prompt_dsa(q, iq, iw, k, v, ik, pos):
    bidx = jnp.arange(k.shape[0])[:, None, None]
    gather = lambda idx: (k[bidx, idx], v[bidx, idx])
    return _sparse_attention(q, iq, iw, ik, gather, pos, min(TOPK_MAX, k.shape[1] // 4))


def _gather_pages(pool, page_table):
    g = pool[page_table]
    return g.reshape(g.shape[0], g.shape[1] * g.shape[2], *g.shape[3:])


def _sample_readers(sb_k, sb_v, dsa_k, dsa_v, idx_k, page_table):
    def read_sb(q, k_new, v_new, pos):
        k = jnp.concatenate([_gather_pages(sb_k, page_table).astype(k_new.dtype), k_new], axis=1)
        v = jnp.concatenate([_gather_pages(sb_v, page_table).astype(v_new.dtype), v_new], axis=1)
        return _stick_breaking(q, k, v, pos)

    def read_dsa(q, iq, iw, k_new, v_new, ik_new, pos):
        DB, T = k_new.shape[:2]
        ik = jnp.concatenate([_gather_pages(idx_k, page_table).astype(ik_new.dtype), ik_new], axis=1)
        bidx = jnp.arange(DB)[:, None, None]

        def gather(idx):
            in_past = (idx < PAST_LEN)[..., None, None]
            pidx = jnp.minimum(idx, PAST_LEN - 1)
            phys = page_table[bidx, pidx // PAGE_SIZE]
            off = pidx % PAGE_SIZE
            nidx = jnp.clip(idx - PAST_LEN, 0, T - 1)
            kg = jnp.where(in_past, dsa_k[phys, off].astype(k_new.dtype), k_new[bidx, nidx])
            vg = jnp.where(in_past, dsa_v[phys, off].astype(v_new.dtype), v_new[bidx, nidx])
            return kg, vg

        return _sparse_attention(q, iq, iw, ik, gather, pos, min(TOPK_MAX, (PAST_LEN + T) // 4))

    return read_sb, read_dsa


def _conv_ffn(h, prev, w_up, conv_w, conv_b, w_down):
    T = h.shape[1]
    a, gate = jnp.split(h @ w_up, 2, axis=-1)
    buf = jnp.concatenate([prev.astype(a.dtype), a], axis=1)
    c = conv_b
    for j in range(CONV_W):
        c = c + buf[:, j:j + T] * conv_w[j]
    y = (jax.nn.silu(c) * gate) @ w_down
    return y, buf[:, buf.shape[1] - (CONV_W - 1):]


def _layer(x, pos, read_sb, read_dsa, hg_state, conv_prev, lb, g_attn, w_in, hg_norm, w_out, g_ffn, w_up, conv_w, conv_b, w_down):
    B, T, _ = x.shape
    heads = lambda a, n: a.reshape(B, T, n, a.shape[-1] // n)
    hq, hf, hi, hg, sq, sk, sv, cq, ck, cv, iq, ik, iw = _split_in(rmsnorm(x, g_attn) @ w_in)
    o_a, hg_new = _hgrn2(heads(hq, HG_HEADS), heads(hf, HG_HEADS), heads(hi, HG_HEADS), heads(hg, HG_HEADS), lb, hg_norm, hg_state)
    sk, sv = heads(sk, SB_HEADS), heads(sv, SB_HEADS)
    o_b = read_sb(heads(sq, SB_HEADS), sk, sv, pos)
    ck, cv = rope(heads(ck, DSA_HEADS), pos), heads(cv, DSA_HEADS)
    ik = rope(ik[:, :, None, :], pos)[:, :, 0, :]
    o_c = read_dsa(rope(heads(cq, DSA_HEADS), pos), rope(heads(iq, IDX_HEADS), pos), iw, ck, cv, ik, pos)
    mixed = jnp.concatenate([o_a.reshape(B, T, -1), o_b.reshape(B, T, -1), o_c.reshape(B, T, -1)], axis=-1)
    x = x + mixed.astype(x.dtype) @ w_out
    y, conv_new = _conv_ffn(rmsnorm(x, g_ffn), conv_prev, w_up, conv_w, conv_b, w_down)
    return x + y, (sk, sv, ck, cv, ik, hg_new.astype(x.dtype), conv_new)


def _stack_rows(rows, j):
    return jnp.stack([r[j] for r in rows])


def _to_pages(a):
    return a.reshape(a.shape[0], a.shape[1], a.shape[2] // PAGE_SIZE, PAGE_SIZE, *a.shape[3:])


def setup_inputs(seed: int = 0) -> dict:
    key = jax.random.key(seed)
    ks = jax.random.split(key, 24)
    f32 = jnp.float32
    n_pages = PAST_LEN // PAGE_SIZE
    n_pool = (5 * DEC_BATCH * n_pages) // 4
    nrm = lambda k, shape, s: jax.random.normal(k, shape, f32) * s
    page_table = jax.random.permutation(ks[0], n_pool)[: DEC_BATCH * n_pages].reshape(DEC_BATCH, n_pages).astype(jnp.int32)
    return {
        'x_prompt': nrm(ks[1], (BATCH, SEQ, D_MODEL), 1.0),
        'x_sample': nrm(ks[2], (DEC_BATCH, DEC_SEQ, D_MODEL), 1.0),
        'cache_sb_k': nrm(ks[3], (DEPTH, n_pool, PAGE_SIZE, SB_HEADS, HEAD_DIM), 1.0),
        'cache_sb_v': nrm(ks[4], (DEPTH, n_pool, PAGE_SIZE, SB_HEADS, HEAD_DIM), 1.0),
        'cache_dsa_k': nrm(ks[5], (DEPTH, n_pool, PAGE_SIZE, DSA_HEADS, HEAD_DIM), 1.0),
        'cache_dsa_v': nrm(ks[6], (DEPTH, n_pool, PAGE_SIZE, DSA_HEADS, HEAD_DIM), 1.0),
        'cache_idx_k': nrm(ks[7], (DEPTH, n_pool, PAGE_SIZE, IDX_DIM), 1.0),
        'state_hgrn': nrm(ks[8], (DEPTH, DEC_BATCH, HG_HEADS, HG_DK, HG_DV), 0.5),
        'state_conv': nrm(ks[9], (DEPTH, DEC_BATCH, CONV_W - 1, D_FF), 1.0),
        'page_table': page_table,
        'hg_lower_bounds': nrm(ks[10], (DEPTH, HG_HEADS, HG_DK), 1.0),
        'attn_norm': 1.0 + nrm(ks[11], (DEPTH, D_MODEL), 0.02),
        'w_in': nrm(ks[12], (DEPTH, D_MODEL, IN_DIM), D_MODEL ** -0.5),
        'hg_norm': 1.0 + nrm(ks[13], (DEPTH, HG_DV), 0.02),
        'w_out': nrm(ks[14], (DEPTH, MIX_WIDTH, D_MODEL), MIX_WIDTH ** -0.5),
        'ffn_norm': 1.0 + nrm(ks[15], (DEPTH, D_MODEL), 0.02),
        'w_up': nrm(ks[16], (DEPTH, D_MODEL, 2 * D_FF), D_MODEL ** -0.5),
        'conv_w': nrm(ks[17], (DEPTH, CONV_W, D_FF), CONV_W ** -0.5),
        'conv_b': nrm(ks[18], (DEPTH, D_FF), 0.01),
        'w_down': nrm(ks[19], (DEPTH, D_FF, D_MODEL), D_FF ** -0.5),
        'final_norm': 1.0 + nrm(ks[20], (D_MODEL,), 0.02),
    }


def reference(x_prompt, x_sample, cache_sb_k, cache_sb_v, cache_dsa_k, cache_dsa_v, cache_idx_k, state_hgrn, state_conv, page_table, hg_lower_bounds, attn_norm, w_in, hg_norm, w_out, ffn_norm, w_up, conv_w, conv_b, w_down, final_norm):
    lb_all = jnp.cumsum(jax.nn.softmax(hg_lower_bounds.astype(jnp.float32), axis=0), axis=0)
    lb_all = lb_all - lb_all[0]
    B, T = x_prompt.shape[:2]
    TS = x_sample.shape[1]
    pos_p = jnp.arange(T, dtype=jnp.int32)
    pos_s = PAST_LEN + jnp.arange(TS, dtype=jnp.int32)
    xp, xs = x_prompt, x_sample
    p_new, s_new = [], []
    for l in range(DEPTH):
        wl = (lb_all[l], attn_norm[l], w_in[l], hg_norm[l], w_out[l], ffn_norm[l], w_up[l], conv_w[l], conv_b[l], w_down[l])
        xp, rp = _layer(xp, pos_p, _prompt_sb, _prompt_dsa,
                        jnp.zeros((B, HG_HEADS, HG_DK, HG_DV), jnp.float32),
                        jnp.zeros((B, CONV_W - 1, D_FF), xp.dtype), *wl)
        read_sb, read_dsa = _sample_readers(cache_sb_k[l], cache_sb_v[l], cache_dsa_k[l], cache_dsa_v[l], cache_idx_k[l], page_table)
        xs, rs = _layer(xs, pos_s, read_sb, read_dsa, state_hgrn[l], state_conv[l], *wl)
        p_new.append(rp)
        s_new.append(rs)
    y_prompt = rmsnorm(xp, final_norm)
    y_sample = rmsnorm(xs, final_norm)
    p_sb_k = _to_pages(_stack_rows(p_new, 0))
    p_sb_v = _to_pages(_stack_rows(p_new, 1))
    p_dsa_k = _to_pages(_stack_rows(p_new, 2))
    p_dsa_v = _to_pages(_stack_rows(p_new, 3))
    p_idx_k = _to_pages(_stack_rows(p_new, 4))
    p_hgrn = _stack_rows(p_new, 5)
    p_conv = _stack_rows(p_new, 6)
    s_sb_k = _stack_rows(s_new, 0)
    s_sb_v = _stack_rows(s_new, 1)
    s_dsa_k = _stack_rows(s_new, 2)
    s_dsa_v = _stack_rows(s_new, 3)
    s_idx_k = _stack_rows(s_new, 4)
    s_hgrn = _stack_rows(s_new, 5)
    s_conv = _stack_rows(s_new, 6)
    return (y_prompt, y_sample, p_sb_k, p_sb_v, p_dsa_k, p_dsa_v, p_idx_k, p_hgrn, p_conv, s_sb_k, s_sb_v, s_dsa_k, s_dsa_v, s_idx_k, s_hgrn, s_conv)
```

```python
import functools
import math

import jax
import jax.numpy as jnp
from jax import lax
from jax.experimental import pallas as pl
from jax.experimental.pallas import tpu as pltpu

F32 = jnp.float32
BF16 = jnp.bfloat16
I32 = jnp.int32

HEAD_DIM = 64
N_HEADS = 4
HG_D = 128
ATT_W = N_HEADS * HEAD_DIM
HG_W = N_HEADS * HG_D
TOPK_MAX = 256
ROPE_THETA = 10000.0
EPS = 1e-6
NEG = -1e30
INT_MIN = -2147483648
VMEM_LIMIT_BYTES = 56 * 1024 * 1024
LANES = 128

_C_HG = 0
_C_SB = 4 * HG_W
_C_DSA = _C_SB + 3 * ATT_W
_C_IQ = _C_DSA + 3 * ATT_W
_C_IK = _C_IQ + ATT_W
_C_IW = _C_IK + ATT_W
IN_PAD = _C_IW + LANES


def _const_spec(shape):
    return pl.BlockSpec(shape, lambda *_: (0,) * len(shape), pipeline_mode=pl.Buffered(1))


def _params(sem):
    return pltpu.CompilerParams(dimension_semantics=sem, vmem_limit_bytes=VMEM_LIMIT_BYTES)


def _dot_nt(a, b):
    return lax.dot_general(a, b, (((1,), (1,)), ((), ())), preferred_element_type=F32)


def _rmsnorm(x, g):
    return x * lax.rsqrt(jnp.mean(x * x, axis=-1, keepdims=True) + EPS) * g


def _softplus(z):
    return jnp.maximum(z, 0.0) + jnp.log1p(jnp.exp(-jnp.abs(z)))


def _head_stack(q):
    lane = lax.broadcasted_iota(I32, q.shape, 1)
    zero = jnp.zeros_like(q)
    return jnp.concatenate([jnp.where(lane // HEAD_DIM == h, q, zero) for h in range(N_HEADS)], axis=0)


def _head_merge(acc, t):
    lane = lax.broadcasted_iota(I32, (t, ATT_W), 1)
    out = jnp.zeros((t, ATT_W), F32)
    for h in range(N_HEADS):
        out = jnp.where(lane // HEAD_DIM == h, acc[h * t:(h + 1) * t], out)
    return out


def _rope(x, cos, sin):
    lane = lax.broadcasted_iota(I32, x.shape, 1)
    first = (lane % HEAD_DIM) < (HEAD_DIM // 2)
    w = x.shape[1]
    swapped = jnp.where(first, pltpu.roll(x, w - HEAD_DIM // 2, 1), pltpu.roll(x, HEAD_DIM // 2, 1))
    return x * cos + swapped * sin


def _inproj_body(x_ref, g_ref, w_ref, cos_ref, sin_ref, *out_refs, page):
    h = _rmsnorm(x_ref[...], g_ref[...]).astype(BF16)
    p = jnp.dot(h, w_ref[...], preferred_element_type=F32)
    cos, sin = cos_ref[...], sin_ref[...]
    scale = HEAD_DIM ** -0.5
    hg = p[:, _C_HG:_C_SB]
    sq = (p[:, _C_SB:_C_SB + ATT_W] * scale).astype(BF16)
    sk = p[:, _C_SB + ATT_W:_C_SB + 2 * ATT_W]
    sv = p[:, _C_SB + 2 * ATT_W:_C_DSA]
    cq = (_rope(p[:, _C_DSA:_C_DSA + ATT_W], cos, sin) * scale).astype(BF16)
    ck = _rope(p[:, _C_DSA + ATT_W:_C_DSA + 2 * ATT_W], cos, sin)
    cv = p[:, _C_DSA + 2 * ATT_W:_C_IQ]
    iq = (_rope(p[:, _C_IQ:_C_IK], cos, sin) * scale).astype(BF16)
    ik4 = _rope(p[:, _C_IK:_C_IW], cos, sin)
    if page is None:
        (hg_ref, sq_ref, sk_ref, sv_ref, cq_ref, ck_ref, cv_ref, iq_ref, ik_ref, iw_ref) = out_refs
        sk_ref[...] = sk
        sv_ref[...] = sv
        ck_ref[...] = ck
        cv_ref[...] = cv
        ik_ref[...] = ik4[:, :HEAD_DIM]
    else:
        (hg_ref, sq_ref, skp_ref, svp_ref, skt_ref, svb_ref, cq_ref, ckp_ref, cvp_ref, ckt_ref, cvb_ref,
         iq_ref, ikt_ref, ikp_ref, iw_ref) = out_refs
        skt, svt, ckt, cvt, ikt = sk.T, sv.T, ck.T, cv.T, ik4.T
        for pg in range(p.shape[0] // page):
            sl = slice(pg * page, (pg + 1) * page)
            skp_ref[pg] = skt[:, sl]
            svp_ref[pg] = svt[:, sl]
            ckp_ref[pg] = ckt[:, sl]
            cvp_ref[pg] = cvt[:, sl]
            ikp_ref[pg] = ikt[:HEAD_DIM, sl]
        skt_ref[0] = skt.astype(BF16)
        ckt_ref[0] = ckt.astype(BF16)
        ikt_ref[0] = ikt.astype(BF16)
        svb_ref[...] = sv.astype(BF16)
        cvb_ref[...] = cv.astype(BF16)
    hg_ref[...] = hg
    sq_ref[...] = sq
    cq_ref[...] = cq
    iq_ref[...] = iq
    iw_ref[...] = p[:, _C_IW:]


def _inproj(x, g, w_pad, cos, sin, block_rows, page=None):
    rows, d = x.shape
    nblk = rows // block_rows
    row = lambda w: pl.BlockSpec((block_rows, w), lambda i: (i, 0))
    f32o = lambda w: (jax.ShapeDtypeStruct((rows, w), F32), row(w))
    bfo = lambda w: (jax.ShapeDtypeStruct((rows, w), BF16), row(w))
    if page is None:
        outs = [f32o(4 * HG_W), bfo(ATT_W), f32o(ATT_W), f32o(ATT_W), bfo(ATT_W), f32o(ATT_W), f32o(ATT_W),
                bfo(ATT_W), f32o(HEAD_DIM), f32o(LANES)]
    else:
        ppb = block_rows // page
        pages = lambda w: (jax.ShapeDtypeStruct((rows // page, w, page), F32),
                           pl.BlockSpec((ppb, w, page), lambda i: (i, 0, 0)))
        tiles = (jax.ShapeDtypeStruct((nblk, ATT_W, block_rows), BF16),
                 pl.BlockSpec((1, ATT_W, block_rows), lambda i: (i, 0, 0)))
        outs = [f32o(4 * HG_W), bfo(ATT_W), pages(ATT_W), pages(ATT_W), tiles, bfo(ATT_W),
                bfo(ATT_W), pages(ATT_W), pages(ATT_W), tiles, bfo(ATT_W),
                bfo(ATT_W), tiles, pages(HEAD_DIM), f32o(LANES)]
    return pl.pallas_call(
        functools.partial(_inproj_body, page=page), grid=(nblk,),
        in_specs=[row(d), _const_spec((1, d)), _const_spec(w_pad.shape), row(ATT_W), row(ATT_W)],
        out_specs=tuple(o[1] for o in outs), out_shape=tuple(o[0] for o in outs),
        compiler_params=_params(("arbitrary",)),
    )(x, g, w_pad, cos, sin)


def _hgrn_body(hq_ref, hf_ref, hi_ref, hgate_ref, loglb_ref, log1mlb_ref, omlb_ref, ng_ref, s0_ref,
               o_ref, s_out_ref, st_ref, *, chunk, n_chunks):
    t = pl.program_id(1)

    @pl.when(t == 0)
    def _():
        for h in range(N_HEADS):
            st_ref[h] = s0_ref[0, h].T

    c = chunk
    ri = lax.broadcasted_iota(I32, (c, c), 0)
    ci = lax.broadcasted_iota(I32, (c, c), 1)
    tri = (ri >= ci).astype(F32)
    row = lax.broadcasted_iota(I32, (c, HG_W), 0)
    loglb, log1mlb, omlb, ng = loglb_ref[...], log1mlb_ref[...], omlb_ref[...], ng_ref[...]

    def one_chunk(ic, carry):
        r0 = pl.multiple_of(ic * c, c)
        q = hq_ref[pl.ds(r0, c), :]
        x = hf_ref[pl.ds(r0, c), :]
        v = hi_ref[pl.ds(r0, c), :]
        gate = hgate_ref[pl.ds(r0, c), :]
        qf = q * jax.nn.sigmoid(q)
        log_sig = jnp.minimum(x, 0.0) - jnp.log1p(jnp.exp(-jnp.abs(x)))
        bterm = log1mlb + log_sig
        log_f = jnp.maximum(loglb, bterm) + jnp.log1p(jnp.exp(-jnp.abs(loglb - bterm)))
        kf = omlb * jax.nn.sigmoid(-x)
        b = jnp.dot(tri, log_f, preferred_element_type=F32, precision=lax.Precision.HIGHEST)
        b_last = b[c - 1:c, :]
        qe = qf * jnp.exp(b)
        kk = kf * jnp.exp(b_last - b)
        e_last = jnp.exp(b_last)
        o_intra = [jnp.zeros((c, HG_D), F32) for _ in range(N_HEADS)]
        for s in range(c):
            w = jnp.exp(jnp.where(row >= s, b - b[s:s + 1, :], -jnp.inf)) * kf[s:s + 1, :]
            pw = qf * w
            for h in range(N_HEADS):
                sl = slice(h * HG_D, (h + 1) * HG_D)
                attn = jnp.sum(pw[:, sl], axis=1, keepdims=True)
                o_intra[h] = o_intra[h] + attn * v[s:s + 1, sl]
        outs = []
        for h in range(N_HEADS):
            sl = slice(h * HG_D, (h + 1) * HG_D)
            st = st_ref[h]
            o = o_intra[h] + lax.dot_general(qe[:, sl], st, (((1,), (1,)), ((), ())),
                                             preferred_element_type=F32, precision=lax.Precision.HIGHEST)
            upd = lax.dot_general(v[:, sl], kk[:, sl], (((0,), (0,)), ((), ())),
                                  preferred_element_type=F32, precision=lax.Precision.HIGHEST)
            st_ref[h] = st * e_last[:, sl] + upd
            o = o * lax.rsqrt(jnp.mean(o * o, axis=1, keepdims=True) + EPS)
            outs.append(o)
        o_all = jnp.concatenate(outs, axis=1) * ng * (gate * jax.nn.sigmoid(gate))
        o_ref[pl.ds(r0, c), :] = o_all
        return carry

    lax.fori_loop(0, n_chunks, one_chunk, 0)

    @pl.when(t == pl.num_programs(1) - 1)
    def _():
        for h in range(N_HEADS):
            s_out_ref[0, h] = st_ref[h].T


def _hgrn(hg4, lb_terms, ng, s0, batch, seq, block_rows, chunk):
    nt = seq // block_rows
    col = lambda cidx: pl.BlockSpec((block_rows, HG_W), lambda b, t: (b * nt + t, cidx))
    loglb, log1mlb, omlb = lb_terms
    body = functools.partial(_hgrn_body, chunk=chunk, n_chunks=block_rows // chunk)
    return pl.pallas_call(
        body, grid=(batch, nt),
        in_specs=[col(0), col(1), col(2), col(3),
                  _const_spec((1, HG_W)), _const_spec((1, HG_W)), _const_spec((1, HG_W)), _const_spec((1, HG_W)),
                  pl.BlockSpec((1, N_HEADS, HG_D, HG_D), lambda b, t: (b, 0, 0, 0))],
        out_specs=(pl.BlockSpec((block_rows, HG_W), lambda b, t: (b * nt + t, 0)),
                   pl.BlockSpec((1, N_HEADS, HG_D, HG_D), lambda b, t: (b, 0, 0, 0))),
        out_shape=(jax.ShapeDtypeStruct((batch * seq, HG_W), F32),
                   jax.ShapeDtypeStruct((batch, N_HEADS, HG_D, HG_D), F32)),
        scratch_shapes=[pltpu.VMEM((N_HEADS, HG_D, HG_D), F32)],
        compiler_params=_params(("arbitrary", "arbitrary")),
    )(hg4, hg4, hg4, hg4, loglb, log1mlb, omlb, ng, s0)


def _suffix_matrix(tk):
    j = lax.broadcasted_iota(I32, (tk, tk), 0)
    s = lax.broadcasted_iota(I32, (tk, tk), 1)
    return (j > s).astype(BF16)


def _times_values(p, v_tile, feature_major):
    if feature_major:
        return _dot_nt(p, v_tile)
    return jnp.dot(p, v_tile, preferred_element_type=F32)


def _sb_tile(z, visible, carry, suffix, v_tile, v_feature_major=False):
    log_keep = -_softplus(z)
    if visible is not None:
        log_keep = jnp.where(visible, log_keep, 0.0)
    hi = log_keep.astype(BF16)
    lo = (log_keep - hi.astype(F32)).astype(BF16)
    after = (jnp.dot(hi, suffix, preferred_element_type=F32)
             + jnp.dot(lo, suffix, preferred_element_type=F32) + carry)
    a = jnp.exp(z + log_keep + after)
    if visible is not None:
        a = jnp.where(visible, a, 0.0)
    carry = carry + jnp.sum(log_keep, axis=1, keepdims=True)
    return _times_values(a.astype(BF16), v_tile, v_feature_major), carry


def _sb_prompt_body(q_ref, kt_ref, v_ref, o_ref, acc_ref, *, tq, tk):
    qi = pl.program_id(0)
    qs = _head_stack(q_ref[...])
    suffix = _suffix_matrix(tk)
    rows = N_HEADS * tq
    row_pos = qi * tq + lax.broadcasted_iota(I32, (rows, tk), 0) % tq
    col = lax.broadcasted_iota(I32, (rows, tk), 1)
    acc_ref[...] = jnp.zeros_like(acc_ref)

    def tile(kb, carry, masked):
        ks = pl.multiple_of(kb * tk, tk)
        z = jnp.dot(qs, kt_ref[kb], preferred_element_type=F32)
        visible = (col + kb * tk < row_pos) if masked else None
        av, carry = _sb_tile(z, visible, carry, suffix, v_ref[pl.ds(ks, tk), :])
        acc_ref[...] += av
        return carry

    kb_last = (qi * tq + tq - 1) // tk
    carry = tile(kb_last, jnp.zeros((rows, 1), F32), True)
    n_full = (qi * tq) // tk

    def partial_tiles(j, carry):
        return tile(kb_last - 1 - j, carry, True)

    carry = lax.fori_loop(0, kb_last - n_full, partial_tiles, carry)

    def full_tiles(j, carry):
        return tile(n_full - 1 - j, carry, False)

    lax.fori_loop(0, n_full, full_tiles, carry)
    o_ref[...] = _head_merge(acc_ref[...], tq)


def _sb_prompt(q, kt, v, tq):
    t = q.shape[0]
    tk = kt.shape[2]
    body = functools.partial(_sb_prompt_body, tq=tq, tk=tk)
    return pl.pallas_call(
        body, grid=(t // tq,),
        in_specs=[pl.BlockSpec((tq, ATT_W), lambda i: (i, 0)), _const_spec(kt.shape), _const_spec((t, ATT_W))],
        out_specs=pl.BlockSpec((tq, ATT_W), lambda i: (i, 0)),
        out_shape=jax.ShapeDtypeStruct((t, ATT_W), F32),
        scratch_shapes=[pltpu.VMEM((N_HEADS * tq, ATT_W), F32)],
        compiler_params=_params(("arbitrary",)),
    )(q, kt, v)


def _sb_sample_body(pt_ref, q_ref, kp_ref, vp_ref, kn_ref, vn_ref, o_ref, acc_ref, carry_ref, *, n_pages, page, past):
    j = pl.program_id(1)
    tq = q_ref.shape[1]
    rows = N_HEADS * tq

    @pl.when(j == 0)
    def _():
        acc_ref[...] = jnp.zeros_like(acc_ref)
        carry_ref[...] = jnp.zeros_like(carry_ref)

    new = j == 0
    k_tile = jnp.where(new, kn_ref[0], kp_ref[...]).astype(BF16)
    v_tile = jnp.where(new, vn_ref[0], vp_ref[...]).astype(BF16)
    base = jnp.where(new, past, (n_pages - j) * page)
    qs = _head_stack(q_ref[0])
    z = jnp.dot(qs, k_tile, preferred_element_type=F32)
    row_pos = past + lax.broadcasted_iota(I32, (rows, page), 0) % tq
    col = lax.broadcasted_iota(I32, (rows, page), 1)
    visible = col + base < row_pos
    av, carry = _sb_tile(z, visible, carry_ref[...], _suffix_matrix(page), v_tile, v_feature_major=True)
    acc_ref[...] += av
    carry_ref[...] = carry

    @pl.when(j == n_pages)
    def _():
        o_ref[0] = _head_merge(acc_ref[...], tq)


def _page_spec(layer, n_pages, page, width, reverse):
    def index(b, j, pt):
        logical = (n_pages - j) if reverse else j
        logical = jnp.clip(logical, 0, n_pages - 1)
        return (layer, pt[b, logical], 0, 0)
    return pl.BlockSpec((None, None, width, page), index)


def _sb_sample(page_table, q, cache_k, cache_v, k_new, v_new, layer):
    nb, tq, _ = q.shape
    n_pages = page_table.shape[1]
    page = cache_k.shape[3]
    body = functools.partial(_sb_sample_body, n_pages=n_pages, page=page, past=n_pages * page)

    per_b = lambda s: pl.BlockSpec((1,) + s, lambda b, j, pt: (b, 0, 0))
    grid_spec = pltpu.PrefetchScalarGridSpec(
        num_scalar_prefetch=1, grid=(nb, n_pages + 1),
        in_specs=[per_b((tq, ATT_W)),
                  _page_spec(layer, n_pages, page, ATT_W, True), _page_spec(layer, n_pages, page, ATT_W, True),
                  per_b((ATT_W, page)), per_b((ATT_W, page))],
        out_specs=per_b((tq, ATT_W)),
        scratch_shapes=[pltpu.VMEM((N_HEADS * tq, ATT_W), F32), pltpu.VMEM((N_HEADS * tq, 1), F32)])
    return pl.pallas_call(
        body, grid_spec=grid_spec, out_shape=jax.ShapeDtypeStruct((nb, tq, ATT_W), F32),
        compiler_params=_params(("arbitrary", "arbitrary")),
    )(page_table, q, cache_k, cache_v, k_new, v_new)


def _score_keys(dots4, iw_b, tq, visible):
    score = jnp.zeros((tq, dots4.shape[1]), F32)
    for h in range(N_HEADS):
        score = score + jnp.maximum(dots4[h * tq:(h + 1) * tq], 0.0) * iw_b[h]
    score = score + 0.0
    if visible is not None:
        score = jnp.where(visible, score, NEG)
    bits = pltpu.bitcast(score, I32)
    return jnp.where(bits < 0, bits ^ 0x7FFFFFFF, bits)


def _kth_largest(count_ge, shape, k):
    def step(i, v_u):
        cand = v_u | lax.shift_left(jnp.int32(1), 31 - i)
        take = count_ge(cand ^ INT_MIN) >= k
        return jnp.where(take, cand, v_u)
    v_u = lax.fori_loop(0, 32, step, jnp.zeros(shape, I32))
    return v_u ^ INT_MIN


def _prefix_matrix(tk):
    j = lax.broadcasted_iota(I32, (tk, tk), 0)
    s = lax.broadcasted_iota(I32, (tk, tk), 1)
    return (j < s).astype(BF16)


def _select_mask(key, vstar_b, need_b, eq_carry, prefix, visible):
    eq = key == vstar_b
    eqf = jnp.where(eq, 1.0, 0.0)
    before = jnp.dot(eqf.astype(BF16), prefix, preferred_element_type=F32) + eq_carry
    sel = jnp.where(key > vstar_b, 1.0, jnp.where(eq, jnp.where(before < need_b, 1.0, 0.0), 0.0))
    if visible is not None:
        sel = jnp.where(visible, sel, 0.0)
    return sel, eq_carry + jnp.sum(eqf, axis=1, keepdims=True)


def _softmax_tile(logits, sel, m, l, acc, v_tile, v_feature_major=False):
    on = sel > 0.5
    m_new = jnp.maximum(m, jnp.max(jnp.where(on, logits, NEG), axis=1, keepdims=True))
    p = jnp.where(on, jnp.exp(logits - m_new), 0.0)
    alpha = jnp.exp(m - m_new)
    l = l * alpha + jnp.sum(p, axis=1, keepdims=True)
    acc = acc * alpha + _times_values(p.astype(BF16), v_tile, v_feature_major)
    return m_new, l, acc


def _dsa_prompt_body(iq_ref, iw_ref, cq_ref, ikt_ref, ckt_ref, cv_ref, o_ref,
                     keys_ref, acc_ref, m_ref, l_ref, *, tq, tk, topk):
    qi = pl.program_id(0)
    kb_last = (qi * tq + tq - 1) // tk
    n_full = (qi * tq + 1) // tk
    row_pos = qi * tq + lax.broadcasted_iota(I32, (tq, tk), 0)
    col = lax.broadcasted_iota(I32, (tq, tk), 1)

    iqs = _head_stack(iq_ref[...])
    iw = iw_ref[...]
    iw_b = [jnp.broadcast_to(iw[:, h:h + 1], (tq, tk)) for h in range(N_HEADS)]

    def score_tile(kb, masked):
        dots4 = jnp.dot(iqs, ikt_ref[kb], preferred_element_type=F32)
        visible = (col + kb * tk <= row_pos) if masked else None
        keys_ref[kb] = _score_keys(dots4, iw_b, tq, visible)

    def score_full(kb, c):
        score_tile(kb, False)
        return c

    def score_part(kb, c):
        score_tile(kb, True)
        return c

    lax.fori_loop(0, n_full, score_full, 0)
    lax.fori_loop(n_full, kb_last + 1, score_part, 0)

    def count(pred):
        def add(kb, cnt):
            return cnt + jnp.where(pred(keys_ref[kb]), 1, 0)
        cnt = lax.fori_loop(0, kb_last + 1, add, jnp.zeros((tq, tk), I32))
        return jnp.sum(cnt, axis=1, keepdims=True)

    def count_ge(cand):
        cand_b = jnp.broadcast_to(cand, (tq, tk))
        return count(lambda key: key >= cand_b)

    vstar = _kth_largest(count_ge, (tq, 1), topk)
    vstar_b = jnp.broadcast_to(vstar, (tq, tk))
    need = (topk - count(lambda key: key > vstar_b)).astype(F32)
    need_b = jnp.broadcast_to(need, (tq, tk))

    cqs = _head_stack(cq_ref[...])
    prefix = _prefix_matrix(tk)
    acc_ref[...] = jnp.zeros_like(acc_ref)
    m_ref[...] = jnp.full_like(m_ref, NEG)
    l_ref[...] = jnp.zeros_like(l_ref)

    def attend(kb, eq_carry, masked):
        ks = pl.multiple_of(kb * tk, tk)
        visible = (col + kb * tk <= row_pos) if masked else None
        sel, eq_carry = _select_mask(keys_ref[kb], vstar_b, need_b, eq_carry, prefix, visible)
        logits4 = jnp.dot(cqs, ckt_ref[kb], preferred_element_type=F32)
        v_tile = cv_ref[pl.ds(ks, tk), :]
        for h in range(N_HEADS):
            m, l, acc = _softmax_tile(logits4[h * tq:(h + 1) * tq], sel, m_ref[h], l_ref[h], acc_ref[h], v_tile)
            m_ref[h] = m
            l_ref[h] = l
            acc_ref[h] = acc
        return eq_carry

    eq_carry = lax.fori_loop(0, n_full, lambda kb, c: attend(kb, c, False), jnp.zeros((tq, 1), F32))
    lax.fori_loop(n_full, kb_last + 1, lambda kb, c: attend(kb, c, True), eq_carry)
    acc = jnp.concatenate([acc_ref[h] / l_ref[h] for h in range(N_HEADS)], axis=0)
    o_ref[...] = _head_merge(acc, tq)


def _dsa_prompt(iq, iw, cq, ikt, ckt, cv, tq, topk):
    t = iq.shape[0]
    tk = ckt.shape[2]
    body = functools.partial(_dsa_prompt_body, tq=tq, tk=tk, topk=topk)
    row = lambda w: pl.BlockSpec((tq, w), lambda i: (i, 0))
    return pl.pallas_call(
        body, grid=(t // tq,),
        in_specs=[row(ATT_W), row(LANES), row(ATT_W),
                  _const_spec(ikt.shape), _const_spec(ckt.shape), _const_spec((t, ATT_W))],
        out_specs=row(ATT_W),
        out_shape=jax.ShapeDtypeStruct((t, ATT_W), F32),
        scratch_shapes=[pltpu.VMEM((t // tk, tq, tk), I32),
                        pltpu.VMEM((N_HEADS, tq, ATT_W), F32),
                        pltpu.VMEM((N_HEADS, tq, 1), F32), pltpu.VMEM((N_HEADS, tq, 1), F32)],
        compiler_params=_params(("arbitrary",)),
    )(iq, iw, cq, ikt, ckt, cv)


def _dsa_score_sample_body(pt_ref, iq_ref, iw_ref, ikp_ref, ikn_ref, keys_ref, vstar_ref, need_ref, all_ref,
                           *, n_pages, page, past, topk):
    j = pl.program_id(1)
    tq = iq_ref.shape[1]
    new = j == n_pages
    ik = jnp.where(new, ikn_ref[0], ikp_ref[...]).astype(BF16)
    iq = iq_ref[0]
    iq4 = jnp.concatenate([iq[:, h * HEAD_DIM:(h + 1) * HEAD_DIM] for h in range(N_HEADS)], axis=0)
    dots4 = jnp.dot(iq4, ik, preferred_element_type=F32)
    iw = iw_ref[0]
    iw_b = [jnp.broadcast_to(iw[:, h:h + 1], (tq, page)) for h in range(N_HEADS)]
    row_pos = past + lax.broadcasted_iota(I32, (tq, page), 0)
    col = lax.broadcasted_iota(I32, (tq, page), 1)
    visible = col + j * page <= row_pos
    key = _score_keys(dots4, iw_b, tq, visible)
    keys_ref[0, 0] = key
    all_ref[j] = key

    @pl.when(new)
    def _():
        def count(pred):
            def add(kb, cnt):
                return cnt + jnp.where(pred(all_ref[kb]), 1, 0)
            cnt = lax.fori_loop(0, n_pages + 1, add, jnp.zeros((tq, page), I32))
            return jnp.sum(cnt, axis=1, keepdims=True)

        def count_ge(cand):
            cand_b = jnp.broadcast_to(cand, (tq, page))
            return count(lambda k_: k_ >= cand_b)

        vstar = _kth_largest(count_ge, (tq, 1), topk)
        vstar_b = jnp.broadcast_to(vstar, (tq, page))
        need = (topk - count(lambda k_: k_ > vstar_b)).astype(F32)
        vstar_ref[0] = vstar_b
        need_ref[0] = jnp.broadcast_to(need, (tq, page))


def _dsa_score_sample(page_table, iq, iw, cache_ik, ik_new, layer, topk):
    nb, tq, _ = iq.shape
    n_pages = page_table.shape[1]
    page = cache_ik.shape[3]
    body = functools.partial(_dsa_score_sample_body, n_pages=n_pages, page=page, past=n_pages * page, topk=topk)
    per_b = lambda s: pl.BlockSpec((1,) + s, lambda b, j, pt: (b, 0, 0))
    grid_spec = pltpu.PrefetchScalarGridSpec(
        num_scalar_prefetch=1, grid=(nb, n_pages + 1),
        in_specs=[per_b((tq, ATT_W)), per_b((tq, LANES)),
                  _page_spec(layer, n_pages, page, HEAD_DIM, False), per_b((HEAD_DIM, page))],
        out_specs=(pl.BlockSpec((1, 1, tq, page), lambda b, j, pt: (b, j, 0, 0)),
                   per_b((tq, page)), per_b((tq, page))),
        scratch_shapes=[pltpu.VMEM((n_pages + 1, tq, page), I32)])
    return pl.pallas_call(
        body, grid_spec=grid_spec,
        out_shape=(jax.ShapeDtypeStruct((nb, n_pages + 1, tq, page), I32),
                   jax.ShapeDtypeStruct((nb, tq, page), I32), jax.ShapeDtypeStruct((nb, tq, page), F32)),
        compiler_params=_params(("arbitrary", "arbitrary")),
    )(page_table, iq, iw, cache_ik, ik_new)


def _dsa_attend_sample_body(pt_ref, cq_ref, keys_ref, vstar_ref, need_ref, kp_ref, vp_ref, kn_ref, vn_ref, o_ref,
                            acc_ref, m_ref, l_ref, eq_ref, *, n_pages, page, past):
    j = pl.program_id(1)
    tq = cq_ref.shape[1]

    @pl.when(j == 0)
    def _():
        acc_ref[...] = jnp.zeros_like(acc_ref)
        m_ref[...] = jnp.full_like(m_ref, NEG)
        l_ref[...] = jnp.zeros_like(l_ref)
        eq_ref[...] = jnp.zeros_like(eq_ref)

    new = j == n_pages
    k_tile = jnp.where(new, kn_ref[0], kp_ref[...]).astype(BF16)
    v_tile = jnp.where(new, vn_ref[0], vp_ref[...]).astype(BF16)
    row_pos = past + lax.broadcasted_iota(I32, (tq, page), 0)
    col = lax.broadcasted_iota(I32, (tq, page), 1)
    visible = col + j * page <= row_pos
    sel, eq_carry = _select_mask(keys_ref[0, 0], vstar_ref[0], need_ref[0], eq_ref[...], _prefix_matrix(page), visible)
    eq_ref[...] = eq_carry
    logits4 = jnp.dot(_head_stack(cq_ref[0]), k_tile, preferred_element_type=F32)
    for h in range(N_HEADS):
        m, l, acc = _softmax_tile(logits4[h * tq:(h + 1) * tq], sel, m_ref[h], l_ref[h], acc_ref[h], v_tile,
                                  v_feature_major=True)
        m_ref[h] = m
        l_ref[h] = l
        acc_ref[h] = acc

    @pl.when(new)
    def _():
        acc = jnp.concatenate([acc_ref[h] / l_ref[h] for h in range(N_HEADS)], axis=0)
        o_ref[0] = _head_merge(acc, tq)


def _dsa_attend_sample(page_table, cq, keys, vstar, need, cache_k, cache_v, k_new, v_new, layer):
    nb, tq, _ = cq.shape
    n_pages = page_table.shape[1]
    page = cache_k.shape[3]
    body = functools.partial(_dsa_attend_sample_body, n_pages=n_pages, page=page, past=n_pages * page)
    per_b = lambda s: pl.BlockSpec((1,) + s, lambda b, j, pt: (b, 0, 0))
    grid_spec = pltpu.PrefetchScalarGridSpec(
        num_scalar_prefetch=1, grid=(nb, n_pages + 1),
        in_specs=[per_b((tq, ATT_W)),
                  pl.BlockSpec((1, 1, tq, page), lambda b, j, pt: (b, j, 0, 0)),
                  per_b((tq, page)), per_b((tq, page)),
                  _page_spec(layer, n_pages, page, ATT_W, False), _page_spec(layer, n_pages, page, ATT_W, False),
                  per_b((ATT_W, page)), per_b((ATT_W, page))],
        out_specs=per_b((tq, ATT_W)),
        scratch_shapes=[pltpu.VMEM((N_HEADS, tq, ATT_W), F32),
                        pltpu.VMEM((N_HEADS, tq, 1), F32), pltpu.VMEM((N_HEADS, tq, 1), F32),
                        pltpu.VMEM((tq, 1), F32)])
    return pl.pallas_call(
        body, grid_spec=grid_spec, out_shape=jax.ShapeDtypeStruct((nb, tq, ATT_W), F32),
        compiler_params=_params(("arbitrary", "arbitrary")),
    )(page_table, cq, keys, vstar, need, cache_k, cache_v, k_new, v_new)


def _ffn_body(*refs, seq_rows, carried, final):
    it = iter(refs)
    x_ref, oa_ref, ob_ref, oc_ref, wout_ref, g_ref, wup_ref, cw_ref, cb_ref, wdown_ref = (next(it) for _ in range(10))
    e0_ref = e1_ref = gf_ref = carry_ref = None
    if not carried:
        e0_ref, e1_ref = next(it), next(it)
    if final:
        gf_ref = next(it)
    xo_ref, tail_ref = next(it), next(it)
    if carried:
        carry_ref = next(it)

        @pl.when(pl.program_id(0) == 0)
        def _():
            carry_ref[...] = jnp.zeros_like(carry_ref)

    mixed = jnp.concatenate([oa_ref[...], ob_ref[...], oc_ref[...]], axis=1).astype(BF16)
    x1 = x_ref[...] + jnp.dot(mixed, wout_ref[...], preferred_element_type=F32)
    h = _rmsnorm(x1, g_ref[...]).astype(BF16)
    up = jnp.dot(h, wup_ref[...], preferred_element_type=F32)
    d_ff = up.shape[1] // 2
    a, gate = up[:, :d_ff], up[:, d_ff:]
    rows = a.shape[0]
    r = lax.broadcasted_iota(I32, a.shape, 0) % seq_rows
    if carried:
        nc = carry_ref.shape[0]
        e0 = jnp.broadcast_to(carry_ref[nc - 2:nc - 1, :], a.shape)
        e1 = jnp.broadcast_to(carry_ref[nc - 1:nc, :], a.shape)
    else:
        e0, e1 = e0_ref[...], e1_ref[...]
    a_m1 = jnp.where(r == 0, e1, pltpu.roll(a, 1, 0))
    a_m2 = jnp.where(r == 0, e0, jnp.where(r == 1, e1, pltpu.roll(a, 2, 0)))
    cw = cw_ref[...]
    c = cb_ref[...] + a_m2 * cw[0:1, :] + a_m1 * cw[1:2, :] + a * cw[2:3, :]
    act = (c * jax.nn.sigmoid(c) * gate).astype(BF16)
    x2 = x1 + jnp.dot(act, wdown_ref[...], preferred_element_type=F32)
    xo_ref[...] = _rmsnorm(x2, gf_ref[...]) if final else x2
    tail = tail_ref.shape[0]
    tail_ref[...] = a[rows - tail:, :]
    if carried:
        nc = carry_ref.shape[0]
        carry_ref[...] = a[rows - nc:, :]


def _ffn(x, oa, ob, oc, w_out, g, w_up, conv_w, conv_b, w_down, block_rows, prev=None, final_g=None):
    rows, d = x.shape
    d_ff = w_down.shape[0]
    carried = prev is None
    final = final_g is not None
    row = lambda w: pl.BlockSpec((block_rows, w), lambda i: (i, 0))
    ins = [x, oa, ob, oc, w_out, g, w_up, conv_w, conv_b, w_down]
    in_specs = [row(d), row(HG_W), row(ATT_W), row(ATT_W), _const_spec(w_out.shape), _const_spec((1, d)),
                _const_spec(w_up.shape), _const_spec(conv_w.shape), _const_spec((1, d_ff)), _const_spec(w_down.shape)]
    if not carried:
        ins += list(prev)
        in_specs += [row(d_ff), row(d_ff)]
    if final:
        ins.append(final_g)
        in_specs.append(_const_spec((1, d)))
    if carried:
        tail_rows, tail_spec = 8, pl.BlockSpec((8, d_ff), lambda i: (0, 0))
        scratch = [pltpu.VMEM((8, d_ff), F32)]
        seq_rows = block_rows
    else:
        tail_rows, tail_spec = rows, row(d_ff)
        scratch = []
        seq_rows = 8
    body = functools.partial(_ffn_body, seq_rows=seq_rows, carried=carried, final=final)
    return pl.pallas_call(
        body, grid=(rows // block_rows,),
        in_specs=in_specs, out_specs=(row(d), tail_spec),
        out_shape=(jax.ShapeDtypeStruct((rows, d), F32), jax.ShapeDtypeStruct((tail_rows, d_ff), F32)),
        scratch_shapes=scratch,
        compiler_params=_params(("arbitrary",)),
    )(*ins)


def _rope_tables(pos):
    half = HEAD_DIM // 2
    inv = jnp.exp(-math.log(ROPE_THETA) * jnp.arange(half, dtype=F32) / half)
    ang = pos.astype(F32)[:, None] * inv[None, :]
    cos, sin = jnp.cos(ang), jnp.sin(ang)
    cos = jnp.tile(jnp.concatenate([cos, cos], axis=1), (1, N_HEADS))
    sin = jnp.tile(jnp.concatenate([-sin, sin], axis=1), (1, N_HEADS))
    return cos, sin


def _pad_w_in(w):
    d = w.shape[0]
    ik = w[:, _C_IQ + ATT_W:_C_IQ + ATT_W + HEAD_DIM]
    iw = w[:, _C_IQ + ATT_W + HEAD_DIM:]
    pad = jnp.zeros((d, LANES - iw.shape[1]), w.dtype)
    return jnp.concatenate([w[:, :_C_IK], ik, ik, ik, ik, iw, pad], axis=1).astype(BF16)


def _new_page(a, page):
    return jnp.swapaxes(jnp.pad(a, ((0, 0), (0, page - a.shape[1]), (0, 0))), 1, 2)


def _feature_major(cache):
    depth, pool, page = cache.shape[:3]
    perm = (0, 1, 3, 4, 2) if cache.ndim == 5 else (0, 1, 3, 2)
    return jnp.transpose(cache, perm).reshape(depth, pool, -1, page)


def _slot_major(pages, lead):
    depth, n_pg, w, page = pages.shape
    if w == HEAD_DIM:
        return jnp.transpose(pages, (0, 1, 3, 2)).reshape((depth,) + lead + (n_pg, page, w))
    a = pages.reshape(depth, n_pg, N_HEADS, HEAD_DIM, page)
    return jnp.transpose(a, (0, 1, 4, 2, 3)).reshape((depth,) + lead + (n_pg, page, N_HEADS, HEAD_DIM))


def kernel(x_prompt, x_sample, cache_sb_k, cache_sb_v, cache_dsa_k, cache_dsa_v, cache_idx_k, state_hgrn, state_conv, page_table, hg_lower_bounds, attn_norm, w_in, hg_norm, w_out, ffn_norm, w_up, conv_w, conv_b, w_down, final_norm):
    depth = w_in.shape[0]
    nb_p, seq, d = x_prompt.shape
    nb_s, seq_s, _ = x_sample.shape
    assert nb_p == 1, "the prompt group is one sequence"
    n_pool, page = cache_sb_k.shape[1], cache_sb_k.shape[2]
    n_pages = page_table.shape[1]
    past = n_pages * page
    d_ff = w_down.shape[1]
    rows_s = nb_s * seq_s

    lb = jnp.cumsum(jax.nn.softmax(hg_lower_bounds.astype(F32), axis=0), axis=0)
    lb = (lb - lb[0]).reshape(depth, 1, HG_W)
    cos_p, sin_p = _rope_tables(jnp.arange(seq, dtype=I32))
    cos_s, sin_s = _rope_tables(jnp.tile(past + jnp.arange(seq_s, dtype=I32), nb_s))

    cache_sb_k, cache_sb_v = _feature_major(cache_sb_k), _feature_major(cache_sb_v)
    cache_dsa_k, cache_dsa_v = _feature_major(cache_dsa_k), _feature_major(cache_dsa_v)
    cache_idx_k = _feature_major(cache_idx_k)

    tq = 128
    block_rows = 256
    topk_p = min(TOPK_MAX, seq // 4)
    topk_s = min(TOPK_MAX, (past + seq_s) // 4)

    xp = x_prompt.reshape(seq, d)
    xs = x_sample.reshape(rows_s, d)
    p_rows, s_rows = [], []
    for l in range(depth):
        w_in_l = _pad_w_in(w_in[l])
        g_attn = attn_norm[l].reshape(1, d)
        lb_terms = (jnp.log(lb[l]), jnp.log1p(-lb[l]), 1.0 - lb[l])
        ng = jnp.tile(hg_norm[l], N_HEADS).reshape(1, HG_W)
        w_out_l, w_up_l, w_down_l = w_out[l].astype(BF16), w_up[l].astype(BF16), w_down[l].astype(BF16)
        g_ffn = ffn_norm[l].reshape(1, d)
        cb = conv_b[l].reshape(1, d_ff)
        last = l == depth - 1
        final_g = final_norm.reshape(1, d) if last else None

        (hg4, sq, skp, svp, skt, svb, cq, ckp, cvp, ckt, cvb, iq, ikt, ikp, iw) = _inproj(
            xp, g_attn, w_in_l, cos_p, sin_p, block_rows, page=page)
        o_a, hg_new = _hgrn(hg4, lb_terms, ng, jnp.zeros((1, N_HEADS, HG_D, HG_D), F32), 1, seq, block_rows, 16)
        o_b = _sb_prompt(sq, skt, svb, tq)
        o_c = _dsa_prompt(iq, iw, cq, ikt, ckt, cvb, tq, topk_p)
        xp, tail = _ffn(xp, o_a, o_b, o_c, w_out_l, g_ffn, w_up_l, conv_w[l], cb, w_down_l, block_rows, final_g=final_g)
        p_rows.append((skp, svp, ckp, cvp, ikp, hg_new, tail[8 - (conv_w.shape[1] - 1):]))

        (hg4, sq, sk, sv, cq, ck, cv, iq, ik, iw) = _inproj(xs, g_attn, w_in_l, cos_s, sin_s, rows_s)
        o_a, hg_new = _hgrn(hg4, lb_terms, ng, state_hgrn[l], nb_s, seq_s, seq_s, seq_s)
        b3 = lambda a: a.reshape(nb_s, seq_s, a.shape[-1])
        new = lambda a: _new_page(b3(a), page)
        o_b = _sb_sample(page_table, b3(sq), cache_sb_k, cache_sb_v, new(sk), new(sv), l)
        keys, vstar, need = _dsa_score_sample(page_table, b3(iq), b3(iw), cache_idx_k, new(ik), l, topk_s)
        o_c = _dsa_attend_sample(page_table, b3(cq), keys, vstar, need, cache_dsa_k, cache_dsa_v, new(ck), new(cv), l)
        prev = (jnp.repeat(state_conv[l][:, 0], seq_s, axis=0), jnp.repeat(state_conv[l][:, 1], seq_s, axis=0))
        xs, a_rows = _ffn(xs, o_a, o_b.reshape(rows_s, ATT_W), o_c.reshape(rows_s, ATT_W), w_out_l, g_ffn, w_up_l,
                          conv_w[l], cb, w_down_l, rows_s, prev=prev, final_g=final_g)
        conv_new = a_rows.reshape(nb_s, seq_s, d_ff)[:, seq_s - (conv_w.shape[1] - 1):]
        s_rows.append((sk, sv, ck, cv, ik, hg_new, conv_new))

    stack = lambda rows, j: jnp.stack([r[j] for r in rows])
    heads = lambda a: _slot_major(a, (1,))
    heads_s = lambda a: a.reshape(depth, nb_s, seq_s, N_HEADS, HEAD_DIM)
    return (xp.reshape(1, seq, d), xs.reshape(nb_s, seq_s, d),
            heads(stack(p_rows, 0)), heads(stack(p_rows, 1)), heads(stack(p_rows, 2)), heads(stack(p_rows, 3)),
            heads(stack(p_rows, 4)),
            stack(p_rows, 5), stack(p_rows, 6).reshape(depth, 1, conv_w.shape[1] - 1, d_ff),
            heads_s(stack(s_rows, 0)), heads_s(stack(s_rows, 1)), heads_s(stack(s_rows, 2)), heads_s(stack(s_rows, 3)),
            stack(s_rows, 4).reshape(depth, nb_s, seq_s, HEAD_DIM),
            stack(s_rows, 5), stack(s_rows, 6))
```

```python
import functools
import math

import jax
import jax.numpy as jnp
from jax import lax
from jax.experimental import pallas as pl
from jax.experimental.pallas import tpu as pltpu

F32 = jnp.float32
BF16 = jnp.bfloat16
I32 = jnp.int32
I16 = jnp.int16

HEAD_DIM = 64
N_HEADS = 4
HG_D = 128
ATT_W = N_HEADS * HEAD_DIM
HG_W = N_HEADS * HG_D
TOPK_MAX = 256
ROPE_THETA = 10000.0
EPS = 1e-6
NEG = -1e30
INT_MIN = -2147483648
I16_MIN = -32768
VMEM_LIMIT_BYTES = 56 * 1024 * 1024
LANES = 128

_C_HG = 0
_C_SB = 4 * HG_W
_C_DSA = _C_SB + 3 * ATT_W
_C_IQ = _C_DSA + 3 * ATT_W
_C_IK = _C_IQ + ATT_W
_C_IW = _C_IK + LANES
IN_PAD = _C_IW + LANES


def _const_spec(shape):
    return pl.BlockSpec(shape, lambda *_: (0,) * len(shape), pipeline_mode=pl.Buffered(1))


def _params(sem):
    return pltpu.CompilerParams(dimension_semantics=sem, vmem_limit_bytes=VMEM_LIMIT_BYTES)


def _dot(a, b):
    return jnp.dot(a, b, preferred_element_type=F32)


def _dot_nt(a, b):
    return lax.dot_general(a, b, (((1,), (1,)), ((), ())), preferred_element_type=F32)


def _loop_pairs(lo, hi, fn, carry):
    n = hi - lo

    def two(i, c):
        k = lo + 2 * i
        return fn(k + 1, fn(k, c))

    carry = lax.fori_loop(0, n // 2, two, carry)
    return lax.cond(n % 2 == 1, lambda c: fn(hi - 1, c), lambda c: c, carry)


def _rmsnorm(x, g):
    return x * lax.rsqrt(jnp.mean(x * x, axis=-1, keepdims=True) + EPS) * g


def _softplus(z):
    return jnp.maximum(z, 0.0) + jnp.log(1.0 + jnp.exp(-jnp.abs(z)))


def _head_rows(q):
    return jnp.concatenate([q[:, h * HEAD_DIM:(h + 1) * HEAD_DIM] for h in range(N_HEADS)], axis=0)


def _head_stack(q):
    lane = lax.broadcasted_iota(I32, q.shape, 1)
    zero = jnp.zeros_like(q)
    return jnp.concatenate([jnp.where(lane // HEAD_DIM == h, q, zero) for h in range(N_HEADS)], axis=0)


def _head_merge(acc, t):
    lane = lax.broadcasted_iota(I32, (t, ATT_W), 1)
    out = jnp.zeros((t, ATT_W), F32)
    for h in range(N_HEADS):
        out = jnp.where(lane // HEAD_DIM == h, acc[h * t:(h + 1) * t], out)
    return out


def _rope(x, cos, sin):
    lane = lax.broadcasted_iota(I32, x.shape, 1)
    first = (lane % HEAD_DIM) < (HEAD_DIM // 2)
    w = x.shape[1]
    swapped = jnp.where(first, pltpu.roll(x, w - HEAD_DIM // 2, 1), pltpu.roll(x, HEAD_DIM // 2, 1))
    return x * cos + swapped * sin


def _inproj_body(x_ref, g_ref, w_ref, cos_ref, sin_ref, *out_refs, page):
    h = _rmsnorm(x_ref[...], g_ref[...]).astype(BF16)
    p = _dot(h, w_ref[...])
    cos, sin = cos_ref[...], sin_ref[...]
    scale = HEAD_DIM ** -0.5
    hg = p[:, _C_HG:_C_SB]
    sq = (p[:, _C_SB:_C_SB + ATT_W] * scale).astype(BF16)
    sk = p[:, _C_SB + ATT_W:_C_SB + 2 * ATT_W]
    sv = p[:, _C_SB + 2 * ATT_W:_C_DSA]
    cq = (_rope(p[:, _C_DSA:_C_DSA + ATT_W], cos, sin) * scale).astype(BF16)
    ck = _rope(p[:, _C_DSA + ATT_W:_C_DSA + 2 * ATT_W], cos, sin)
    cv = p[:, _C_DSA + 2 * ATT_W:_C_IQ]
    iq = (_rope(p[:, _C_IQ:_C_IK], cos, sin) * scale).astype(BF16)
    ik = _rope(p[:, _C_IK:_C_IW], cos[:, :LANES], sin[:, :LANES])
    if page is None:
        (hg_ref, sq_ref, sk_ref, sv_ref, cq_ref, ck_ref, cv_ref, iq_ref, ik_ref, iw_ref) = out_refs
        sk_ref[...] = sk
        sv_ref[...] = sv
        ck_ref[...] = ck
        cv_ref[...] = cv
        ik_ref[...] = ik[:, :HEAD_DIM]
    else:
        (hg_ref, sq_ref, skp_ref, svp_ref, skt_ref, svb_ref, cq_ref, ckp_ref, cvp_ref, ckt_ref, cvo_ref,
         iq_ref, ikt_ref, ikp_ref, iw_ref) = out_refs
        skt, svt, ckt, cvt, ikt = sk.T, sv.T, ck.T, cv.T, ik.T[:HEAD_DIM]
        for pg in range(p.shape[0] // page):
            sl = slice(pg * page, (pg + 1) * page)
            skp_ref[pg] = skt[:, sl]
            svp_ref[pg] = svt[:, sl]
            ckp_ref[pg] = ckt[:, sl]
            cvp_ref[pg] = cvt[:, sl]
            ikp_ref[pg] = ikt[:, sl]
        skt_ref[0] = skt.astype(BF16)
        ckt_ref[0] = ckt.astype(BF16)
        ikt_ref[0] = ikt.astype(BF16)
        svb_ref[...] = sv.astype(BF16)
        cvb = cv.astype(BF16)
        ones = jnp.ones((cvb.shape[0], HEAD_DIM), BF16)
        pieces = []
        for hd in range(N_HEADS):
            pieces += [cvb[:, hd * HEAD_DIM:(hd + 1) * HEAD_DIM], ones]
        cvo_ref[...] = jnp.concatenate(pieces, axis=1)
    hg_ref[...] = hg
    sq_ref[...] = sq
    cq_ref[...] = cq
    iq_ref[...] = iq
    iw_ref[...] = p[:, _C_IW:]


def _inproj(x, g, w_pad, cos, sin, block_rows, page=None):
    rows, d = x.shape
    nblk = rows // block_rows
    row = lambda w: pl.BlockSpec((block_rows, w), lambda i: (i, 0))
    f32o = lambda w: (jax.ShapeDtypeStruct((rows, w), F32), row(w))
    bfo = lambda w: (jax.ShapeDtypeStruct((rows, w), BF16), row(w))
    if page is None:
        outs = [f32o(4 * HG_W), bfo(ATT_W), f32o(ATT_W), f32o(ATT_W), bfo(ATT_W), f32o(ATT_W), f32o(ATT_W),
                bfo(ATT_W), f32o(HEAD_DIM), f32o(LANES)]
    else:
        ppb = block_rows // page
        pages = lambda w: (jax.ShapeDtypeStruct((rows // page, w, page), F32),
                           pl.BlockSpec((ppb, w, page), lambda i: (i, 0, 0)))
        tiles = lambda w: (jax.ShapeDtypeStruct((nblk, w, block_rows), BF16),
                           pl.BlockSpec((1, w, block_rows), lambda i: (i, 0, 0)))
        outs = [f32o(4 * HG_W), bfo(ATT_W), pages(ATT_W), pages(ATT_W), tiles(ATT_W), bfo(ATT_W),
                bfo(ATT_W), pages(ATT_W), pages(ATT_W), tiles(ATT_W), bfo(2 * ATT_W),
                bfo(ATT_W), tiles(HEAD_DIM), pages(HEAD_DIM), f32o(LANES)]
    return pl.pallas_call(
        functools.partial(_inproj_body, page=page), grid=(nblk,),
        in_specs=[row(d), _const_spec((1, d)), _const_spec(w_pad.shape), row(ATT_W), row(ATT_W)],
        out_specs=tuple(o[1] for o in outs), out_shape=tuple(o[0] for o in outs),
        compiler_params=_params(("arbitrary",)),
    )(x, g, w_pad, cos, sin)


def _hgrn_body(hq_ref, hf_ref, hi_ref, hgate_ref, loglb_ref, log1mlb_ref, omlb_ref, ng_ref, s0_ref,
               o_ref, s_out_ref, st_ref, *, chunk, n_chunks):
    t = pl.program_id(1)

    @pl.when(t == 0)
    def _():
        for h in range(N_HEADS):
            st_ref[h] = s0_ref[0, h].T

    c = chunk
    ri = lax.broadcasted_iota(I32, (c, c), 0)
    ci = lax.broadcasted_iota(I32, (c, c), 1)
    tri = (ri >= ci).astype(F32)
    row = lax.broadcasted_iota(I32, (c, HG_W), 0)
    loglb, log1mlb, omlb, ng = loglb_ref[...], log1mlb_ref[...], omlb_ref[...], ng_ref[...]

    def one_chunk(ic, carry):
        r0 = pl.multiple_of(ic * c, c)
        q = hq_ref[pl.ds(r0, c), :]
        x = hf_ref[pl.ds(r0, c), :]
        v = hi_ref[pl.ds(r0, c), :]
        gate = hgate_ref[pl.ds(r0, c), :]
        qf = q * jax.nn.sigmoid(q)
        log_sig = jnp.minimum(x, 0.0) - jnp.log1p(jnp.exp(-jnp.abs(x)))
        bterm = log1mlb + log_sig
        log_f = jnp.maximum(loglb, bterm) + jnp.log1p(jnp.exp(-jnp.abs(loglb - bterm)))
        kf = omlb * jax.nn.sigmoid(-x)
        b = jnp.dot(tri, log_f, preferred_element_type=F32, precision=lax.Precision.HIGHEST)
        b_last = b[c - 1:c, :]
        qe = qf * jnp.exp(b)
        kk = kf * jnp.exp(b_last - b)
        e_last = jnp.exp(b_last)
        o_intra = [jnp.zeros((c, HG_D), F32) for _ in range(N_HEADS)]
        for s in range(c):
            w = jnp.exp(jnp.where(row >= s, b - b[s:s + 1, :], -jnp.inf)) * kf[s:s + 1, :]
            pw = qf * w
            for h in range(N_HEADS):
                sl = slice(h * HG_D, (h + 1) * HG_D)
                attn = jnp.sum(pw[:, sl], axis=1, keepdims=True)
                o_intra[h] = o_intra[h] + attn * v[s:s + 1, sl]
        outs = []
        for h in range(N_HEADS):
            sl = slice(h * HG_D, (h + 1) * HG_D)
            st = st_ref[h]
            o = o_intra[h] + lax.dot_general(qe[:, sl], st, (((1,), (1,)), ((), ())),
                                             preferred_element_type=F32, precision=lax.Precision.HIGHEST)
            upd = lax.dot_general(v[:, sl], kk[:, sl], (((0,), (0,)), ((), ())),
                                  preferred_element_type=F32, precision=lax.Precision.HIGHEST)
            st_ref[h] = st * e_last[:, sl] + upd
            o = o * lax.rsqrt(jnp.mean(o * o, axis=1, keepdims=True) + EPS)
            outs.append(o)
        o_all = jnp.concatenate(outs, axis=1) * ng * (gate * jax.nn.sigmoid(gate))
        o_ref[pl.ds(r0, c), :] = o_all
        return carry

    lax.fori_loop(0, n_chunks, one_chunk, 0)

    @pl.when(t == pl.num_programs(1) - 1)
    def _():
        for h in range(N_HEADS):
            s_out_ref[0, h] = st_ref[h].T


def _hgrn(hg4, lb_terms, ng, s0, batch, seq, block_rows, chunk):
    nt = seq // block_rows
    col = lambda cidx: pl.BlockSpec((block_rows, HG_W), lambda b, t: (b * nt + t, cidx))
    loglb, log1mlb, omlb = lb_terms
    body = functools.partial(_hgrn_body, chunk=chunk, n_chunks=block_rows // chunk)
    return pl.pallas_call(
        body, grid=(batch, nt),
        in_specs=[col(0), col(1), col(2), col(3),
                  _const_spec((1, HG_W)), _const_spec((1, HG_W)), _const_spec((1, HG_W)), _const_spec((1, HG_W)),
                  pl.BlockSpec((1, N_HEADS, HG_D, HG_D), lambda b, t: (b, 0, 0, 0))],
        out_specs=(pl.BlockSpec((block_rows, HG_W), lambda b, t: (b * nt + t, 0)),
                   pl.BlockSpec((1, N_HEADS, HG_D, HG_D), lambda b, t: (b, 0, 0, 0))),
        out_shape=(jax.ShapeDtypeStruct((batch * seq, HG_W), F32),
                   jax.ShapeDtypeStruct((batch, N_HEADS, HG_D, HG_D), F32)),
        scratch_shapes=[pltpu.VMEM((N_HEADS, HG_D, HG_D), F32)],
        compiler_params=_params(("arbitrary", "arbitrary")),
    )(hg4, hg4, hg4, hg4, loglb, log1mlb, omlb, ng, s0)


def _pages_per_step(n_pages):
    return math.gcd(n_pages, 16)


def _group_width(pps, page):
    return 2 * page if pps % 2 == 0 else page


def _chunk(refs):
    return jnp.concatenate([r[...].astype(BF16) for r in refs], axis=1)


def _page_specs(layer, n_pages, pps, page, width, reverse):
    nc = n_pages // pps

    def spec(i):
        def index(b, j, pt):
            chunk = (nc - 1 - j) if reverse else j
            return (layer, pt[b, chunk * pps + i], 0, 0)
        return pl.BlockSpec((None, None, width, page), index)
    return [spec(i) for i in range(pps)]


def _per_batch(shape):
    return pl.BlockSpec((1,) + shape, lambda b, j, pt: (b,) + (0,) * len(shape))


def _suffix_matrix(tk):
    j = lax.broadcasted_iota(I32, (tk, tk), 0)
    s = lax.broadcasted_iota(I32, (tk, tk), 1)
    return (j > s).astype(BF16)


def _sb_weights(z, visible, carry, suffix):
    gw = suffix.shape[0]
    groups = z.shape[1] // gw
    log_keep = -_softplus(z)
    if visible is not None:
        log_keep = jnp.where(visible, log_keep, 0.0)
    within, total = [], []
    for g in range(groups):
        lk = log_keep[:, g * gw:(g + 1) * gw]
        hi = lk.astype(BF16)
        lo = (lk - hi.astype(F32)).astype(BF16)
        ex = _dot(hi, suffix) + _dot(lo, suffix)
        within.append(ex)
        total.append(ex[:, 0:1] + lk[:, 0:1])
    after = [None] * groups
    for g in reversed(range(groups)):
        after[g] = within[g] + carry
        carry = carry + total[g]
    after = after[0] if groups == 1 else jnp.concatenate(after, axis=1)
    a = jnp.exp(z + log_keep + after)
    if visible is not None:
        a = jnp.where(visible, a, 0.0)
    return a.astype(BF16), carry


def _sb_prompt_body(q_ref, kt_ref, v_ref, o_ref, acc_ref, *, tq, tk):
    qi = pl.program_id(0)
    qs = _head_stack(q_ref[...])
    suffix = _suffix_matrix(tk)
    rows = N_HEADS * tq
    row_pos = qi * tq + lax.broadcasted_iota(I32, (rows, tk), 0) % tq
    col = lax.broadcasted_iota(I32, (rows, tk), 1)
    acc_ref[...] = jnp.zeros_like(acc_ref)

    def tile(kb, carry, masked):
        ks = pl.multiple_of(kb * tk, tk)
        visible = (col + kb * tk < row_pos) if masked else None
        a, carry = _sb_weights(_dot(qs, kt_ref[kb]), visible, carry, suffix)
        acc_ref[...] += _dot(a, v_ref[pl.ds(ks, tk), :])
        return carry

    kb_last = (qi * tq + tq - 1) // tk
    carry = tile(kb_last, jnp.zeros((rows, 1), F32), True)
    n_full = (qi * tq) // tk

    def partial_tiles(j, carry):
        return tile(kb_last - 1 - j, carry, True)

    carry = lax.fori_loop(0, kb_last - n_full, partial_tiles, carry)

    def full_tiles(j, carry):
        return tile(n_full - 1 - j, carry, False)

    _loop_pairs(0, n_full, full_tiles, carry)
    o_ref[...] = _head_merge(acc_ref[...], tq)


def _sb_prompt(q, kt, v, tq):
    t = q.shape[0]
    tk = kt.shape[2]
    body = functools.partial(_sb_prompt_body, tq=tq, tk=tk)
    return pl.pallas_call(
        body, grid=(t // tq,),
        in_specs=[pl.BlockSpec((tq, ATT_W), lambda i: (i, 0)), _const_spec(kt.shape), _const_spec((t, ATT_W))],
        out_specs=pl.BlockSpec((tq, ATT_W), lambda i: (i, 0)),
        out_shape=jax.ShapeDtypeStruct((t, ATT_W), F32),
        scratch_shapes=[pltpu.VMEM((N_HEADS * tq, ATT_W), F32)],
        compiler_params=_params(("arbitrary",)),
    )(q, kt, v)


def _sb_sample_body(pt_ref, q_ref, *refs, pps, page):
    kp, vp = refs[:pps], refs[pps:2 * pps]
    kn_ref, vn_ref, o_ref, acc_ref, carry_ref = refs[2 * pps:]
    j = pl.program_id(1)
    tq = q_ref.shape[1]
    rows = N_HEADS * tq
    qs = _head_stack(q_ref[0])

    @pl.when(j == 0)
    def _():
        z = _dot(qs, kn_ref[0].astype(BF16))
        row = lax.broadcasted_iota(I32, (rows, page), 0) % tq
        col = lax.broadcasted_iota(I32, (rows, page), 1)
        a, carry = _sb_weights(z, col < row, jnp.zeros((rows, 1), F32), _suffix_matrix(page))
        acc_ref[...] = _dot_nt(a, vn_ref[0].astype(BF16))
        carry_ref[...] = carry

    z = _dot(qs, _chunk(kp))
    a, carry = _sb_weights(z, None, carry_ref[...], _suffix_matrix(_group_width(pps, page)))
    acc_ref[...] += _dot_nt(a, _chunk(vp))
    carry_ref[...] = carry

    @pl.when(j == pl.num_programs(1) - 1)
    def _():
        o_ref[0] = _head_merge(acc_ref[...], tq)


def _sb_sample(page_table, q, cache_k, cache_v, k_new, v_new, layer):
    nb, tq, _ = q.shape
    n_pages = page_table.shape[1]
    page = cache_k.shape[3]
    pps = _pages_per_step(n_pages)
    body = functools.partial(_sb_sample_body, pps=pps, page=page)
    grid_spec = pltpu.PrefetchScalarGridSpec(
        num_scalar_prefetch=1, grid=(nb, n_pages // pps),
        in_specs=[_per_batch((tq, ATT_W))] + _page_specs(layer, n_pages, pps, page, ATT_W, True)
        + _page_specs(layer, n_pages, pps, page, ATT_W, True) + [_per_batch((ATT_W, page)), _per_batch((ATT_W, page))],
        out_specs=_per_batch((tq, ATT_W)),
        scratch_shapes=[pltpu.VMEM((N_HEADS * tq, ATT_W), F32), pltpu.VMEM((N_HEADS * tq, 1), F32)])
    return pl.pallas_call(
        body, grid_spec=grid_spec, out_shape=jax.ShapeDtypeStruct((nb, tq, ATT_W), F32),
        compiler_params=_params(("arbitrary", "arbitrary")),
    )(page_table, q, *([cache_k] * pps), *([cache_v] * pps), k_new, v_new)


def _score_keys(dots4, iw_b, tq, visible):
    score = jnp.zeros((tq, dots4.shape[1]), F32)
    for h in range(N_HEADS):
        score = score + jnp.maximum(dots4[h * tq:(h + 1) * tq], 0.0) * iw_b[h]
    score = score + 0.0
    if visible is not None:
        score = jnp.where(visible, score, NEG)
    bits = pltpu.bitcast(score, I32)
    return jnp.where(bits < 0, bits ^ 0x7FFFFFFF, bits)


def _kth_largest(count_ge, shape, k, bits):
    def step(i, v):
        cand = v | lax.shift_left(jnp.int32(1), bits - 1 - i)
        return jnp.where(count_ge(cand) >= k, cand, v)
    return lax.fori_loop(0, bits, step, jnp.zeros(shape, I32))


def _prefix_matrix(tk):
    j = lax.broadcasted_iota(I32, (tk, tk), 0)
    s = lax.broadcasted_iota(I32, (tk, tk), 1)
    return (j < s).astype(BF16)


def _additive_mask(key, vstar, need, eq_carry, prefix, visible):
    gw = prefix.shape[0]
    eq = key == vstar
    eqf = jnp.where(eq, 1.0, 0.0)
    before = []
    for g in range(key.shape[1] // gw):
        e = eqf[:, g * gw:(g + 1) * gw]
        before.append(_dot(e.astype(BF16), prefix) + eq_carry)
        eq_carry = eq_carry + jnp.sum(e, axis=1, keepdims=True)
    before = before[0] if len(before) == 1 else jnp.concatenate(before, axis=1)
    am = jnp.where(key > vstar, 0.0, jnp.where(eq, jnp.where(before < need, 0.0, NEG), NEG))
    if visible is not None:
        am = jnp.where(visible, am, NEG)
    return am, eq_carry


def _dsa_prompt_body(iq_ref, iw_ref, cq_ref, ikt_ref, ckt_ref, cvo_ref, o_ref,
                     keys_ref, half_ref, mx_ref, acc_ref, *, tq, tk, topk):
    qi = pl.program_id(0)
    n_tiles = (qi * tq + tq - 1) // tk + 1
    n_full = (qi * tq + 1) // tk
    row_pos = qi * tq + lax.broadcasted_iota(I32, (tq, tk), 0)
    col = lax.broadcasted_iota(I32, (tq, tk), 1)

    iq4 = _head_rows(iq_ref[...])
    iw = iw_ref[...]
    iw_b = [jnp.broadcast_to(iw[:, h:h + 1], (tq, tk)) for h in range(N_HEADS)]

    def score_tile(kb, masked):
        visible = (col + kb * tk <= row_pos) if masked else None
        key = _score_keys(_dot(iq4, ikt_ref[kb]), iw_b, tq, visible)
        keys_ref[kb] = key
        half_ref[kb] = lax.shift_right_arithmetic(key, 16).astype(I16)

    def score_full(kb, c):
        score_tile(kb, False)
        return c

    def score_part(kb, c):
        score_tile(kb, True)
        return c

    _loop_pairs(0, n_full, score_full, 0)
    lax.fori_loop(n_full, n_tiles, score_part, 0)

    def count16(pred):
        def add(kb, cnt):
            return cnt + jnp.where(pred(half_ref[kb]), jnp.int16(1), jnp.int16(0))
        cnt = _loop_pairs(0, n_tiles, add, jnp.zeros((tq, tk), I16))
        return jnp.sum(cnt.astype(I32), axis=1, keepdims=True)

    def half_b(v):
        return jnp.broadcast_to(v, (tq, tk)).astype(I16)

    def kth16(k):
        def count_ge(cand):
            cand_b = half_b(cand + I16_MIN)
            return count16(lambda x: x >= cand_b)
        return _kth_largest(count_ge, (tq, 1), k, 16) + I16_MIN

    vh = kth16(topk)
    vh_b = half_b(vh)
    k_low = topk - count16(lambda x: x > vh_b)
    vh_b32 = jnp.broadcast_to(vh, (tq, tk))

    def fill_low(kb, c):
        key = keys_ref[kb]
        low = (key & 0xFFFF) + I16_MIN
        in_group = lax.shift_right_arithmetic(key, 16) == vh_b32
        half_ref[kb] = jnp.where(in_group, low, I16_MIN).astype(I16)
        return c

    lax.fori_loop(0, n_tiles, fill_low, 0)
    vl = kth16(k_low)
    vl_b = half_b(vl)
    need = (k_low - count16(lambda x: x > vl_b)).astype(F32)
    vstar = lax.shift_left(vh, 16) | (vl - I16_MIN)
    vstar_b = jnp.broadcast_to(vstar, (tq, tk))
    need_b = jnp.broadcast_to(need, (tq, tk))

    cqs = _head_stack(cq_ref[...])
    prefix = _prefix_matrix(tk)
    mx_ref[...] = jnp.full_like(mx_ref, NEG)

    def select(kb, eq_carry, masked):
        visible = (col + kb * tk <= row_pos) if masked else None
        am, eq_carry = _additive_mask(keys_ref[kb], vstar_b, need_b, eq_carry, prefix, visible)
        keys_ref[kb] = pltpu.bitcast(am, I32)
        logits4 = _dot(cqs, ckt_ref[kb])
        for h in range(N_HEADS):
            lg = logits4[h * tq:(h + 1) * tq] + am
            top = lg[:, :LANES]
            for g in range(1, tk // LANES):
                top = jnp.maximum(top, lg[:, g * LANES:(g + 1) * LANES])
            mx_ref[h] = jnp.maximum(mx_ref[h], top)
        return eq_carry

    eq_carry = _loop_pairs(0, n_full, lambda kb, c: select(kb, c, False), jnp.zeros((tq, 1), F32))
    lax.fori_loop(n_full, n_tiles, lambda kb, c: select(kb, c, True), eq_carry)

    m_b = [jnp.broadcast_to(jnp.max(mx_ref[h], axis=1, keepdims=True), (tq, tk)) for h in range(N_HEADS)]
    acc_ref[...] = jnp.zeros_like(acc_ref)

    def attend(kb, c):
        ks = pl.multiple_of(kb * tk, tk)
        am = pltpu.bitcast(keys_ref[kb], F32)
        logits4 = _dot(cqs, ckt_ref[kb])
        vo = cvo_ref[pl.ds(ks, tk), :]
        for h in range(N_HEADS):
            p = jnp.exp(logits4[h * tq:(h + 1) * tq] + am - m_b[h]).astype(BF16)
            acc_ref[h] += _dot(p, vo[:, h * 2 * HEAD_DIM:(h + 1) * 2 * HEAD_DIM])
        return c

    _loop_pairs(0, n_tiles, attend, 0)
    outs = []
    for h in range(N_HEADS):
        acc = acc_ref[h]
        outs.append(acc[:, :HEAD_DIM] / acc[:, HEAD_DIM:HEAD_DIM + 1])
    o_ref[...] = jnp.concatenate(outs, axis=1)


def _dsa_prompt(iq, iw, cq, ikt, ckt, cvo, tq, topk):
    t = iq.shape[0]
    tk = ckt.shape[2]
    body = functools.partial(_dsa_prompt_body, tq=tq, tk=tk, topk=topk)
    row = lambda w: pl.BlockSpec((tq, w), lambda i: (i, 0))
    return pl.pallas_call(
        body, grid=(t // tq,),
        in_specs=[row(ATT_W), row(LANES), row(ATT_W),
                  _const_spec(ikt.shape), _const_spec(ckt.shape), _const_spec(cvo.shape)],
        out_specs=row(ATT_W),
        out_shape=jax.ShapeDtypeStruct((t, ATT_W), F32),
        scratch_shapes=[pltpu.VMEM((t // tk, tq, tk), I32), pltpu.VMEM((t // tk, tq, tk), I16),
                        pltpu.VMEM((N_HEADS, tq, LANES), F32), pltpu.VMEM((N_HEADS, tq, 2 * HEAD_DIM), F32)],
        compiler_params=_params(("arbitrary",)),
    )(iq, iw, cq, ikt, ckt, cvo)


def _dsa_score_sample_body(pt_ref, iq_ref, iw_ref, *refs, pps, page, topk):
    ikp = refs[:pps]
    ikn_ref, keys_ref, keysn_ref, vstar_ref, need_ref, all_ref, alln_ref = refs[pps:]
    j = pl.program_id(1)
    nc = pl.num_programs(1)
    tq = iq_ref.shape[1]
    cw = pps * page
    iq4 = _head_rows(iq_ref[0])
    iw = iw_ref[0]
    iw_b = lambda n: [jnp.broadcast_to(iw[:, h:h + 1], (tq, n)) for h in range(N_HEADS)]
    key = _score_keys(_dot(iq4, _chunk(ikp)), iw_b(cw), tq, None)
    keys_ref[0, 0] = key
    all_ref[j] = key

    @pl.when(j == nc - 1)
    def _():
        row = lax.broadcasted_iota(I32, (tq, page), 0)
        col = lax.broadcasted_iota(I32, (tq, page), 1)
        key_n = _score_keys(_dot(iq4, ikn_ref[0].astype(BF16)), iw_b(page), tq, col <= row)
        keysn_ref[0] = key_n
        alln_ref[...] = key_n

        def count(pred):
            def add(c, cnt):
                return cnt + jnp.where(pred(all_ref[c]), 1, 0)
            cnt = lax.fori_loop(0, nc, add, jnp.zeros((tq, cw), I32))
            cnt_n = jnp.where(pred(alln_ref[...]), 1, 0)
            return jnp.sum(cnt, axis=1, keepdims=True) + jnp.sum(cnt_n, axis=1, keepdims=True)

        vstar = _kth_largest(lambda cand: count(lambda x: x >= (cand ^ INT_MIN)), (tq, 1), topk, 32) ^ INT_MIN
        need = (topk - count(lambda x: x > vstar)).astype(F32)
        vstar_ref[0] = jnp.broadcast_to(vstar, (tq, page))
        need_ref[0] = jnp.broadcast_to(need, (tq, page))


def _dsa_score_sample(page_table, iq, iw, cache_ik, ik_new, layer, topk):
    nb, tq, _ = iq.shape
    n_pages = page_table.shape[1]
    page = cache_ik.shape[3]
    pps = _pages_per_step(n_pages)
    nc, cw = n_pages // pps, pps * page
    body = functools.partial(_dsa_score_sample_body, pps=pps, page=page, topk=topk)
    grid_spec = pltpu.PrefetchScalarGridSpec(
        num_scalar_prefetch=1, grid=(nb, nc),
        in_specs=[_per_batch((tq, ATT_W)), _per_batch((tq, LANES))]
        + _page_specs(layer, n_pages, pps, page, HEAD_DIM, False) + [_per_batch((HEAD_DIM, page))],
        out_specs=(pl.BlockSpec((1, 1, tq, cw), lambda b, j, pt: (b, j, 0, 0)),
                   _per_batch((tq, page)), _per_batch((tq, page)), _per_batch((tq, page))),
        scratch_shapes=[pltpu.VMEM((nc, tq, cw), I32), pltpu.VMEM((tq, page), I32)])
    return pl.pallas_call(
        body, grid_spec=grid_spec,
        out_shape=(jax.ShapeDtypeStruct((nb, nc, tq, cw), I32), jax.ShapeDtypeStruct((nb, tq, page), I32),
                   jax.ShapeDtypeStruct((nb, tq, page), I32), jax.ShapeDtypeStruct((nb, tq, page), F32)),
        compiler_params=_params(("arbitrary", "arbitrary")),
    )(page_table, iq, iw, *([cache_ik] * pps), ik_new)


def _dsa_attend_sample_body(pt_ref, cq_ref, keys_ref, keysn_ref, vstar_ref, need_ref, *refs, pps, page):
    kp, vp = refs[:pps], refs[pps:2 * pps]
    kn_ref, vn_ref, o_ref, acc_ref, m_ref, l_ref, eq_ref = refs[2 * pps:]
    j = pl.program_id(1)
    tq = cq_ref.shape[1]

    @pl.when(j == 0)
    def _():
        acc_ref[...] = jnp.zeros_like(acc_ref)
        m_ref[...] = jnp.full_like(m_ref, NEG)
        l_ref[...] = jnp.zeros_like(l_ref)
        eq_ref[...] = jnp.zeros_like(eq_ref)

    cqs = _head_stack(cq_ref[0])
    vstar = vstar_ref[0][:, 0:1]
    need = need_ref[0][:, 0:1]

    def step(key, k_tile, v_tile, visible, gw):
        am, eq_carry = _additive_mask(key, vstar, need, eq_ref[...], _prefix_matrix(gw), visible)
        eq_ref[...] = eq_carry
        am4 = jnp.concatenate([am] * N_HEADS, axis=0)
        lg = _dot(cqs, k_tile) + am4
        m_old = m_ref[...]
        m_new = jnp.maximum(m_old, jnp.max(lg, axis=1, keepdims=True))
        p = jnp.where(am4 == 0.0, jnp.exp(lg - m_new), 0.0)
        alpha = jnp.exp(m_old - m_new)
        l_ref[...] = l_ref[...] * alpha + jnp.sum(p, axis=1, keepdims=True)
        acc_ref[...] = acc_ref[...] * alpha + _dot_nt(p.astype(BF16), v_tile)
        m_ref[...] = m_new

    step(keys_ref[0, 0], _chunk(kp), _chunk(vp), None, _group_width(pps, page))

    @pl.when(j == pl.num_programs(1) - 1)
    def _():
        row = lax.broadcasted_iota(I32, (tq, page), 0)
        col = lax.broadcasted_iota(I32, (tq, page), 1)
        step(keysn_ref[0], kn_ref[0].astype(BF16), vn_ref[0].astype(BF16), col <= row, page)
        o_ref[0] = _head_merge(acc_ref[...] / l_ref[...], tq)


def _dsa_attend_sample(page_table, cq, keys, keys_new, vstar, need, cache_k, cache_v, k_new, v_new, layer):
    nb, tq, _ = cq.shape
    n_pages = page_table.shape[1]
    page = cache_k.shape[3]
    pps = _pages_per_step(n_pages)
    nc, cw = n_pages // pps, pps * page
    body = functools.partial(_dsa_attend_sample_body, pps=pps, page=page)
    grid_spec = pltpu.PrefetchScalarGridSpec(
        num_scalar_prefetch=1, grid=(nb, nc),
        in_specs=[_per_batch((tq, ATT_W)),
                  pl.BlockSpec((1, 1, tq, cw), lambda b, j, pt: (b, j, 0, 0)),
                  _per_batch((tq, page)), _per_batch((tq, page)), _per_batch((tq, page))]
        + _page_specs(layer, n_pages, pps, page, ATT_W, False) + _page_specs(layer, n_pages, pps, page, ATT_W, False)
        + [_per_batch((ATT_W, page)), _per_batch((ATT_W, page))],
        out_specs=_per_batch((tq, ATT_W)),
        scratch_shapes=[pltpu.VMEM((N_HEADS * tq, ATT_W), F32),
                        pltpu.VMEM((N_HEADS * tq, 1), F32), pltpu.VMEM((N_HEADS * tq, 1), F32),
                        pltpu.VMEM((tq, 1), F32)])
    return pl.pallas_call(
        body, grid_spec=grid_spec, out_shape=jax.ShapeDtypeStruct((nb, tq, ATT_W), F32),
        compiler_params=_params(("arbitrary", "arbitrary")),
    )(page_table, cq, keys, keys_new, vstar, need, *([cache_k] * pps), *([cache_v] * pps), k_new, v_new)


def _ffn_body(*refs, seq_rows, carried, final):
    it = iter(refs)
    x_ref, oa_ref, ob_ref, oc_ref, wout_ref, g_ref, wup_ref, cw_ref, cb_ref, wdown_ref = (next(it) for _ in range(10))
    e0_ref = e1_ref = gf_ref = carry_ref = None
    if not carried:
        e0_ref, e1_ref = next(it), next(it)
    if final:
        gf_ref = next(it)
    xo_ref, tail_ref = next(it), next(it)
    if carried:
        carry_ref = next(it)

        @pl.when(pl.program_id(0) == 0)
        def _():
            carry_ref[...] = jnp.zeros_like(carry_ref)

    mixed = jnp.concatenate([oa_ref[...], ob_ref[...], oc_ref[...]], axis=1).astype(BF16)
    x1 = x_ref[...] + _dot(mixed, wout_ref[...])
    h = _rmsnorm(x1, g_ref[...]).astype(BF16)
    up = _dot(h, wup_ref[...])
    d_ff = up.shape[1] // 2
    a, gate = up[:, :d_ff], up[:, d_ff:]
    rows = a.shape[0]
    r = lax.broadcasted_iota(I32, a.shape, 0) % seq_rows
    if carried:
        nc = carry_ref.shape[0]
        e0 = jnp.broadcast_to(carry_ref[nc - 2:nc - 1, :], a.shape)
        e1 = jnp.broadcast_to(carry_ref[nc - 1:nc, :], a.shape)
    else:
        e0, e1 = e0_ref[...], e1_ref[...]
    a_m1 = jnp.where(r == 0, e1, pltpu.roll(a, 1, 0))
    a_m2 = jnp.where(r == 0, e0, jnp.where(r == 1, e1, pltpu.roll(a, 2, 0)))
    cw = cw_ref[...]
    c = cb_ref[...] + a_m2 * cw[0:1, :] + a_m1 * cw[1:2, :] + a * cw[2:3, :]
    act = (c * jax.nn.sigmoid(c) * gate).astype(BF16)
    x2 = x1 + _dot(act, wdown_ref[...])
    xo_ref[...] = _rmsnorm(x2, gf_ref[...]) if final else x2
    tail = tail_ref.shape[0]
    tail_ref[...] = a[rows - tail:, :]
    if carried:
        nc = carry_ref.shape[0]
        carry_ref[...] = a[rows - nc:, :]


def _ffn(x, oa, ob, oc, w_out, g, w_up, conv_w, conv_b, w_down, block_rows, prev=None, final_g=None):
    rows, d = x.shape
    d_ff = w_down.shape[0]
    carried = prev is None
    final = final_g is not None
    row = lambda w: pl.BlockSpec((block_rows, w), lambda i: (i, 0))
    ins = [x, oa, ob, oc, w_out, g, w_up, conv_w, conv_b, w_down]
    in_specs = [row(d), row(HG_W), row(ATT_W), row(ATT_W), _const_spec(w_out.shape), _const_spec((1, d)),
                _const_spec(w_up.shape), _const_spec(conv_w.shape), _const_spec((1, d_ff)), _const_spec(w_down.shape)]
    if not carried:
        ins += list(prev)
        in_specs += [row(d_ff), row(d_ff)]
    if final:
        ins.append(final_g)
        in_specs.append(_const_spec((1, d)))
    if carried:
        tail_rows, tail_spec = 8, pl.BlockSpec((8, d_ff), lambda i: (0, 0))
        scratch = [pltpu.VMEM((8, d_ff), F32)]
        seq_rows = block_rows
    else:
        tail_rows, tail_spec = rows, row(d_ff)
        scratch = []
        seq_rows = 8
    body = functools.partial(_ffn_body, seq_rows=seq_rows, carried=carried, final=final)
    return pl.pallas_call(
        body, grid=(rows // block_rows,),
        in_specs=in_specs, out_specs=(row(d), tail_spec),
        out_shape=(jax.ShapeDtypeStruct((rows, d), F32), jax.ShapeDtypeStruct((tail_rows, d_ff), F32)),
        scratch_shapes=scratch,
        compiler_params=_params(("arbitrary",)),
    )(*ins)


def _rope_tables(pos):
    half = HEAD_DIM // 2
    inv = jnp.exp(-math.log(ROPE_THETA) * jnp.arange(half, dtype=F32) / half)
    ang = pos.astype(F32)[:, None] * inv[None, :]
    cos, sin = jnp.cos(ang), jnp.sin(ang)
    cos = jnp.tile(jnp.concatenate([cos, cos], axis=1), (1, N_HEADS))
    sin = jnp.tile(jnp.concatenate([-sin, sin], axis=1), (1, N_HEADS))
    return cos, sin


def _pad_w_in(w):
    d = w.shape[0]
    ik = w[:, _C_IK:_C_IK + HEAD_DIM]
    iw = w[:, _C_IK + HEAD_DIM:]
    pad = lambda n: jnp.zeros((d, n), w.dtype)
    return jnp.concatenate([w[:, :_C_IK], ik, pad(LANES - HEAD_DIM), iw, pad(LANES - iw.shape[1])], axis=1).astype(BF16)


def _new_page(a, page):
    return jnp.swapaxes(jnp.pad(a, ((0, 0), (0, page - a.shape[1]), (0, 0))), 1, 2)


def _feature_major(cache):
    depth, pool, page = cache.shape[:3]
    perm = (0, 1, 3, 4, 2) if cache.ndim == 5 else (0, 1, 3, 2)
    return jnp.transpose(cache, perm).reshape(depth, pool, -1, page)


def _slot_major(pages, lead):
    depth, n_pg, w, page = pages.shape
    if w == HEAD_DIM:
        return jnp.transpose(pages, (0, 1, 3, 2)).reshape((depth,) + lead + (n_pg, page, w))
    a = pages.reshape(depth, n_pg, N_HEADS, HEAD_DIM, page)
    return jnp.transpose(a, (0, 1, 4, 2, 3)).reshape((depth,) + lead + (n_pg, page, N_HEADS, HEAD_DIM))


def kernel(x_prompt, x_sample, cache_sb_k, cache_sb_v, cache_dsa_k, cache_dsa_v, cache_idx_k, state_hgrn, state_conv, page_table, hg_lower_bounds, attn_norm, w_in, hg_norm, w_out, ffn_norm, w_up, conv_w, conv_b, w_down, final_norm):
    depth = w_in.shape[0]
    nb_p, seq, d = x_prompt.shape
    nb_s, seq_s, _ = x_sample.shape
    assert nb_p == 1, "the prompt group is one sequence"
    page = cache_sb_k.shape[2]
    n_pages = page_table.shape[1]
    past = n_pages * page
    d_ff = w_down.shape[1]
    rows_s = nb_s * seq_s

    lb = jnp.cumsum(jax.nn.softmax(hg_lower_bounds.astype(F32), axis=0), axis=0)
    lb = (lb - lb[0]).reshape(depth, 1, HG_W)
    cos_p, sin_p = _rope_tables(jnp.arange(seq, dtype=I32))
    cos_s, sin_s = _rope_tables(jnp.tile(past + jnp.arange(seq_s, dtype=I32), nb_s))

    cache_sb_k, cache_sb_v = _feature_major(cache_sb_k), _feature_major(cache_sb_v)
    cache_dsa_k, cache_dsa_v = _feature_major(cache_dsa_k), _feature_major(cache_dsa_v)
    cache_idx_k = _feature_major(cache_idx_k)

    tq = 128
    block_rows = 256
    topk_p = min(TOPK_MAX, seq // 4)
    topk_s = min(TOPK_MAX, (past + seq_s) // 4)

    xp = x_prompt.reshape(seq, d)
    xs = x_sample.reshape(rows_s, d)
    p_rows, s_rows = [], []
    for l in range(depth):
        w_in_l = _pad_w_in(w_in[l])
        g_attn = attn_norm[l].reshape(1, d)
        lb_terms = (jnp.log(lb[l]), jnp.log1p(-lb[l]), 1.0 - lb[l])
        ng = jnp.tile(hg_norm[l], N_HEADS).reshape(1, HG_W)
        w_out_l, w_up_l, w_down_l = w_out[l].astype(BF16), w_up[l].astype(BF16), w_down[l].astype(BF16)
        g_ffn = ffn_norm[l].reshape(1, d)
        cb = conv_b[l].reshape(1, d_ff)
        last = l == depth - 1
        final_g = final_norm.reshape(1, d) if last else None

        (hg4, sq, skp, svp, skt, svb, cq, ckp, cvp, ckt, cvo, iq, ikt, ikp, iw) = _inproj(
            xp, g_attn, w_in_l, cos_p, sin_p, block_rows, page=page)
        o_a, hg_new = _hgrn(hg4, lb_terms, ng, jnp.zeros((1, N_HEADS, HG_D, HG_D), F32), 1, seq, block_rows, 16)
        o_b = _sb_prompt(sq, skt, svb, tq)
        o_c = _dsa_prompt(iq, iw, cq, ikt, ckt, cvo, tq, topk_p)
        xp, tail = _ffn(xp, o_a, o_b, o_c, w_out_l, g_ffn, w_up_l, conv_w[l], cb, w_down_l, block_rows, final_g=final_g)
        p_rows.append((skp, svp, ckp, cvp, ikp, hg_new, tail[8 - (conv_w.shape[1] - 1):]))

        (hg4, sq, sk, sv, cq, ck, cv, iq, ik, iw) = _inproj(xs, g_attn, w_in_l, cos_s, sin_s, rows_s)
        o_a, hg_new = _hgrn(hg4, lb_terms, ng, state_hgrn[l], nb_s, seq_s, seq_s, seq_s)
        b3 = lambda a: a.reshape(nb_s, seq_s, a.shape[-1])
        new = lambda a: _new_page(b3(a), page)
        o_b = _sb_sample(page_table, b3(sq), cache_sb_k, cache_sb_v, new(sk), new(sv), l)
        keys, keys_new, vstar, need = _dsa_score_sample(page_table, b3(iq), b3(iw), cache_idx_k, new(ik), l, topk_s)
        o_c = _dsa_attend_sample(page_table, b3(cq), keys, keys_new, vstar, need, cache_dsa_k, cache_dsa_v,
                                 new(ck), new(cv), l)
        prev = (jnp.repeat(state_conv[l][:, 0], seq_s, axis=0), jnp.repeat(state_conv[l][:, 1], seq_s, axis=0))
        xs, a_rows = _ffn(xs, o_a, o_b.reshape(rows_s, ATT_W), o_c.reshape(rows_s, ATT_W), w_out_l, g_ffn, w_up_l,
                          conv_w[l], cb, w_down_l, rows_s, prev=prev, final_g=final_g)
        conv_new = a_rows.reshape(nb_s, seq_s, d_ff)[:, seq_s - (conv_w.shape[1] - 1):]
        s_rows.append((sk, sv, ck, cv, ik, hg_new, conv_new))

    stack = lambda rows, j: jnp.stack([r[j] for r in rows])
    heads = lambda a: _slot_major(a, (1,))
    heads_s = lambda a: a.reshape(depth, nb_s, seq_s, N_HEADS, HEAD_DIM)
    return (xp.reshape(1, seq, d), xs.reshape(nb_s, seq_s, d),
            heads(stack(p_rows, 0)), heads(stack(p_rows, 1)), heads(stack(p_rows, 2)), heads(stack(p_rows, 3)),
            heads(stack(p_rows, 4)),
            stack(p_rows, 5), stack(p_rows, 6).reshape(depth, 1, conv_w.shape[1] - 1, d_ff),
            heads_s(stack(s_rows, 0)), heads_s(stack(s_rows, 1)), heads_s(stack(s_rows, 2)), heads_s(stack(s_rows, 3)),
            stack(s_rows, 4).reshape(depth, nb_s, seq_s, HEAD_DIM),
            stack(s_rows, 5), stack(s_rows, 6))
```

```python
import functools
import math

import jax
import jax.numpy as jnp
from jax import lax
from jax.experimental import pallas as pl
from jax.experimental.pallas import tpu as pltpu

F32 = jnp.float32
BF16 = jnp.bfloat16
I32 = jnp.int32
I16 = jnp.int16

HEAD_DIM = 64
N_HEADS = 4
HG_D = 128
ATT_W = N_HEADS * HEAD_DIM
HG_W = N_HEADS * HG_D
TOPK_MAX = 256
ROPE_THETA = 10000.0
EPS = 1e-6
NEG = -1e30
LOG2E = 1.4426950408889634
INT_MIN = -2147483648
I16_MIN = -32768
VMEM_LIMIT_BYTES = 56 * 1024 * 1024
LANES = 128

_C_HG = 0
_C_SB = 4 * HG_W
_C_DSA = _C_SB + 3 * ATT_W
_C_IQ = _C_DSA + 3 * ATT_W
_C_IK = _C_IQ + ATT_W
_C_IW = _C_IK + LANES
IN_PAD = _C_IW + LANES


def _const_spec(shape):
    return pl.BlockSpec(shape, lambda *_: (0,) * len(shape), pipeline_mode=pl.Buffered(1))


def _params(sem):
    return pltpu.CompilerParams(dimension_semantics=sem, vmem_limit_bytes=VMEM_LIMIT_BYTES)


def _dot(a, b):
    return jnp.dot(a, b, preferred_element_type=F32)


def _dot_nt(a, b):
    return lax.dot_general(a, b, (((1,), (1,)), ((), ())), preferred_element_type=F32)


def _loop_pairs(lo, hi, fn, carry):
    n = hi - lo

    def two(i, c):
        k = lo + 2 * i
        return fn(k + 1, fn(k, c))

    carry = lax.fori_loop(0, n // 2, two, carry)
    return lax.cond(n % 2 == 1, lambda c: fn(hi - 1, c), lambda c: c, carry)


def _rmsnorm(x, g):
    return x * lax.rsqrt(jnp.mean(x * x, axis=-1, keepdims=True) + EPS) * g


def _head_rows(q):
    return jnp.concatenate([q[:, h * HEAD_DIM:(h + 1) * HEAD_DIM] for h in range(N_HEADS)], axis=0)


def _head_stack(q):
    lane = lax.broadcasted_iota(I32, q.shape, 1)
    zero = jnp.zeros_like(q)
    return jnp.concatenate([jnp.where(lane // HEAD_DIM == h, q, zero) for h in range(N_HEADS)], axis=0)


def _head_merge(acc, t):
    lane = lax.broadcasted_iota(I32, (t, ATT_W), 1)
    out = jnp.zeros((t, ATT_W), F32)
    for h in range(N_HEADS):
        out = jnp.where(lane // HEAD_DIM == h, acc[h * t:(h + 1) * t], out)
    return out


def _rope(x, cos, sin):
    lane = lax.broadcasted_iota(I32, x.shape, 1)
    first = (lane % HEAD_DIM) < (HEAD_DIM // 2)
    w = x.shape[1]
    swapped = jnp.where(first, pltpu.roll(x, w - HEAD_DIM // 2, 1), pltpu.roll(x, HEAD_DIM // 2, 1))
    return x * cos + swapped * sin


def _inproj_body(x_ref, g_ref, w_ref, cos_ref, sin_ref, *out_refs, page):
    h = _rmsnorm(x_ref[...], g_ref[...]).astype(BF16)
    p = _dot(h, w_ref[...])
    cos, sin = cos_ref[...], sin_ref[...]
    scale = HEAD_DIM ** -0.5
    hg = p[:, _C_HG:_C_SB]
    sq = (p[:, _C_SB:_C_SB + ATT_W] * scale).astype(BF16)
    sk = p[:, _C_SB + ATT_W:_C_SB + 2 * ATT_W]
    sv = p[:, _C_SB + 2 * ATT_W:_C_DSA]
    cq = (_rope(p[:, _C_DSA:_C_DSA + ATT_W], cos, sin) * scale).astype(BF16)
    ck = _rope(p[:, _C_DSA + ATT_W:_C_DSA + 2 * ATT_W], cos, sin)
    cv = p[:, _C_DSA + 2 * ATT_W:_C_IQ]
    iq = (_rope(p[:, _C_IQ:_C_IK], cos, sin) * scale).astype(BF16)
    ik = _rope(p[:, _C_IK:_C_IW], cos[:, :LANES], sin[:, :LANES])
    if page is None:
        (hg_ref, sq_ref, sk_ref, sv_ref, cq_ref, ck_ref, cv_ref, iq_ref, ik_ref, iw_ref) = out_refs
        sk_ref[...] = sk
        sv_ref[...] = sv
        ck_ref[...] = ck
        cv_ref[...] = cv
        ik_ref[...] = ik[:, :HEAD_DIM]
    else:
        (hg_ref, sq_ref, skp_ref, svp_ref, skt_ref, svb_ref, cq_ref, ckp_ref, cvp_ref, ckt_ref, cvo_ref,
         iq_ref, ikt_ref, ikp_ref, iw_ref) = out_refs
        skt, svt, ckt, cvt, ikt = sk.T, sv.T, ck.T, cv.T, ik.T[:HEAD_DIM]
        for pg in range(p.shape[0] // page):
            sl = slice(pg * page, (pg + 1) * page)
            skp_ref[pg] = skt[:, sl]
            svp_ref[pg] = svt[:, sl]
            ckp_ref[pg] = ckt[:, sl]
            cvp_ref[pg] = cvt[:, sl]
            ikp_ref[pg] = ikt[:, sl]
        skt_ref[0] = skt.astype(BF16)
        ckt_ref[0] = ckt.astype(BF16)
        ikt_ref[0] = ikt.astype(BF16)
        svb_ref[...] = sv.astype(BF16)
        cvb = cv.astype(BF16)
        ones = jnp.ones((cvb.shape[0], HEAD_DIM), BF16)
        pieces = []
        for hd in range(N_HEADS):
            pieces += [cvb[:, hd * HEAD_DIM:(hd + 1) * HEAD_DIM], ones]
        cvo_ref[...] = jnp.concatenate(pieces, axis=1)
    hg_ref[...] = hg
    sq_ref[...] = sq
    cq_ref[...] = cq
    iq_ref[...] = iq
    iw_ref[...] = p[:, _C_IW:]


def _inproj(x, g, w_pad, cos, sin, block_rows, page=None, key_tile=None):
    rows, d = x.shape
    nblk = rows // block_rows
    row = lambda w: pl.BlockSpec((block_rows, w), lambda i: (i, 0))
    f32o = lambda w: (jax.ShapeDtypeStruct((rows, w), F32), row(w))
    bfo = lambda w: (jax.ShapeDtypeStruct((rows, w), BF16), row(w))
    if page is None:
        outs = [f32o(4 * HG_W), bfo(ATT_W), f32o(ATT_W), f32o(ATT_W), bfo(ATT_W), f32o(ATT_W), f32o(ATT_W),
                bfo(ATT_W), f32o(HEAD_DIM), f32o(LANES)]
    else:
        ppb = block_rows // page
        pages = lambda w: (jax.ShapeDtypeStruct((rows // page, w, page), F32),
                           pl.BlockSpec((ppb, w, page), lambda i: (i, 0, 0)))
        bpt = key_tile // block_rows
        tiles = lambda w: (jax.ShapeDtypeStruct((rows // key_tile, w, key_tile), BF16),
                           pl.BlockSpec((1, w, block_rows), lambda i: (i // bpt, 0, i % bpt)))
        outs = [f32o(4 * HG_W), bfo(ATT_W), pages(ATT_W), pages(ATT_W), tiles(ATT_W), bfo(ATT_W),
                bfo(ATT_W), pages(ATT_W), pages(ATT_W), tiles(ATT_W), bfo(2 * ATT_W),
                bfo(ATT_W), tiles(HEAD_DIM), pages(HEAD_DIM), f32o(LANES)]
    return pl.pallas_call(
        functools.partial(_inproj_body, page=page), grid=(nblk,),
        in_specs=[row(d), _const_spec((1, d)), _const_spec(w_pad.shape), row(ATT_W), row(ATT_W)],
        out_specs=tuple(o[1] for o in outs), out_shape=tuple(o[0] for o in outs),
        compiler_params=_params(("arbitrary",)),
    )(x, g, w_pad, cos, sin)


def _hgrn_body(hq_ref, hf_ref, hi_ref, hgate_ref, loglb_ref, log1mlb_ref, omlb_ref, ng_ref, s0_ref,
               o_ref, s_out_ref, st_ref, *, chunk, n_chunks):
    t = pl.program_id(1)

    @pl.when(t == 0)
    def _():
        for h in range(N_HEADS):
            st_ref[h] = s0_ref[0, h].T

    c = chunk
    ri = lax.broadcasted_iota(I32, (c, c), 0)
    ci = lax.broadcasted_iota(I32, (c, c), 1)
    tri = (ri >= ci).astype(F32)
    row = lax.broadcasted_iota(I32, (c, HG_W), 0)
    loglb, log1mlb, omlb, ng = loglb_ref[...], log1mlb_ref[...], omlb_ref[...], ng_ref[...]

    def one_chunk(ic, carry):
        r0 = pl.multiple_of(ic * c, c)
        q = hq_ref[pl.ds(r0, c), :]
        x = hf_ref[pl.ds(r0, c), :]
        v = hi_ref[pl.ds(r0, c), :]
        gate = hgate_ref[pl.ds(r0, c), :]
        qf = q * jax.nn.sigmoid(q)
        log_sig = jnp.minimum(x, 0.0) - jnp.log1p(jnp.exp(-jnp.abs(x)))
        bterm = log1mlb + log_sig
        log_f = jnp.maximum(loglb, bterm) + jnp.log1p(jnp.exp(-jnp.abs(loglb - bterm)))
        kf = omlb * jax.nn.sigmoid(-x)
        b = jnp.dot(tri, log_f, preferred_element_type=F32, precision=lax.Precision.HIGHEST)
        b_last = b[c - 1:c, :]
        qe = qf * jnp.exp(b)
        kk = kf * jnp.exp(b_last - b)
        e_last = jnp.exp(b_last)
        o_intra = [jnp.zeros((c, HG_D), F32) for _ in range(N_HEADS)]
        for s in range(c):
            w = jnp.exp(jnp.where(row >= s, b - b[s:s + 1, :], -jnp.inf)) * kf[s:s + 1, :]
            pw = qf * w
            for h in range(N_HEADS):
                sl = slice(h * HG_D, (h + 1) * HG_D)
                attn = jnp.sum(pw[:, sl], axis=1, keepdims=True)
                o_intra[h] = o_intra[h] + attn * v[s:s + 1, sl]
        outs = []
        for h in range(N_HEADS):
            sl = slice(h * HG_D, (h + 1) * HG_D)
            st = st_ref[h]
            o = o_intra[h] + _dot_nt(qe[:, sl].astype(BF16), st.astype(BF16))
            upd = lax.dot_general(v[:, sl].astype(BF16), kk[:, sl].astype(BF16), (((0,), (0,)), ((), ())),
                                  preferred_element_type=F32)
            st_ref[h] = st * e_last[:, sl] + upd
            o = o * lax.rsqrt(jnp.mean(o * o, axis=1, keepdims=True) + EPS)
            outs.append(o)
        o_all = jnp.concatenate(outs, axis=1) * ng * (gate * jax.nn.sigmoid(gate))
        o_ref[pl.ds(r0, c), :] = o_all
        return carry

    _loop_pairs(0, n_chunks, one_chunk, 0)

    @pl.when(t == pl.num_programs(1) - 1)
    def _():
        for h in range(N_HEADS):
            s_out_ref[0, h] = st_ref[h].T


def _hgrn(hg4, lb_terms, ng, s0, batch, seq, block_rows, chunk):
    nt = seq // block_rows
    col = lambda cidx: pl.BlockSpec((block_rows, HG_W), lambda b, t: (b * nt + t, cidx))
    loglb, log1mlb, omlb = lb_terms
    body = functools.partial(_hgrn_body, chunk=chunk, n_chunks=block_rows // chunk)
    return pl.pallas_call(
        body, grid=(batch, nt),
        in_specs=[col(0), col(1), col(2), col(3),
                  _const_spec((1, HG_W)), _const_spec((1, HG_W)), _const_spec((1, HG_W)), _const_spec((1, HG_W)),
                  pl.BlockSpec((1, N_HEADS, HG_D, HG_D), lambda b, t: (b, 0, 0, 0))],
        out_specs=(pl.BlockSpec((block_rows, HG_W), lambda b, t: (b * nt + t, 0)),
                   pl.BlockSpec((1, N_HEADS, HG_D, HG_D), lambda b, t: (b, 0, 0, 0))),
        out_shape=(jax.ShapeDtypeStruct((batch * seq, HG_W), F32),
                   jax.ShapeDtypeStruct((batch, N_HEADS, HG_D, HG_D), F32)),
        scratch_shapes=[pltpu.VMEM((N_HEADS, HG_D, HG_D), F32)],
        compiler_params=_params(("arbitrary", "arbitrary")),
    )(hg4, hg4, hg4, hg4, loglb, log1mlb, omlb, ng, s0)


def _pages_per_step(n_pages):
    return math.gcd(n_pages, 16)


def _group_width(pps, page):
    return 2 * page if pps % 2 == 0 else page


def _chunk(refs):
    return jnp.concatenate([r[...].astype(BF16) for r in refs], axis=1)


def _page_specs(layer, n_pages, pps, page, width, reverse):
    nc = n_pages // pps

    def spec(i):
        def index(b, j, pt):
            chunk = (nc - 1 - j) if reverse else j
            return (layer, pt[b, chunk * pps + i], 0, 0)
        return pl.BlockSpec((None, None, width, page), index)
    return [spec(i) for i in range(pps)]


def _per_batch(shape):
    return pl.BlockSpec((1,) + shape, lambda b, j, pt: (b,) + (0,) * len(shape))


def _suffix_matrix(tk):
    j = lax.broadcasted_iota(I32, (tk, tk), 0)
    s = lax.broadcasted_iota(I32, (tk, tk), 1)
    return (j > s).astype(BF16)


def _sb_exponents(z, visible, suffix):
    gw = suffix.shape[0]
    z2 = z * LOG2E
    log_sig = jnp.minimum(z2, 0.0) - jnp.log2(1.0 + jnp.exp2(-jnp.abs(z2)))
    log_keep = log_sig - z2
    if visible is not None:
        log_keep = jnp.where(visible, log_keep, 0.0)
        log_sig = jnp.where(visible, log_sig, NEG)
    exps, totals = [], []
    for g in range(z.shape[1] // gw):
        lk = log_keep[:, g * gw:(g + 1) * gw]
        hi = lk.astype(BF16)
        lo = (lk - hi.astype(F32)).astype(BF16)
        ex = _dot(hi, suffix) + _dot(lo, suffix)
        exps.append(log_sig[:, g * gw:(g + 1) * gw] + ex)
        totals.append(ex[:, 0:1] + lk[:, 0:1])
    return exps, totals


def _sb_weights(z, visible, carry, suffix):
    exps, totals = _sb_exponents(z, visible, suffix)
    parts = [None] * len(exps)
    for g in reversed(range(len(exps))):
        parts[g] = jnp.exp2(exps[g] + carry).astype(BF16)
        carry = carry + totals[g]
    return (parts[0] if len(parts) == 1 else jnp.concatenate(parts, axis=1)), carry


def _sb_prompt_body(q_ref, kt_ref, v_ref, o_ref, acc_ref, *, tq, tk):
    qi = pl.program_id(0)
    qs = _head_stack(q_ref[...])
    suffix = _suffix_matrix(min(tk, 2 * LANES))
    rows = N_HEADS * tq
    row_pos = qi * tq + lax.broadcasted_iota(I32, (rows, tk), 0) % tq
    col = lax.broadcasted_iota(I32, (rows, tk), 1)
    acc_ref[...] = jnp.zeros_like(acc_ref)

    def tile(kb, carry, masked):
        ks = pl.multiple_of(kb * tk, tk)
        visible = (col + kb * tk < row_pos) if masked else None
        a, carry = _sb_weights(_dot(qs, kt_ref[kb]), visible, carry, suffix)
        acc_ref[...] += _dot(a, v_ref[pl.ds(ks, tk), :])
        return carry

    kb_last = (qi * tq + tq - 1) // tk
    n_full = (qi * tq) // tk
    carry = tile(kb_last, jnp.zeros((rows, 1), F32), True)
    carry = lax.fori_loop(0, kb_last - n_full, lambda j, c: tile(kb_last - 1 - j, c, True), carry)
    _loop_pairs(0, n_full, lambda j, c: tile(n_full - 1 - j, c, False), carry)
    o_ref[...] = _head_merge(acc_ref[...], tq)


def _sb_prompt(q, kt, v, tq):
    t = q.shape[0]
    tk = kt.shape[2]
    body = functools.partial(_sb_prompt_body, tq=tq, tk=tk)
    return pl.pallas_call(
        body, grid=(t // tq,),
        in_specs=[pl.BlockSpec((tq, ATT_W), lambda i: (i, 0)), _const_spec(kt.shape), _const_spec((t, ATT_W))],
        out_specs=pl.BlockSpec((tq, ATT_W), lambda i: (i, 0)),
        out_shape=jax.ShapeDtypeStruct((t, ATT_W), F32),
        scratch_shapes=[pltpu.VMEM((N_HEADS * tq, ATT_W), F32)],
        compiler_params=_params(("arbitrary",)),
    )(q, kt, v)


def _sb_sample_body(pt_ref, q_ref, *refs, pps, page):
    kp, vp = refs[:pps], refs[pps:2 * pps]
    kn_ref, vn_ref, o_ref, acc_ref, carry_ref = refs[2 * pps:]
    j = pl.program_id(1)
    tq = q_ref.shape[1]
    rows = N_HEADS * tq
    qs = _head_stack(q_ref[0])

    @pl.when(j == 0)
    def _():
        z = _dot(qs, kn_ref[0].astype(BF16))
        row = lax.broadcasted_iota(I32, (rows, page), 0) % tq
        col = lax.broadcasted_iota(I32, (rows, page), 1)
        a, carry = _sb_weights(z, col < row, jnp.zeros((rows, 1), F32), _suffix_matrix(page))
        acc_ref[...] = _dot_nt(a, vn_ref[0].astype(BF16))
        carry_ref[...] = carry

    z = _dot(qs, _chunk(kp))
    a, carry = _sb_weights(z, None, carry_ref[...], _suffix_matrix(_group_width(pps, page)))
    acc_ref[...] += _dot_nt(a, _chunk(vp))
    carry_ref[...] = carry

    @pl.when(j == pl.num_programs(1) - 1)
    def _():
        o_ref[0] = _head_merge(acc_ref[...], tq)


def _sb_sample(page_table, q, cache_k, cache_v, k_new, v_new, layer):
    nb, tq, _ = q.shape
    n_pages = page_table.shape[1]
    page = cache_k.shape[3]
    pps = _pages_per_step(n_pages)
    body = functools.partial(_sb_sample_body, pps=pps, page=page)
    grid_spec = pltpu.PrefetchScalarGridSpec(
        num_scalar_prefetch=1, grid=(nb, n_pages // pps),
        in_specs=[_per_batch((tq, ATT_W))] + _page_specs(layer, n_pages, pps, page, ATT_W, True)
        + _page_specs(layer, n_pages, pps, page, ATT_W, True) + [_per_batch((ATT_W, page)), _per_batch((ATT_W, page))],
        out_specs=_per_batch((tq, ATT_W)),
        scratch_shapes=[pltpu.VMEM((N_HEADS * tq, ATT_W), F32), pltpu.VMEM((N_HEADS * tq, 1), F32)])
    return pl.pallas_call(
        body, grid_spec=grid_spec, out_shape=jax.ShapeDtypeStruct((nb, tq, ATT_W), F32),
        compiler_params=_params(("arbitrary", "arbitrary")),
    )(page_table, q, *([cache_k] * pps), *([cache_v] * pps), k_new, v_new)


def _score_keys(dots4, iw_b, tq, visible):
    score = jnp.zeros((tq, dots4.shape[1]), F32)
    for h in range(N_HEADS):
        score = score + jnp.maximum(dots4[h * tq:(h + 1) * tq], 0.0) * iw_b[h]
    score = score + 0.0
    if visible is not None:
        score = jnp.where(visible, score, NEG)
    bits = pltpu.bitcast(score, I32)
    return jnp.where(bits < 0, bits ^ 0x7FFFFFFF, bits)


def _kth_largest(count_ge, shape, k, bits):
    def step(i, v):
        cand = v | lax.shift_left(jnp.int32(1), bits - 1 - i)
        return jnp.where(count_ge(cand) >= k, cand, v)
    return lax.fori_loop(0, bits, step, jnp.zeros(shape, I32))


def _prefix_matrix(tk):
    j = lax.broadcasted_iota(I32, (tk, tk), 0)
    s = lax.broadcasted_iota(I32, (tk, tk), 1)
    return (j < s).astype(BF16)


def _additive_mask(key, vstar, need, eq_carry, prefix, visible):
    gw = prefix.shape[0]
    eq = key == vstar
    eqf = jnp.where(eq, 1.0, 0.0)
    before = []
    for g in range(key.shape[1] // gw):
        e = eqf[:, g * gw:(g + 1) * gw]
        before.append(_dot(e.astype(BF16), prefix) + eq_carry)
        eq_carry = eq_carry + jnp.sum(e, axis=1, keepdims=True)
    before = before[0] if len(before) == 1 else jnp.concatenate(before, axis=1)
    am = jnp.where(key > vstar, 0.0, jnp.where(eq, jnp.where(before < need, 0.0, NEG), NEG))
    if visible is not None:
        am = jnp.where(visible, am, NEG)
    return am, eq_carry


def _dsa_prompt_body(iq_ref, iw_ref, cq_ref, ikt_ref, ckt_ref, cvo_ref, o_ref,
                     keys_ref, half_ref, mx_ref, acc_ref, *, tq, tk, topk):
    qi = pl.program_id(0)
    n_tiles = (qi * tq + tq - 1) // tk + 1
    n_full = (qi * tq + 1) // tk
    row_pos = qi * tq + lax.broadcasted_iota(I32, (tq, tk), 0)
    col = lax.broadcasted_iota(I32, (tq, tk), 1)

    iq4 = _head_rows(iq_ref[...])
    iw = iw_ref[...]
    iw_b = [jnp.broadcast_to(iw[:, h:h + 1], (tq, tk)) for h in range(N_HEADS)]

    def score_tile(kb, masked):
        visible = (col + kb * tk <= row_pos) if masked else None
        key = _score_keys(_dot(iq4, ikt_ref[kb]), iw_b, tq, visible)
        keys_ref[kb] = key
        half_ref[kb] = lax.shift_right_arithmetic(key, 16).astype(I16)

    def score_full(kb, c):
        score_tile(kb, False)
        return c

    def score_part(kb, c):
        score_tile(kb, True)
        return c

    _loop_pairs(0, n_full, score_full, 0)
    lax.fori_loop(n_full, n_tiles, score_part, 0)

    def count16(pred):
        def add(kb, cnt):
            return cnt + jnp.where(pred(half_ref[kb]), jnp.int16(1), jnp.int16(0))
        cnt = _loop_pairs(0, n_tiles, add, jnp.zeros((tq, tk), I16))
        return jnp.sum(cnt.astype(I32), axis=1, keepdims=True)

    def half_b(v):
        return jnp.broadcast_to(v, (tq, tk)).astype(I16)

    def kth16(k):
        def count_ge(cand):
            cand_b = half_b(cand + I16_MIN)
            return count16(lambda x: x >= cand_b)
        return _kth_largest(count_ge, (tq, 1), k, 16) + I16_MIN

    vh = kth16(topk)
    vh_b = half_b(vh)
    k_low = topk - count16(lambda x: x > vh_b)
    vh_b32 = jnp.broadcast_to(vh, (tq, tk))

    def fill_low(kb, c):
        key = keys_ref[kb]
        low = (key & 0xFFFF) + I16_MIN
        in_group = lax.shift_right_arithmetic(key, 16) == vh_b32
        half_ref[kb] = jnp.where(in_group, low, I16_MIN).astype(I16)
        return c

    lax.fori_loop(0, n_tiles, fill_low, 0)
    vl = kth16(k_low)
    vl_b = half_b(vl)
    need = (k_low - count16(lambda x: x > vl_b)).astype(F32)
    vstar = lax.shift_left(vh, 16) | (vl - I16_MIN)
    vstar_b = jnp.broadcast_to(vstar, (tq, tk))
    need_b = jnp.broadcast_to(need, (tq, tk))

    cqs = _head_stack(cq_ref[...])
    prefix = _prefix_matrix(min(tk, 2 * LANES))
    mx_ref[...] = jnp.full_like(mx_ref, NEG)

    def select(kb, eq_carry, masked):
        visible = (col + kb * tk <= row_pos) if masked else None
        am, eq_carry = _additive_mask(keys_ref[kb], vstar_b, need_b, eq_carry, prefix, visible)
        keys_ref[kb] = pltpu.bitcast(am, I32)
        logits4 = _dot(cqs, ckt_ref[kb])
        for h in range(N_HEADS):
            lg = logits4[h * tq:(h + 1) * tq] + am
            top = lg[:, :LANES]
            for g in range(1, tk // LANES):
                top = jnp.maximum(top, lg[:, g * LANES:(g + 1) * LANES])
            mx_ref[h] = jnp.maximum(mx_ref[h], top)
        return eq_carry

    eq_carry = _loop_pairs(0, n_full, lambda kb, c: select(kb, c, False), jnp.zeros((tq, 1), F32))
    lax.fori_loop(n_full, n_tiles, lambda kb, c: select(kb, c, True), eq_carry)

    m_b = [jnp.broadcast_to(jnp.max(mx_ref[h], axis=1, keepdims=True), (tq, tk)) for h in range(N_HEADS)]
    acc_ref[...] = jnp.zeros_like(acc_ref)

    def attend(kb, c):
        ks = pl.multiple_of(kb * tk, tk)
        am = pltpu.bitcast(keys_ref[kb], F32)
        logits4 = _dot(cqs, ckt_ref[kb])
        vo = cvo_ref[pl.ds(ks, tk), :]
        for h in range(N_HEADS):
            p = jnp.exp(logits4[h * tq:(h + 1) * tq] + am - m_b[h]).astype(BF16)
            acc_ref[h] += _dot(p, vo[:, h * 2 * HEAD_DIM:(h + 1) * 2 * HEAD_DIM])
        return c

    _loop_pairs(0, n_tiles, attend, 0)
    outs = []
    for h in range(N_HEADS):
        acc = acc_ref[h]
        outs.append(acc[:, :HEAD_DIM] / acc[:, HEAD_DIM:HEAD_DIM + 1])
    o_ref[...] = jnp.concatenate(outs, axis=1)


def _dsa_prompt(iq, iw, cq, ikt, ckt, cvo, tq, topk):
    t = iq.shape[0]
    tk = ckt.shape[2]
    body = functools.partial(_dsa_prompt_body, tq=tq, tk=tk, topk=topk)
    row = lambda w: pl.BlockSpec((tq, w), lambda i: (i, 0))
    return pl.pallas_call(
        body, grid=(t // tq,),
        in_specs=[row(ATT_W), row(LANES), row(ATT_W),
                  _const_spec(ikt.shape), _const_spec(ckt.shape), _const_spec(cvo.shape)],
        out_specs=row(ATT_W),
        out_shape=jax.ShapeDtypeStruct((t, ATT_W), F32),
        scratch_shapes=[pltpu.VMEM((t // tk, tq, tk), I32), pltpu.VMEM((t // tk, tq, tk), I16),
                        pltpu.VMEM((N_HEADS, tq, LANES), F32), pltpu.VMEM((N_HEADS, tq, 2 * HEAD_DIM), F32)],
        compiler_params=_params(("arbitrary",)),
    )(iq, iw, cq, ikt, ckt, cvo)


def _dsa_score_sample_body(pt_ref, iq_ref, iw_ref, *refs, pps, page, topk):
    ikp = refs[:pps]
    ikn_ref, keys_ref, keysn_ref, vstar_ref, need_ref, all_ref, alln_ref = refs[pps:]
    j = pl.program_id(1)
    nc = pl.num_programs(1)
    tq = iq_ref.shape[1]
    cw = pps * page
    iq4 = _head_rows(iq_ref[0])
    iw = iw_ref[0]
    iw_b = lambda n: [jnp.broadcast_to(iw[:, h:h + 1], (tq, n)) for h in range(N_HEADS)]
    key = _score_keys(_dot(iq4, _chunk(ikp)), iw_b(cw), tq, None)
    keys_ref[0, 0] = key
    all_ref[j] = key

    @pl.when(j == nc - 1)
    def _():
        row = lax.broadcasted_iota(I32, (tq, page), 0)
        col = lax.broadcasted_iota(I32, (tq, page), 1)
        key_n = _score_keys(_dot(iq4, ikn_ref[0].astype(BF16)), iw_b(page), tq, col <= row)
        keysn_ref[0] = key_n
        alln_ref[...] = key_n

        def count(pred):
            def add(c, cnt):
                return cnt + jnp.where(pred(all_ref[c]), 1, 0)
            cnt = lax.fori_loop(0, nc, add, jnp.zeros((tq, cw), I32))
            cnt_n = jnp.where(pred(alln_ref[...]), 1, 0)
            return jnp.sum(cnt, axis=1, keepdims=True) + jnp.sum(cnt_n, axis=1, keepdims=True)

        vstar = _kth_largest(lambda cand: count(lambda x: x >= (cand ^ INT_MIN)), (tq, 1), topk, 32) ^ INT_MIN
        need = (topk - count(lambda x: x > vstar)).astype(F32)
        vstar_ref[0] = jnp.broadcast_to(vstar, (tq, page))
        need_ref[0] = jnp.broadcast_to(need, (tq, page))


def _dsa_score_sample(page_table, iq, iw, cache_ik, ik_new, layer, topk):
    nb, tq, _ = iq.shape
    n_pages = page_table.shape[1]
    page = cache_ik.shape[3]
    pps = _pages_per_step(n_pages)
    nc, cw = n_pages // pps, pps * page
    body = functools.partial(_dsa_score_sample_body, pps=pps, page=page, topk=topk)
    grid_spec = pltpu.PrefetchScalarGridSpec(
        num_scalar_prefetch=1, grid=(nb, nc),
        in_specs=[_per_batch((tq, ATT_W)), _per_batch((tq, LANES))]
        + _page_specs(layer, n_pages, pps, page, HEAD_DIM, False) + [_per_batch((HEAD_DIM, page))],
        out_specs=(pl.BlockSpec((1, 1, tq, cw), lambda b, j, pt: (b, j, 0, 0)),
                   _per_batch((tq, page)), _per_batch((tq, page)), _per_batch((tq, page))),
        scratch_shapes=[pltpu.VMEM((nc, tq, cw), I32), pltpu.VMEM((tq, page), I32)])
    return pl.pallas_call(
        body, grid_spec=grid_spec,
        out_shape=(jax.ShapeDtypeStruct((nb, nc, tq, cw), I32), jax.ShapeDtypeStruct((nb, tq, page), I32),
                   jax.ShapeDtypeStruct((nb, tq, page), I32), jax.ShapeDtypeStruct((nb, tq, page), F32)),
        compiler_params=_params(("arbitrary", "arbitrary")),
    )(page_table, iq, iw, *([cache_ik] * pps), ik_new)


def _dsa_attend_sample_body(pt_ref, cq_ref, keys_ref, keysn_ref, vstar_ref, need_ref, *refs, pps, page):
    kp, vp = refs[:pps], refs[pps:2 * pps]
    kn_ref, vn_ref, o_ref, acc_ref, m_ref, l_ref, eq_ref = refs[2 * pps:]
    j = pl.program_id(1)
    tq = cq_ref.shape[1]

    @pl.when(j == 0)
    def _():
        acc_ref[...] = jnp.zeros_like(acc_ref)
        m_ref[...] = jnp.full_like(m_ref, NEG)
        l_ref[...] = jnp.zeros_like(l_ref)
        eq_ref[...] = jnp.zeros_like(eq_ref)

    cqs = _head_stack(cq_ref[0])
    vstar = vstar_ref[0][:, 0:1]
    need = need_ref[0][:, 0:1]

    def step(key, k_tile, v_tile, visible, gw):
        am, eq_carry = _additive_mask(key, vstar, need, eq_ref[...], _prefix_matrix(gw), visible)
        eq_ref[...] = eq_carry
        am4 = jnp.concatenate([am] * N_HEADS, axis=0)
        lg = _dot(cqs, k_tile) + am4
        m_old = m_ref[...]
        m_new = jnp.maximum(m_old, jnp.max(lg, axis=1, keepdims=True))
        p = jnp.where(am4 == 0.0, jnp.exp(lg - m_new), 0.0)
        alpha = jnp.exp(m_old - m_new)
        l_ref[...] = l_ref[...] * alpha + jnp.sum(p, axis=1, keepdims=True)
        acc_ref[...] = acc_ref[...] * alpha + _dot_nt(p.astype(BF16), v_tile)
        m_ref[...] = m_new

    step(keys_ref[0, 0], _chunk(kp), _chunk(vp), None, _group_width(pps, page))

    @pl.when(j == pl.num_programs(1) - 1)
    def _():
        row = lax.broadcasted_iota(I32, (tq, page), 0)
        col = lax.broadcasted_iota(I32, (tq, page), 1)
        step(keysn_ref[0], kn_ref[0].astype(BF16), vn_ref[0].astype(BF16), col <= row, page)
        o_ref[0] = _head_merge(acc_ref[...] / l_ref[...], tq)


def _dsa_attend_sample(page_table, cq, keys, keys_new, vstar, need, cache_k, cache_v, k_new, v_new, layer):
    nb, tq, _ = cq.shape
    n_pages = page_table.shape[1]
    page = cache_k.shape[3]
    pps = _pages_per_step(n_pages)
    nc, cw = n_pages // pps, pps * page
    body = functools.partial(_dsa_attend_sample_body, pps=pps, page=page)
    grid_spec = pltpu.PrefetchScalarGridSpec(
        num_scalar_prefetch=1, grid=(nb, nc),
        in_specs=[_per_batch((tq, ATT_W)),
                  pl.BlockSpec((1, 1, tq, cw), lambda b, j, pt: (b, j, 0, 0)),
                  _per_batch((tq, page)), _per_batch((tq, page)), _per_batch((tq, page))]
        + _page_specs(layer, n_pages, pps, page, ATT_W, False) + _page_specs(layer, n_pages, pps, page, ATT_W, False)
        + [_per_batch((ATT_W, page)), _per_batch((ATT_W, page))],
        out_specs=_per_batch((tq, ATT_W)),
        scratch_shapes=[pltpu.VMEM((N_HEADS * tq, ATT_W), F32),
                        pltpu.VMEM((N_HEADS * tq, 1), F32), pltpu.VMEM((N_HEADS * tq, 1), F32),
                        pltpu.VMEM((tq, 1), F32)])
    return pl.pallas_call(
        body, grid_spec=grid_spec, out_shape=jax.ShapeDtypeStruct((nb, tq, ATT_W), F32),
        compiler_params=_params(("arbitrary", "arbitrary")),
    )(page_table, cq, keys, keys_new, vstar, need, *([cache_k] * pps), *([cache_v] * pps), k_new, v_new)


def _ffn_body(*refs, seq_rows, carried, final):
    it = iter(refs)
    x_ref, oa_ref, ob_ref, oc_ref, wout_ref, g_ref, wup_ref, cw_ref, cb_ref, wdown_ref = (next(it) for _ in range(10))
    e0_ref = e1_ref = gf_ref = carry_ref = None
    if not carried:
        e0_ref, e1_ref = next(it), next(it)
    if final:
        gf_ref = next(it)
    xo_ref, tail_ref = next(it), next(it)
    if carried:
        carry_ref = next(it)

        @pl.when(pl.program_id(0) == 0)
        def _():
            carry_ref[...] = jnp.zeros_like(carry_ref)

    mixed = jnp.concatenate([oa_ref[...], ob_ref[...], oc_ref[...]], axis=1).astype(BF16)
    x1 = x_ref[...] + _dot(mixed, wout_ref[...])
    h = _rmsnorm(x1, g_ref[...]).astype(BF16)
    up = _dot(h, wup_ref[...])
    d_ff = up.shape[1] // 2
    a, gate = up[:, :d_ff], up[:, d_ff:]
    rows = a.shape[0]
    r = lax.broadcasted_iota(I32, a.shape, 0) % seq_rows
    if carried:
        nc = carry_ref.shape[0]
        e0 = jnp.broadcast_to(carry_ref[nc - 2:nc - 1, :], a.shape)
        e1 = jnp.broadcast_to(carry_ref[nc - 1:nc, :], a.shape)
    else:
        e0, e1 = e0_ref[...], e1_ref[...]
    a_m1 = jnp.where(r == 0, e1, pltpu.roll(a, 1, 0))
    a_m2 = jnp.where(r == 0, e0, jnp.where(r == 1, e1, pltpu.roll(a, 2, 0)))
    cw = cw_ref[...]
    c = cb_ref[...] + a_m2 * cw[0:1, :] + a_m1 * cw[1:2, :] + a * cw[2:3, :]
    act = (c * jax.nn.sigmoid(c) * gate).astype(BF16)
    x2 = x1 + _dot(act, wdown_ref[...])
    xo_ref[...] = _rmsnorm(x2, gf_ref[...]) if final else x2
    tail = tail_ref.shape[0]
    tail_ref[...] = a[rows - tail:, :]
    if carried:
        nc = carry_ref.shape[0]
        carry_ref[...] = a[rows - nc:, :]


def _ffn(x, oa, ob, oc, w_out, g, w_up, conv_w, conv_b, w_down, block_rows, prev=None, final_g=None):
    rows, d = x.shape
    d_ff = w_down.shape[0]
    carried = prev is None
    final = final_g is not None
    row = lambda w: pl.BlockSpec((block_rows, w), lambda i: (i, 0))
    ins = [x, oa, ob, oc, w_out, g, w_up, conv_w, conv_b, w_down]
    in_specs = [row(d), row(HG_W), row(ATT_W), row(ATT_W), _const_spec(w_out.shape), _const_spec((1, d)),
                _const_spec(w_up.shape), _const_spec(conv_w.shape), _const_spec((1, d_ff)), _const_spec(w_down.shape)]
    if not carried:
        ins += list(prev)
        in_specs += [row(d_ff), row(d_ff)]
    if final:
        ins.append(final_g)
        in_specs.append(_const_spec((1, d)))
    if carried:
        tail_rows, tail_spec = 8, pl.BlockSpec((8, d_ff), lambda i: (0, 0))
        scratch = [pltpu.VMEM((8, d_ff), F32)]
        seq_rows = block_rows
    else:
        tail_rows, tail_spec = rows, row(d_ff)
        scratch = []
        seq_rows = 8
    body = functools.partial(_ffn_body, seq_rows=seq_rows, carried=carried, final=final)
    return pl.pallas_call(
        body, grid=(rows // block_rows,),
        in_specs=in_specs, out_specs=(row(d), tail_spec),
        out_shape=(jax.ShapeDtypeStruct((rows, d), F32), jax.ShapeDtypeStruct((tail_rows, d_ff), F32)),
        scratch_shapes=scratch,
        compiler_params=_params(("arbitrary",)),
    )(*ins)


def _rope_tables(pos):
    half = HEAD_DIM // 2
    inv = jnp.exp(-math.log(ROPE_THETA) * jnp.arange(half, dtype=F32) / half)
    ang = pos.astype(F32)[:, None] * inv[None, :]
    cos, sin = jnp.cos(ang), jnp.sin(ang)
    cos = jnp.tile(jnp.concatenate([cos, cos], axis=1), (1, N_HEADS))
    sin = jnp.tile(jnp.concatenate([-sin, sin], axis=1), (1, N_HEADS))
    return cos, sin


def _pad_w_in(w):
    d = w.shape[0]
    ik = w[:, _C_IK:_C_IK + HEAD_DIM]
    iw = w[:, _C_IK + HEAD_DIM:]
    pad = lambda n: jnp.zeros((d, n), w.dtype)
    return jnp.concatenate([w[:, :_C_IK], ik, pad(LANES - HEAD_DIM), iw, pad(LANES - iw.shape[1])], axis=1).astype(BF16)


def _new_page(a, page):
    return jnp.swapaxes(jnp.pad(a, ((0, 0), (0, page - a.shape[1]), (0, 0))), 1, 2)


def _feature_major(cache):
    depth, pool, page = cache.shape[:3]
    perm = (0, 1, 3, 4, 2) if cache.ndim == 5 else (0, 1, 3, 2)
    return jnp.transpose(cache, perm).reshape(depth, pool, -1, page)


def _slot_major(pages, lead):
    depth, n_pg, w, page = pages.shape
    if w == HEAD_DIM:
        return jnp.transpose(pages, (0, 1, 3, 2)).reshape((depth,) + lead + (n_pg, page, w))
    a = pages.reshape(depth, n_pg, N_HEADS, HEAD_DIM, page)
    return jnp.transpose(a, (0, 1, 4, 2, 3)).reshape((depth,) + lead + (n_pg, page, N_HEADS, HEAD_DIM))


def kernel(x_prompt, x_sample, cache_sb_k, cache_sb_v, cache_dsa_k, cache_dsa_v, cache_idx_k, state_hgrn, state_conv, page_table, hg_lower_bounds, attn_norm, w_in, hg_norm, w_out, ffn_norm, w_up, conv_w, conv_b, w_down, final_norm):
    depth = w_in.shape[0]
    nb_p, seq, d = x_prompt.shape
    nb_s, seq_s, _ = x_sample.shape
    assert nb_p == 1, "the prompt group is one sequence"
    page = cache_sb_k.shape[2]
    n_pages = page_table.shape[1]
    past = n_pages * page
    d_ff = w_down.shape[1]
    rows_s = nb_s * seq_s

    lb = jnp.cumsum(jax.nn.softmax(hg_lower_bounds.astype(F32), axis=0), axis=0)
    lb = (lb - lb[0]).reshape(depth, 1, HG_W)
    cos_p, sin_p = _rope_tables(jnp.arange(seq, dtype=I32))
    cos_s, sin_s = _rope_tables(jnp.tile(past + jnp.arange(seq_s, dtype=I32), nb_s))

    cache_sb_k, cache_sb_v = _feature_major(cache_sb_k), _feature_major(cache_sb_v)
    cache_dsa_k, cache_dsa_v = _feature_major(cache_dsa_k), _feature_major(cache_dsa_v)
    cache_idx_k = _feature_major(cache_idx_k)

    tq = 128
    block_rows = 256
    key_tile = 512
    assert seq % key_tile == 0 and key_tile % block_rows == 0
    topk_p = min(TOPK_MAX, seq // 4)
    topk_s = min(TOPK_MAX, (past + seq_s) // 4)

    xp = x_prompt.reshape(seq, d)
    xs = x_sample.reshape(rows_s, d)
    p_rows, s_rows = [], []
    for l in range(depth):
        w_in_l = _pad_w_in(w_in[l])
        g_attn = attn_norm[l].reshape(1, d)
        lb_terms = (jnp.log(lb[l]), jnp.log1p(-lb[l]), 1.0 - lb[l])
        ng = jnp.tile(hg_norm[l], N_HEADS).reshape(1, HG_W)
        w_out_l, w_up_l, w_down_l = w_out[l].astype(BF16), w_up[l].astype(BF16), w_down[l].astype(BF16)
        g_ffn = ffn_norm[l].reshape(1, d)
        cb = conv_b[l].reshape(1, d_ff)
        last = l == depth - 1
        final_g = final_norm.reshape(1, d) if last else None

        (hg4, sq, skp, svp, skt, svb, cq, ckp, cvp, ckt, cvo, iq, ikt, ikp, iw) = _inproj(
            xp, g_attn, w_in_l, cos_p, sin_p, block_rows, page=page, key_tile=key_tile)
        o_a, hg_new = _hgrn(hg4, lb_terms, ng, jnp.zeros((1, N_HEADS, HG_D, HG_D), F32), 1, seq, block_rows, 16)
        o_b = _sb_prompt(sq, skt, svb, tq)
        o_c = _dsa_prompt(iq, iw, cq, ikt, ckt, cvo, tq, topk_p)
        xp, tail = _ffn(xp, o_a, o_b, o_c, w_out_l, g_ffn, w_up_l, conv_w[l], cb, w_down_l, block_rows, final_g=final_g)
        p_rows.append((skp, svp, ckp, cvp, ikp, hg_new, tail[8 - (conv_w.shape[1] - 1):]))

        (hg4, sq, sk, sv, cq, ck, cv, iq, ik, iw) = _inproj(xs, g_attn, w_in_l, cos_s, sin_s, rows_s)
        o_a, hg_new = _hgrn(hg4, lb_terms, ng, state_hgrn[l], nb_s, seq_s, seq_s, seq_s)
        b3 = lambda a: a.reshape(nb_s, seq_s, a.shape[-1])
        new = lambda a: _new_page(b3(a), page)
        o_b = _sb_sample(page_table, b3(sq), cache_sb_k, cache_sb_v, new(sk), new(sv), l)
        keys, keys_new, vstar, need = _dsa_score_sample(page_table, b3(iq), b3(iw), cache_idx_k, new(ik), l, topk_s)
        o_c = _dsa_attend_sample(page_table, b3(cq), keys, keys_new, vstar, need, cache_dsa_k, cache_dsa_v,
                                 new(ck), new(cv), l)
        prev = (jnp.repeat(state_conv[l][:, 0], seq_s, axis=0), jnp.repeat(state_conv[l][:, 1], seq_s, axis=0))
        xs, a_rows = _ffn(xs, o_a, o_b.reshape(rows_s, ATT_W), o_c.reshape(rows_s, ATT_W), w_out_l, g_ffn, w_up_l,
                          conv_w[l], cb, w_down_l, rows_s, prev=prev, final_g=final_g)
        conv_new = a_rows.reshape(nb_s, seq_s, d_ff)[:, seq_s - (conv_w.shape[1] - 1):]
        s_rows.append((sk, sv, ck, cv, ik, hg_new, conv_new))

    stack = lambda rows, j: jnp.stack([r[j] for r in rows])
    heads = lambda a: _slot_major(a, (1,))
    heads_s = lambda a: a.reshape(depth, nb_s, seq_s, N_HEADS, HEAD_DIM)
    return (xp.reshape(1, seq, d), xs.reshape(nb_s, seq_s, d),
            heads(stack(p_rows, 0)), heads(stack(p_rows, 1)), heads(stack(p_rows, 2)), heads(stack(p_rows, 3)),
            heads(stack(p_rows, 4)),
            stack(p_rows, 5), stack(p_rows, 6).reshape(depth, 1, conv_w.shape[1] - 1, d_ff),
            heads_s(stack(s_rows, 0)), heads_s(stack(s_rows, 1)), heads_s(stack(s_rows, 2)), heads_s(stack(s_rows, 3)),
            stack(s_rows, 4).reshape(depth, nb_s, seq_s, HEAD_DIM),
            stack(s_rows, 5), stack(s_rows, 6))
```

```python
import functools
import math

import jax
import jax.numpy as jnp
from jax import lax
from jax.experimental import pallas as pl
from jax.experimental.pallas import tpu as pltpu

F32 = jnp.float32
BF16 = jnp.bfloat16
I32 = jnp.int32
I16 = jnp.int16

HEAD_DIM = 64
N_HEADS = 4
HG_D = 128
ATT_W = N_HEADS * HEAD_DIM
HG_W = N_HEADS * HG_D
TOPK_MAX = 256
ROPE_THETA = 10000.0
EPS = 1e-6
NEG = -1e30
LOG2E = 1.4426950408889634
INT_MIN = -2147483648
I16_MIN = -32768
VMEM_LIMIT_BYTES = 56 * 1024 * 1024
LANES = 128

_C_HG = 0
_C_SB = 4 * HG_W
_C_DSA = _C_SB + 3 * ATT_W
_C_IQ = _C_DSA + 3 * ATT_W
_C_IK = _C_IQ + ATT_W
_C_IW = _C_IK + LANES
IN_PAD = _C_IW + LANES


def _const_spec(shape):
    return pl.BlockSpec(shape, lambda *_: (0,) * len(shape), pipeline_mode=pl.Buffered(1))


def _params(sem):
    return pltpu.CompilerParams(dimension_semantics=sem, vmem_limit_bytes=VMEM_LIMIT_BYTES)


def _dot(a, b):
    return jnp.dot(a, b, preferred_element_type=F32)


def _dot_nt(a, b):
    return lax.dot_general(a, b, (((1,), (1,)), ((), ())), preferred_element_type=F32)


def _loop_pairs(lo, hi, fn, carry):
    n = hi - lo

    def two(i, c):
        k = lo + 2 * i
        return fn(k + 1, fn(k, c))

    carry = lax.fori_loop(0, n // 2, two, carry)
    return lax.cond(n % 2 == 1, lambda c: fn(hi - 1, c), lambda c: c, carry)


def _rmsnorm(x, g):
    return x * lax.rsqrt(jnp.mean(x * x, axis=-1, keepdims=True) + EPS) * g


def _head_rows(q):
    return jnp.concatenate([q[:, h * HEAD_DIM:(h + 1) * HEAD_DIM] for h in range(N_HEADS)], axis=0)


def _head_stack(q):
    lane = lax.broadcasted_iota(I32, q.shape, 1)
    zero = jnp.zeros_like(q)
    return jnp.concatenate([jnp.where(lane // HEAD_DIM == h, q, zero) for h in range(N_HEADS)], axis=0)


def _head_merge(acc, t):
    lane = lax.broadcasted_iota(I32, (t, ATT_W), 1)
    out = jnp.zeros((t, ATT_W), F32)
    for h in range(N_HEADS):
        out = jnp.where(lane // HEAD_DIM == h, acc[h * t:(h + 1) * t], out)
    return out


def _rope(x, cos, sin):
    lane = lax.broadcasted_iota(I32, x.shape, 1)
    first = (lane % HEAD_DIM) < (HEAD_DIM // 2)
    w = x.shape[1]
    swapped = jnp.where(first, pltpu.roll(x, w - HEAD_DIM // 2, 1), pltpu.roll(x, HEAD_DIM // 2, 1))
    return x * cos + swapped * sin


def _inproj_body(x_ref, g_ref, w_ref, cos_ref, sin_ref, *out_refs, page):
    h = _rmsnorm(x_ref[...], g_ref[...]).astype(BF16)
    p = _dot(h, w_ref[...])
    cos, sin = cos_ref[...], sin_ref[...]
    scale = HEAD_DIM ** -0.5
    hg = p[:, _C_HG:_C_SB]
    sq = (p[:, _C_SB:_C_SB + ATT_W] * scale).astype(BF16)
    sk = p[:, _C_SB + ATT_W:_C_SB + 2 * ATT_W]
    sv = p[:, _C_SB + 2 * ATT_W:_C_DSA]
    cq = (_rope(p[:, _C_DSA:_C_DSA + ATT_W], cos, sin) * scale).astype(BF16)
    ck = _rope(p[:, _C_DSA + ATT_W:_C_DSA + 2 * ATT_W], cos, sin)
    cv = p[:, _C_DSA + 2 * ATT_W:_C_IQ]
    iq = (_rope(p[:, _C_IQ:_C_IK], cos, sin) * scale).astype(BF16)
    ik = _rope(p[:, _C_IK:_C_IW], cos[:, :LANES], sin[:, :LANES])
    if page is None:
        (hg_ref, sq_ref, sk_ref, sv_ref, cq_ref, ck_ref, cv_ref, iq_ref, ik_ref, iw_ref) = out_refs
        sk_ref[...] = sk
        sv_ref[...] = sv
        ck_ref[...] = ck
        cv_ref[...] = cv
        ik_ref[...] = ik[:, :HEAD_DIM]
    else:
        (hg_ref, sq_ref, skp_ref, svp_ref, skt_ref, svb_ref, cq_ref, ckp_ref, cvp_ref, ckt_ref, cvo_ref,
         iq_ref, ikt_ref, ikp_ref, iw_ref) = out_refs
        skt, svt, ckt, cvt, ikt = sk.T, sv.T, ck.T, cv.T, ik.T[:HEAD_DIM]
        for pg in range(p.shape[0] // page):
            sl = slice(pg * page, (pg + 1) * page)
            skp_ref[pg] = skt[:, sl]
            svp_ref[pg] = svt[:, sl]
            ckp_ref[pg] = ckt[:, sl]
            cvp_ref[pg] = cvt[:, sl]
            ikp_ref[pg] = ikt[:, sl]
        skt_ref[0] = skt.astype(BF16)
        ckt_ref[0] = ckt.astype(BF16)
        ikt_ref[0] = ikt.astype(BF16)
        svb_ref[...] = sv.astype(BF16)
        cvb = cv.astype(BF16)
        ones = jnp.ones((cvb.shape[0], HEAD_DIM), BF16)
        pieces = []
        for hd in range(N_HEADS):
            pieces += [cvb[:, hd * HEAD_DIM:(hd + 1) * HEAD_DIM], ones]
        cvo_ref[...] = jnp.concatenate(pieces, axis=1)
    hg_ref[...] = hg
    sq_ref[...] = sq
    cq_ref[...] = cq
    iq_ref[...] = iq
    iw_ref[...] = p[:, _C_IW:]


def _inproj(x, g, w_pad, cos, sin, block_rows, page=None, key_tiles=None):
    rows, d = x.shape
    nblk = rows // block_rows
    row = lambda w: pl.BlockSpec((block_rows, w), lambda i: (i, 0))
    f32o = lambda w: (jax.ShapeDtypeStruct((rows, w), F32), row(w))
    bfo = lambda w: (jax.ShapeDtypeStruct((rows, w), BF16), row(w))
    if page is None:
        outs = [f32o(4 * HG_W), bfo(ATT_W), f32o(ATT_W), f32o(ATT_W), bfo(ATT_W), f32o(ATT_W), f32o(ATT_W),
                bfo(ATT_W), f32o(HEAD_DIM), f32o(LANES)]
    else:
        ppb = block_rows // page
        pages = lambda w: (jax.ShapeDtypeStruct((rows // page, w, page), F32),
                           pl.BlockSpec((ppb, w, page), lambda i: (i, 0, 0)))
        sb_tile, dsa_tile = key_tiles

        def tiles(w, kt):
            bpt = kt // block_rows
            return (jax.ShapeDtypeStruct((rows // kt, w, kt), BF16),
                    pl.BlockSpec((1, w, block_rows), lambda i: (i // bpt, 0, i % bpt)))
        outs = [f32o(4 * HG_W), bfo(ATT_W), pages(ATT_W), pages(ATT_W), tiles(ATT_W, sb_tile), bfo(ATT_W),
                bfo(ATT_W), pages(ATT_W), pages(ATT_W), tiles(ATT_W, dsa_tile), bfo(2 * ATT_W),
                bfo(ATT_W), tiles(HEAD_DIM, dsa_tile), pages(HEAD_DIM), f32o(LANES)]
    return pl.pallas_call(
        functools.partial(_inproj_body, page=page), grid=(nblk,),
        in_specs=[row(d), _const_spec((1, d)), _const_spec(w_pad.shape), row(ATT_W), row(ATT_W)],
        out_specs=tuple(o[1] for o in outs), out_shape=tuple(o[0] for o in outs),
        compiler_params=_params(("arbitrary",)),
    )(x, g, w_pad, cos, sin)


def _hgrn_body(hq_ref, hf_ref, hi_ref, hgate_ref, loglb_ref, log1mlb_ref, omlb_ref, ng_ref, s0_ref,
               o_ref, s_out_ref, st_ref, *, chunk, n_chunks):
    t = pl.program_id(1)

    @pl.when(t == 0)
    def _():
        for h in range(N_HEADS):
            st_ref[h] = s0_ref[0, h].T

    c = chunk
    ri = lax.broadcasted_iota(I32, (c, c), 0)
    ci = lax.broadcasted_iota(I32, (c, c), 1)
    tri = (ri >= ci).astype(F32)
    row = lax.broadcasted_iota(I32, (c, HG_W), 0)
    loglb, log1mlb, omlb, ng = loglb_ref[...], log1mlb_ref[...], omlb_ref[...], ng_ref[...]

    def one_chunk(ic, carry):
        r0 = pl.multiple_of(ic * c, c)
        q = hq_ref[pl.ds(r0, c), :]
        x = hf_ref[pl.ds(r0, c), :]
        v = hi_ref[pl.ds(r0, c), :]
        gate = hgate_ref[pl.ds(r0, c), :]
        qf = q * jax.nn.sigmoid(q)
        log_sig = jnp.minimum(x, 0.0) - jnp.log1p(jnp.exp(-jnp.abs(x)))
        bterm = log1mlb + log_sig
        log_f = jnp.maximum(loglb, bterm) + jnp.log1p(jnp.exp(-jnp.abs(loglb - bterm)))
        kf = omlb * jax.nn.sigmoid(-x)
        b = jnp.dot(tri, log_f, preferred_element_type=F32, precision=lax.Precision.HIGHEST)
        b_last = b[c - 1:c, :]
        qe = qf * jnp.exp(b)
        kk = kf * jnp.exp(b_last - b)
        e_last = jnp.exp(b_last)
        o_intra = [jnp.zeros((c, HG_D), F32) for _ in range(N_HEADS)]
        for s in range(c):
            w = jnp.exp(jnp.where(row >= s, b - b[s:s + 1, :], -jnp.inf)) * kf[s:s + 1, :]
            pw = qf * w
            for h in range(N_HEADS):
                sl = slice(h * HG_D, (h + 1) * HG_D)
                attn = jnp.sum(pw[:, sl], axis=1, keepdims=True)
                o_intra[h] = o_intra[h] + attn * v[s:s + 1, sl]
        outs = []
        for h in range(N_HEADS):
            sl = slice(h * HG_D, (h + 1) * HG_D)
            st = st_ref[h]
            o = o_intra[h] + _dot_nt(qe[:, sl].astype(BF16), st.astype(BF16))
            upd = lax.dot_general(v[:, sl].astype(BF16), kk[:, sl].astype(BF16), (((0,), (0,)), ((), ())),
                                  preferred_element_type=F32)
            st_ref[h] = st * e_last[:, sl] + upd
            o = o * lax.rsqrt(jnp.mean(o * o, axis=1, keepdims=True) + EPS)
            outs.append(o)
        o_all = jnp.concatenate(outs, axis=1) * ng * (gate * jax.nn.sigmoid(gate))
        o_ref[pl.ds(r0, c), :] = o_all
        return carry

    _loop_pairs(0, n_chunks, one_chunk, 0)

    @pl.when(t == pl.num_programs(1) - 1)
    def _():
        for h in range(N_HEADS):
            s_out_ref[0, h] = st_ref[h].T


def _hgrn(hg4, lb_terms, ng, s0, batch, seq, block_rows, chunk):
    nt = seq // block_rows
    col = lambda cidx: pl.BlockSpec((block_rows, HG_W), lambda b, t: (b * nt + t, cidx))
    loglb, log1mlb, omlb = lb_terms
    body = functools.partial(_hgrn_body, chunk=chunk, n_chunks=block_rows // chunk)
    return pl.pallas_call(
        body, grid=(batch, nt),
        in_specs=[col(0), col(1), col(2), col(3),
                  _const_spec((1, HG_W)), _const_spec((1, HG_W)), _const_spec((1, HG_W)), _const_spec((1, HG_W)),
                  pl.BlockSpec((1, N_HEADS, HG_D, HG_D), lambda b, t: (b, 0, 0, 0))],
        out_specs=(pl.BlockSpec((block_rows, HG_W), lambda b, t: (b * nt + t, 0)),
                   pl.BlockSpec((1, N_HEADS, HG_D, HG_D), lambda b, t: (b, 0, 0, 0))),
        out_shape=(jax.ShapeDtypeStruct((batch * seq, HG_W), F32),
                   jax.ShapeDtypeStruct((batch, N_HEADS, HG_D, HG_D), F32)),
        scratch_shapes=[pltpu.VMEM((N_HEADS, HG_D, HG_D), F32)],
        compiler_params=_params(("arbitrary", "arbitrary")),
    )(hg4, hg4, hg4, hg4, loglb, log1mlb, omlb, ng, s0)


def _pages_per_step(n_pages):
    return math.gcd(n_pages, 32)


def _group_width(pps, page):
    return 2 * page if pps % 2 == 0 else page


def _chunk(refs):
    return jnp.concatenate([r[...].astype(BF16) for r in refs], axis=1)


def _page_specs(layer, n_pages, pps, page, width, reverse):
    nc = n_pages // pps

    def spec(i):
        def index(b, j, pt):
            chunk = (nc - 1 - j) if reverse else j
            return (layer, pt[b, chunk * pps + i], 0, 0)
        return pl.BlockSpec((None, None, width, page), index)
    return [spec(i) for i in range(pps)]


def _per_batch(shape):
    return pl.BlockSpec((1,) + shape, lambda b, j, pt: (b,) + (0,) * len(shape))


def _suffix_matrix(tk):
    j = lax.broadcasted_iota(I32, (tk, tk), 0)
    s = lax.broadcasted_iota(I32, (tk, tk), 1)
    return (j > s).astype(BF16)


def _sb_exponents(z, visible, suffix):
    gw = suffix.shape[0]
    z2 = z * LOG2E
    log_sig = jnp.minimum(z2, 0.0) - jnp.log2(1.0 + jnp.exp2(-jnp.abs(z2)))
    log_keep = log_sig - z2
    if visible is not None:
        log_keep = jnp.where(visible, log_keep, 0.0)
        log_sig = jnp.where(visible, log_sig, NEG)
    exps, totals = [], []
    for g in range(z.shape[1] // gw):
        lk = log_keep[:, g * gw:(g + 1) * gw]
        hi = lk.astype(BF16)
        lo = (lk - hi.astype(F32)).astype(BF16)
        ex = _dot(hi, suffix) + _dot(lo, suffix)
        exps.append(log_sig[:, g * gw:(g + 1) * gw] + ex)
        totals.append(ex[:, 0:1] + lk[:, 0:1])
    return exps, totals


def _sb_weights(z, visible, carry, suffix):
    exps, totals = _sb_exponents(z, visible, suffix)
    parts = [None] * len(exps)
    for g in reversed(range(len(exps))):
        parts[g] = jnp.exp2(exps[g] + carry).astype(BF16)
        carry = carry + totals[g]
    return (parts[0] if len(parts) == 1 else jnp.concatenate(parts, axis=1)), carry


def _sb_prompt_body(q_ref, kt_ref, v_ref, o_ref, acc_ref, *, tq, tk):
    qi = pl.program_id(0)
    qs = _head_stack(q_ref[...])
    suffix = _suffix_matrix(min(tk, 2 * LANES))
    rows = N_HEADS * tq
    row_pos = qi * tq + lax.broadcasted_iota(I32, (rows, tk), 0) % tq
    col = lax.broadcasted_iota(I32, (rows, tk), 1)
    acc_ref[...] = jnp.zeros_like(acc_ref)

    def tile(kb, carry, masked):
        ks = pl.multiple_of(kb * tk, tk)
        visible = (col + kb * tk < row_pos) if masked else None
        a, carry = _sb_weights(_dot(qs, kt_ref[kb]), visible, carry, suffix)
        acc_ref[...] += _dot(a, v_ref[pl.ds(ks, tk), :])
        return carry

    kb_last = (qi * tq + tq - 1) // tk
    n_full = (qi * tq) // tk
    carry = tile(kb_last, jnp.zeros((rows, 1), F32), True)
    carry = lax.fori_loop(0, kb_last - n_full, lambda j, c: tile(kb_last - 1 - j, c, True), carry)
    _loop_pairs(0, n_full, lambda j, c: tile(n_full - 1 - j, c, False), carry)
    o_ref[...] = _head_merge(acc_ref[...], tq)


def _sb_prompt(q, kt, v, tq):
    t = q.shape[0]
    tk = kt.shape[2]
    body = functools.partial(_sb_prompt_body, tq=tq, tk=tk)
    return pl.pallas_call(
        body, grid=(t // tq,),
        in_specs=[pl.BlockSpec((tq, ATT_W), lambda i: (i, 0)), _const_spec(kt.shape), _const_spec((t, ATT_W))],
        out_specs=pl.BlockSpec((tq, ATT_W), lambda i: (i, 0)),
        out_shape=jax.ShapeDtypeStruct((t, ATT_W), F32),
        scratch_shapes=[pltpu.VMEM((N_HEADS * tq, ATT_W), F32)],
        compiler_params=_params(("arbitrary",)),
    )(q, kt, v)


def _sb_sample_body(pt_ref, q_ref, *refs, pps, page):
    kp, vp = refs[:pps], refs[pps:2 * pps]
    kn_ref, vn_ref, o_ref, acc_ref, carry_ref = refs[2 * pps:]
    j = pl.program_id(1)
    tq = q_ref.shape[1]
    rows = N_HEADS * tq
    qs = _head_stack(q_ref[0])

    @pl.when(j == 0)
    def _():
        z = _dot(qs, kn_ref[0].astype(BF16))
        row = lax.broadcasted_iota(I32, (rows, page), 0) % tq
        col = lax.broadcasted_iota(I32, (rows, page), 1)
        a, carry = _sb_weights(z, col < row, jnp.zeros((rows, 1), F32), _suffix_matrix(page))
        acc_ref[...] = _dot_nt(a, vn_ref[0].astype(BF16))
        carry_ref[...] = carry

    z = _dot(qs, _chunk(kp))
    a, carry = _sb_weights(z, None, carry_ref[...], _suffix_matrix(_group_width(pps, page)))
    acc_ref[...] += _dot_nt(a, _chunk(vp))
    carry_ref[...] = carry

    @pl.when(j == pl.num_programs(1) - 1)
    def _():
        o_ref[0] = _head_merge(acc_ref[...], tq)


def _sb_sample(page_table, q, cache_k, cache_v, k_new, v_new, layer):
    nb, tq, _ = q.shape
    n_pages = page_table.shape[1]
    page = cache_k.shape[3]
    pps = _pages_per_step(n_pages)
    body = functools.partial(_sb_sample_body, pps=pps, page=page)
    grid_spec = pltpu.PrefetchScalarGridSpec(
        num_scalar_prefetch=1, grid=(nb, n_pages // pps),
        in_specs=[_per_batch((tq, ATT_W))] + _page_specs(layer, n_pages, pps, page, ATT_W, True)
        + _page_specs(layer, n_pages, pps, page, ATT_W, True) + [_per_batch((ATT_W, page)), _per_batch((ATT_W, page))],
        out_specs=_per_batch((tq, ATT_W)),
        scratch_shapes=[pltpu.VMEM((N_HEADS * tq, ATT_W), F32), pltpu.VMEM((N_HEADS * tq, 1), F32)])
    return pl.pallas_call(
        body, grid_spec=grid_spec, out_shape=jax.ShapeDtypeStruct((nb, tq, ATT_W), F32),
        compiler_params=_params(("arbitrary", "arbitrary")),
    )(page_table, q, *([cache_k] * pps), *([cache_v] * pps), k_new, v_new)


def _score_keys(dots4, iw_b, tq, visible):
    score = jnp.zeros((tq, dots4.shape[1]), F32)
    for h in range(N_HEADS):
        score = score + jnp.maximum(dots4[h * tq:(h + 1) * tq], 0.0) * iw_b[h]
    score = score + 0.0
    if visible is not None:
        score = jnp.where(visible, score, NEG)
    bits = pltpu.bitcast(score, I32)
    return jnp.where(bits < 0, bits ^ 0x7FFFFFFF, bits)


def _kth_largest(count_ge, shape, k, bits):
    def step(i, v):
        cand = v | lax.shift_left(jnp.int32(1), bits - 1 - i)
        return jnp.where(count_ge(cand) >= k, cand, v)
    return lax.fori_loop(0, bits, step, jnp.zeros(shape, I32))


def _prefix_matrix(tk):
    j = lax.broadcasted_iota(I32, (tk, tk), 0)
    s = lax.broadcasted_iota(I32, (tk, tk), 1)
    return (j < s).astype(BF16)


def _additive_mask(key, vstar, need, eq_carry, prefix, visible):
    gw = prefix.shape[0]
    eq = key == vstar
    eqf = jnp.where(eq, 1.0, 0.0)
    before = []
    for g in range(key.shape[1] // gw):
        e = eqf[:, g * gw:(g + 1) * gw]
        before.append(_dot(e.astype(BF16), prefix) + eq_carry)
        eq_carry = eq_carry + jnp.sum(e, axis=1, keepdims=True)
    before = before[0] if len(before) == 1 else jnp.concatenate(before, axis=1)
    am = jnp.where(key > vstar, 0.0, jnp.where(eq, jnp.where(before < need, 0.0, NEG), NEG))
    if visible is not None:
        am = jnp.where(visible, am, NEG)
    return am, eq_carry


def _dsa_prompt_body(iq_ref, iw_ref, cq_ref, ikt_ref, ckt_ref, cvo_ref, o_ref,
                     keys_ref, half_ref, mx_ref, acc_ref, *, tq, tk, topk):
    qi = pl.program_id(0)
    n_tiles = (qi * tq + tq - 1) // tk + 1
    n_full = (qi * tq + 1) // tk
    row_pos = qi * tq + lax.broadcasted_iota(I32, (tq, tk), 0)
    col = lax.broadcasted_iota(I32, (tq, tk), 1)

    iq4 = _head_rows(iq_ref[...])
    iw = iw_ref[...]
    iw_b = [jnp.broadcast_to(iw[:, h:h + 1], (tq, tk)) for h in range(N_HEADS)]

    def score_tile(kb, masked):
        visible = (col + kb * tk <= row_pos) if masked else None
        key = _score_keys(_dot(iq4, ikt_ref[kb]), iw_b, tq, visible)
        keys_ref[kb] = key
        half_ref[kb] = lax.shift_right_arithmetic(key, 16).astype(I16)

    def score_full(kb, c):
        score_tile(kb, False)
        return c

    def score_part(kb, c):
        score_tile(kb, True)
        return c

    _loop_pairs(0, n_full, score_full, 0)
    lax.fori_loop(n_full, n_tiles, score_part, 0)

    cw = min(tk, 2 * LANES)

    def count16(pred):
        def add(kb, cnt):
            x = half_ref[kb]
            for g in range(tk // cw):
                cnt = cnt + jnp.where(pred(x[:, g * cw:(g + 1) * cw]), jnp.int16(1), jnp.int16(0))
            return cnt
        cnt = _loop_pairs(0, n_tiles, add, jnp.zeros((tq, cw), I16))
        return jnp.sum(cnt.astype(I32), axis=1, keepdims=True)

    def half_b(v):
        return jnp.broadcast_to(v, (tq, cw)).astype(I16)

    def kth16(k):
        def count_ge(cand):
            cand_b = half_b(cand + I16_MIN)
            return count16(lambda x: x >= cand_b)
        return _kth_largest(count_ge, (tq, 1), k, 16) + I16_MIN

    vh = kth16(topk)
    vh_b = half_b(vh)
    k_low = topk - count16(lambda x: x > vh_b)
    vh_b32 = jnp.broadcast_to(vh, (tq, tk))

    def fill_low(kb, c):
        key = keys_ref[kb]
        low = (key & 0xFFFF) + I16_MIN
        in_group = lax.shift_right_arithmetic(key, 16) == vh_b32
        half_ref[kb] = jnp.where(in_group, low, I16_MIN).astype(I16)
        return c

    lax.fori_loop(0, n_tiles, fill_low, 0)
    vl = kth16(k_low)
    vl_b = half_b(vl)
    need = (k_low - count16(lambda x: x > vl_b)).astype(F32)
    vstar = lax.shift_left(vh, 16) | (vl - I16_MIN)
    vstar_b = jnp.broadcast_to(vstar, (tq, tk))
    need_b = jnp.broadcast_to(need, (tq, tk))

    cqs = _head_stack(cq_ref[...])
    prefix = _prefix_matrix(min(tk, 2 * LANES))
    mx_ref[...] = jnp.full_like(mx_ref, NEG)

    def select(kb, eq_carry, masked):
        visible = (col + kb * tk <= row_pos) if masked else None
        am, eq_carry = _additive_mask(keys_ref[kb], vstar_b, need_b, eq_carry, prefix, visible)
        keys_ref[kb] = pltpu.bitcast(am, I32)
        logits4 = _dot(cqs, ckt_ref[kb])
        for h in range(N_HEADS):
            lg = logits4[h * tq:(h + 1) * tq] + am
            top = lg[:, :LANES]
            for g in range(1, tk // LANES):
                top = jnp.maximum(top, lg[:, g * LANES:(g + 1) * LANES])
            mx_ref[h] = jnp.maximum(mx_ref[h], top)
        return eq_carry

    eq_carry = _loop_pairs(0, n_full, lambda kb, c: select(kb, c, False), jnp.zeros((tq, 1), F32))
    lax.fori_loop(n_full, n_tiles, lambda kb, c: select(kb, c, True), eq_carry)

    m_b = [jnp.broadcast_to(jnp.max(mx_ref[h], axis=1, keepdims=True), (tq, tk)) for h in range(N_HEADS)]
    acc_ref[...] = jnp.zeros_like(acc_ref)

    def attend(kb, c):
        ks = pl.multiple_of(kb * tk, tk)
        am = pltpu.bitcast(keys_ref[kb], F32)
        logits4 = _dot(cqs, ckt_ref[kb])
        vo = cvo_ref[pl.ds(ks, tk), :]
        for h in range(N_HEADS):
            p = jnp.exp(logits4[h * tq:(h + 1) * tq] + am - m_b[h]).astype(BF16)
            acc_ref[h] += _dot(p, vo[:, h * 2 * HEAD_DIM:(h + 1) * 2 * HEAD_DIM])
        return c

    _loop_pairs(0, n_tiles, attend, 0)
    outs = []
    for h in range(N_HEADS):
        acc = acc_ref[h]
        outs.append(acc[:, :HEAD_DIM] / acc[:, HEAD_DIM:HEAD_DIM + 1])
    o_ref[...] = jnp.concatenate(outs, axis=1)


def _dsa_prompt(iq, iw, cq, ikt, ckt, cvo, tq, topk):
    t = iq.shape[0]
    tk = ckt.shape[2]
    body = functools.partial(_dsa_prompt_body, tq=tq, tk=tk, topk=topk)
    row = lambda w: pl.BlockSpec((tq, w), lambda i: (i, 0))
    return pl.pallas_call(
        body, grid=(t // tq,),
        in_specs=[row(ATT_W), row(LANES), row(ATT_W),
                  _const_spec(ikt.shape), _const_spec(ckt.shape), _const_spec(cvo.shape)],
        out_specs=row(ATT_W),
        out_shape=jax.ShapeDtypeStruct((t, ATT_W), F32),
        scratch_shapes=[pltpu.VMEM((t // tk, tq, tk), I32), pltpu.VMEM((t // tk, tq, tk), I16),
                        pltpu.VMEM((N_HEADS, tq, LANES), F32), pltpu.VMEM((N_HEADS, tq, 2 * HEAD_DIM), F32)],
        compiler_params=_params(("arbitrary",)),
    )(iq, iw, cq, ikt, ckt, cvo)


def _dsa_score_sample_body(pt_ref, iq_ref, iw_ref, *refs, pps, page, topk):
    ikp = refs[:pps]
    ikn_ref, keys_ref, keysn_ref, vstar_ref, need_ref, all_ref, alln_ref = refs[pps:]
    j = pl.program_id(1)
    nc = pl.num_programs(1)
    tq = iq_ref.shape[1]
    cw = pps * page
    iq4 = _head_rows(iq_ref[0])
    iw = iw_ref[0]
    iw_b = lambda n: [jnp.broadcast_to(iw[:, h:h + 1], (tq, n)) for h in range(N_HEADS)]
    key = _score_keys(_dot(iq4, _chunk(ikp)), iw_b(cw), tq, None)
    keys_ref[0, 0] = key
    all_ref[j] = key

    @pl.when(j == nc - 1)
    def _():
        row = lax.broadcasted_iota(I32, (tq, page), 0)
        col = lax.broadcasted_iota(I32, (tq, page), 1)
        key_n = _score_keys(_dot(iq4, ikn_ref[0].astype(BF16)), iw_b(page), tq, col <= row)
        keysn_ref[0] = key_n
        alln_ref[...] = key_n

        def count(pred):
            def add(c, cnt):
                return cnt + jnp.where(pred(all_ref[c]), 1, 0)
            cnt = lax.fori_loop(0, nc, add, jnp.zeros((tq, cw), I32))
            cnt_n = jnp.where(pred(alln_ref[...]), 1, 0)
            return jnp.sum(cnt, axis=1, keepdims=True) + jnp.sum(cnt_n, axis=1, keepdims=True)

        vstar = _kth_largest(lambda cand: count(lambda x: x >= (cand ^ INT_MIN)), (tq, 1), topk, 32) ^ INT_MIN
        need = (topk - count(lambda x: x > vstar)).astype(F32)
        vstar_ref[0] = jnp.broadcast_to(vstar, (tq, page))
        need_ref[0] = jnp.broadcast_to(need, (tq, page))


def _dsa_score_sample(page_table, iq, iw, cache_ik, ik_new, layer, topk):
    nb, tq, _ = iq.shape
    n_pages = page_table.shape[1]
    page = cache_ik.shape[3]
    pps = _pages_per_step(n_pages)
    nc, cw = n_pages // pps, pps * page
    body = functools.partial(_dsa_score_sample_body, pps=pps, page=page, topk=topk)
    grid_spec = pltpu.PrefetchScalarGridSpec(
        num_scalar_prefetch=1, grid=(nb, nc),
        in_specs=[_per_batch((tq, ATT_W)), _per_batch((tq, LANES))]
        + _page_specs(layer, n_pages, pps, page, HEAD_DIM, False) + [_per_batch((HEAD_DIM, page))],
        out_specs=(pl.BlockSpec((1, 1, tq, cw), lambda b, j, pt: (b, j, 0, 0)),
                   _per_batch((tq, page)), _per_batch((tq, page)), _per_batch((tq, page))),
        scratch_shapes=[pltpu.VMEM((nc, tq, cw), I32), pltpu.VMEM((tq, page), I32)])
    return pl.pallas_call(
        body, grid_spec=grid_spec,
        out_shape=(jax.ShapeDtypeStruct((nb, nc, tq, cw), I32), jax.ShapeDtypeStruct((nb, tq, page), I32),
                   jax.ShapeDtypeStruct((nb, tq, page), I32), jax.ShapeDtypeStruct((nb, tq, page), F32)),
        compiler_params=_params(("arbitrary", "arbitrary")),
    )(page_table, iq, iw, *([cache_ik] * pps), ik_new)


def _dsa_attend_sample_body(pt_ref, cq_ref, keys_ref, keysn_ref, vstar_ref, need_ref, *refs, pps, page):
    kp, vp = refs[:pps], refs[pps:2 * pps]
    kn_ref, vn_ref, o_ref, acc_ref, m_ref, l_ref, eq_ref = refs[2 * pps:]
    j = pl.program_id(1)
    tq = cq_ref.shape[1]

    @pl.when(j == 0)
    def _():
        acc_ref[...] = jnp.zeros_like(acc_ref)
        m_ref[...] = jnp.full_like(m_ref, NEG)
        l_ref[...] = jnp.zeros_like(l_ref)
        eq_ref[...] = jnp.zeros_like(eq_ref)

    cqs = _head_stack(cq_ref[0])
    vstar = vstar_ref[0][:, 0:1]
    need = need_ref[0][:, 0:1]

    def step(key, k_tile, v_tile, visible, gw):
        am, eq_carry = _additive_mask(key, vstar, need, eq_ref[...], _prefix_matrix(gw), visible)
        eq_ref[...] = eq_carry
        am4 = jnp.concatenate([am] * N_HEADS, axis=0)
        lg = _dot(cqs, k_tile) + am4
        m_old = m_ref[...]
        m_new = jnp.maximum(m_old, jnp.max(lg, axis=1, keepdims=True))
        p = jnp.where(am4 == 0.0, jnp.exp(lg - m_new), 0.0)
        alpha = jnp.exp(m_old - m_new)
        l_ref[...] = l_ref[...] * alpha + jnp.sum(p, axis=1, keepdims=True)
        acc_ref[...] = acc_ref[...] * alpha + _dot_nt(p.astype(BF16), v_tile)
        m_ref[...] = m_new

    step(keys_ref[0, 0], _chunk(kp), _chunk(vp), None, _group_width(pps, page))

    @pl.when(j == pl.num_programs(1) - 1)
    def _():
        row = lax.broadcasted_iota(I32, (tq, page), 0)
        col = lax.broadcasted_iota(I32, (tq, page), 1)
        step(keysn_ref[0], kn_ref[0].astype(BF16), vn_ref[0].astype(BF16), col <= row, page)
        o_ref[0] = _head_merge(acc_ref[...] / l_ref[...], tq)


def _dsa_attend_sample(page_table, cq, keys, keys_new, vstar, need, cache_k, cache_v, k_new, v_new, layer):
    nb, tq, _ = cq.shape
    n_pages = page_table.shape[1]
    page = cache_k.shape[3]
    pps = _pages_per_step(n_pages)
    nc, cw = n_pages // pps, pps * page
    body = functools.partial(_dsa_attend_sample_body, pps=pps, page=page)
    grid_spec = pltpu.PrefetchScalarGridSpec(
        num_scalar_prefetch=1, grid=(nb, nc),
        in_specs=[_per_batch((tq, ATT_W)),
                  pl.BlockSpec((1, 1, tq, cw), lambda b, j, pt: (b, j, 0, 0)),
                  _per_batch((tq, page)), _per_batch((tq, page)), _per_batch((tq, page))]
        + _page_specs(layer, n_pages, pps, page, ATT_W, False) + _page_specs(layer, n_pages, pps, page, ATT_W, False)
        + [_per_batch((ATT_W, page)), _per_batch((ATT_W, page))],
        out_specs=_per_batch((tq, ATT_W)),
        scratch_shapes=[pltpu.VMEM((N_HEADS * tq, ATT_W), F32),
                        pltpu.VMEM((N_HEADS * tq, 1), F32), pltpu.VMEM((N_HEADS * tq, 1), F32),
                        pltpu.VMEM((tq, 1), F32)])
    return pl.pallas_call(
        body, grid_spec=grid_spec, out_shape=jax.ShapeDtypeStruct((nb, tq, ATT_W), F32),
        compiler_params=_params(("arbitrary", "arbitrary")),
    )(page_table, cq, keys, keys_new, vstar, need, *([cache_k] * pps), *([cache_v] * pps), k_new, v_new)


def _ffn_body(*refs, seq_rows, carried, final):
    it = iter(refs)
    x_ref, oa_ref, ob_ref, oc_ref, wout_ref, g_ref, wup_ref, cw_ref, cb_ref, wdown_ref = (next(it) for _ in range(10))
    e0_ref = e1_ref = gf_ref = carry_ref = None
    if not carried:
        e0_ref, e1_ref = next(it), next(it)
    if final:
        gf_ref = next(it)
    xo_ref, tail_ref = next(it), next(it)
    if carried:
        carry_ref = next(it)

        @pl.when(pl.program_id(0) == 0)
        def _():
            carry_ref[...] = jnp.zeros_like(carry_ref)

    mixed = jnp.concatenate([oa_ref[...], ob_ref[...], oc_ref[...]], axis=1).astype(BF16)
    x1 = x_ref[...] + _dot(mixed, wout_ref[...])
    h = _rmsnorm(x1, g_ref[...]).astype(BF16)
    up = _dot(h, wup_ref[...])
    d_ff = up.shape[1] // 2
    a, gate = up[:, :d_ff], up[:, d_ff:]
    rows = a.shape[0]
    r = lax.broadcasted_iota(I32, a.shape, 0) % seq_rows
    if carried:
        nc = carry_ref.shape[0]
        e0 = jnp.broadcast_to(carry_ref[nc - 2:nc - 1, :], a.shape)
        e1 = jnp.broadcast_to(carry_ref[nc - 1:nc, :], a.shape)
    else:
        e0, e1 = e0_ref[...], e1_ref[...]
    a_m1 = jnp.where(r == 0, e1, pltpu.roll(a, 1, 0))
    a_m2 = jnp.where(r == 0, e0, jnp.where(r == 1, e1, pltpu.roll(a, 2, 0)))
    cw = cw_ref[...]
    c = cb_ref[...] + a_m2 * cw[0:1, :] + a_m1 * cw[1:2, :] + a * cw[2:3, :]
    act = (c * jax.nn.sigmoid(c) * gate).astype(BF16)
    x2 = x1 + _dot(act, wdown_ref[...])
    xo_ref[...] = _rmsnorm(x2, gf_ref[...]) if final else x2
    tail = tail_ref.shape[0]
    tail_ref[...] = a[rows - tail:, :]
    if carried:
        nc = carry_ref.shape[0]
        carry_ref[...] = a[rows - nc:, :]


def _ffn(x, oa, ob, oc, w_out, g, w_up, conv_w, conv_b, w_down, block_rows, prev=None, final_g=None):
    rows, d = x.shape
    d_ff = w_down.shape[0]
    carried = prev is None
    final = final_g is not None
    row = lambda w: pl.BlockSpec((block_rows, w), lambda i: (i, 0))
    ins = [x, oa, ob, oc, w_out, g, w_up, conv_w, conv_b, w_down]
    in_specs = [row(d), row(HG_W), row(ATT_W), row(ATT_W), _const_spec(w_out.shape), _const_spec((1, d)),
                _const_spec(w_up.shape), _const_spec(conv_w.shape), _const_spec((1, d_ff)), _const_spec(w_down.shape)]
    if not carried:
        ins += list(prev)
        in_specs += [row(d_ff), row(d_ff)]
    if final:
        ins.append(final_g)
        in_specs.append(_const_spec((1, d)))
    if carried:
        tail_rows, tail_spec = 8, pl.BlockSpec((8, d_ff), lambda i: (0, 0))
        scratch = [pltpu.VMEM((8, d_ff), F32)]
        seq_rows = block_rows
    else:
        tail_rows, tail_spec = rows, row(d_ff)
        scratch = []
        seq_rows = 8
    body = functools.partial(_ffn_body, seq_rows=seq_rows, carried=carried, final=final)
    return pl.pallas_call(
        body, grid=(rows // block_rows,),
        in_specs=in_specs, out_specs=(row(d), tail_spec),
        out_shape=(jax.ShapeDtypeStruct((rows, d), F32), jax.ShapeDtypeStruct((tail_rows, d_ff), F32)),
        scratch_shapes=scratch,
        compiler_params=_params(("arbitrary",)),
    )(*ins)


def _rope_tables(pos):
    half = HEAD_DIM // 2
    inv = jnp.exp(-math.log(ROPE_THETA) * jnp.arange(half, dtype=F32) / half)
    ang = pos.astype(F32)[:, None] * inv[None, :]
    cos, sin = jnp.cos(ang), jnp.sin(ang)
    cos = jnp.tile(jnp.concatenate([cos, cos], axis=1), (1, N_HEADS))
    sin = jnp.tile(jnp.concatenate([-sin, sin], axis=1), (1, N_HEADS))
    return cos, sin


def _pad_w_in(w):
    d = w.shape[0]
    ik = w[:, _C_IK:_C_IK + HEAD_DIM]
    iw = w[:, _C_IK + HEAD_DIM:]
    pad = lambda n: jnp.zeros((d, n), w.dtype)
    return jnp.concatenate([w[:, :_C_IK], ik, pad(LANES - HEAD_DIM), iw, pad(LANES - iw.shape[1])], axis=1).astype(BF16)


def _new_page(a, page):
    return jnp.swapaxes(jnp.pad(a, ((0, 0), (0, page - a.shape[1]), (0, 0))), 1, 2)


def _feature_major(cache):
    depth, pool, page = cache.shape[:3]
    perm = (0, 1, 3, 4, 2) if cache.ndim == 5 else (0, 1, 3, 2)
    return jnp.transpose(cache, perm).reshape(depth, pool, -1, page)


def _slot_major(pages, lead):
    depth, n_pg, w, page = pages.shape
    if w == HEAD_DIM:
        return jnp.transpose(pages, (0, 1, 3, 2)).reshape((depth,) + lead + (n_pg, page, w))
    a = pages.reshape(depth, n_pg, N_HEADS, HEAD_DIM, page)
    return jnp.transpose(a, (0, 1, 4, 2, 3)).reshape((depth,) + lead + (n_pg, page, N_HEADS, HEAD_DIM))


def kernel(x_prompt, x_sample, cache_sb_k, cache_sb_v, cache_dsa_k, cache_dsa_v, cache_idx_k, state_hgrn, state_conv, page_table, hg_lower_bounds, attn_norm, w_in, hg_norm, w_out, ffn_norm, w_up, conv_w, conv_b, w_down, final_norm):
    depth = w_in.shape[0]
    nb_p, seq, d = x_prompt.shape
    nb_s, seq_s, _ = x_sample.shape
    assert nb_p == 1, "the prompt group is one sequence"
    page = cache_sb_k.shape[2]
    n_pages = page_table.shape[1]
    past = n_pages * page
    d_ff = w_down.shape[1]
    rows_s = nb_s * seq_s

    lb = jnp.cumsum(jax.nn.softmax(hg_lower_bounds.astype(F32), axis=0), axis=0)
    lb = (lb - lb[0]).reshape(depth, 1, HG_W)
    cos_p, sin_p = _rope_tables(jnp.arange(seq, dtype=I32))
    cos_s, sin_s = _rope_tables(jnp.tile(past + jnp.arange(seq_s, dtype=I32), nb_s))

    cache_sb_k, cache_sb_v = _feature_major(cache_sb_k), _feature_major(cache_sb_v)
    cache_dsa_k, cache_dsa_v = _feature_major(cache_dsa_k), _feature_major(cache_dsa_v)
    cache_idx_k = _feature_major(cache_idx_k)

    tq = 128
    block_rows = 256
    key_tiles = (min(512, seq), min(1024, seq))
    assert all(seq % kt == 0 and kt % block_rows == 0 for kt in key_tiles)
    topk_p = min(TOPK_MAX, seq // 4)
    topk_s = min(TOPK_MAX, (past + seq_s) // 4)

    xp = x_prompt.reshape(seq, d)
    xs = x_sample.reshape(rows_s, d)
    p_rows, s_rows = [], []
    for l in range(depth):
        w_in_l = _pad_w_in(w_in[l])
        g_attn = attn_norm[l].reshape(1, d)
        lb_terms = (jnp.log(lb[l]), jnp.log1p(-lb[l]), 1.0 - lb[l])
        ng = jnp.tile(hg_norm[l], N_HEADS).reshape(1, HG_W)
        w_out_l, w_up_l, w_down_l = w_out[l].astype(BF16), w_up[l].astype(BF16), w_down[l].astype(BF16)
        g_ffn = ffn_norm[l].reshape(1, d)
        cb = conv_b[l].reshape(1, d_ff)
        last = l == depth - 1
        final_g = final_norm.reshape(1, d) if last else None

        (hg4, sq, skp, svp, skt, svb, cq, ckp, cvp, ckt, cvo, iq, ikt, ikp, iw) = _inproj(
            xp, g_attn, w_in_l, cos_p, sin_p, block_rows, page=page, key_tiles=key_tiles)
        o_a, hg_new = _hgrn(hg4, lb_terms, ng, jnp.zeros((1, N_HEADS, HG_D, HG_D), F32), 1, seq, block_rows, 16)
        o_b = _sb_prompt(sq, skt, svb, tq)
        o_c = _dsa_prompt(iq, iw, cq, ikt, ckt, cvo, tq, topk_p)
        xp, tail = _ffn(xp, o_a, o_b, o_c, w_out_l, g_ffn, w_up_l, conv_w[l], cb, w_down_l, block_rows, final_g=final_g)
        p_rows.append((skp, svp, ckp, cvp, ikp, hg_new, tail[8 - (conv_w.shape[1] - 1):]))

        (hg4, sq, sk, sv, cq, ck, cv, iq, ik, iw) = _inproj(xs, g_attn, w_in_l, cos_s, sin_s, rows_s)
        o_a, hg_new = _hgrn(hg4, lb_terms, ng, state_hgrn[l], nb_s, seq_s, seq_s, seq_s)
        b3 = lambda a: a.reshape(nb_s, seq_s, a.shape[-1])
        new = lambda a: _new_page(b3(a), page)
        o_b = _sb_sample(page_table, b3(sq), cache_sb_k, cache_sb_v, new(sk), new(sv), l)
        keys, keys_new, vstar, need = _dsa_score_sample(page_table, b3(iq), b3(iw), cache_idx_k, new(ik), l, topk_s)
        o_c = _dsa_attend_sample(page_table, b3(cq), keys, keys_new, vstar, need, cache_dsa_k, cache_dsa_v,
                                 new(ck), new(cv), l)
        prev = (jnp.repeat(state_conv[l][:, 0], seq_s, axis=0), jnp.repeat(state_conv[l][:, 1], seq_s, axis=0))
        xs, a_rows = _ffn(xs, o_a, o_b.reshape(rows_s, ATT_W), o_c.reshape(rows_s, ATT_W), w_out_l, g_ffn, w_up_l,
                          conv_w[l], cb, w_down_l, rows_s, prev=prev, final_g=final_g)
        conv_new = a_rows.reshape(nb_s, seq_s, d_ff)[:, seq_s - (conv_w.shape[1] - 1):]
        s_rows.append((sk, sv, ck, cv, ik, hg_new, conv_new))

    stack = lambda rows, j: jnp.stack([r[j] for r in rows])
    heads = lambda a: _slot_major(a, (1,))
    heads_s = lambda a: a.reshape(depth, nb_s, seq_s, N_HEADS, HEAD_DIM)
    return (xp.reshape(1, seq, d), xs.reshape(nb_s, seq_s, d),
            heads(stack(p_rows, 0)), heads(stack(p_rows, 1)), heads(stack(p_rows, 2)), heads(stack(p_rows, 3)),
            heads(stack(p_rows, 4)),
            stack(p_rows, 5), stack(p_rows, 6).reshape(depth, 1, conv_w.shape[1] - 1, d_ff),
            heads_s(stack(s_rows, 0)), heads_s(stack(s_rows, 1)), heads_s(stack(s_rows, 2)), heads_s(stack(s_rows, 3)),
            stack(s_rows, 4).reshape(depth, nb_s, seq_s, HEAD_DIM),
            stack(s_rows, 5), stack(s_rows, 6))
```

```python
import functools
import math

import jax
import jax.numpy as jnp
from jax import lax
from jax.experimental import pallas as pl
from jax.experimental.pallas import tpu as pltpu

F32 = jnp.float32
BF16 = jnp.bfloat16
I32 = jnp.int32
I16 = jnp.int16

HEAD_DIM = 64
N_HEADS = 4
HG_D = 128
ATT_W = N_HEADS * HEAD_DIM
HG_W = N_HEADS * HG_D
TOPK_MAX = 256
ROPE_THETA = 10000.0
EPS = 1e-6
NEG = -1e30
LOG2E = 1.4426950408889634
INT_MIN = -2147483648
I16_MIN = -32768
VMEM_LIMIT_BYTES = 56 * 1024 * 1024
LANES = 128

_C_HG = 0
_C_SB = 4 * HG_W
_C_DSA = _C_SB + 3 * ATT_W
_C_IQ = _C_DSA + 3 * ATT_W
_C_IK = _C_IQ + ATT_W
_C_IW = _C_IK + LANES
IN_PAD = _C_IW + LANES


def _const_spec(shape):
    return pl.BlockSpec(shape, lambda *_: (0,) * len(shape), pipeline_mode=pl.Buffered(1))


def _params(sem):
    return pltpu.CompilerParams(dimension_semantics=sem, vmem_limit_bytes=VMEM_LIMIT_BYTES)


def _dot(a, b):
    return jnp.dot(a, b, preferred_element_type=F32)


def _dot_nt(a, b):
    return lax.dot_general(a, b, (((1,), (1,)), ((), ())), preferred_element_type=F32)


def _loop_pairs(lo, hi, fn, carry):
    n = hi - lo

    def two(i, c):
        k = lo + 2 * i
        return fn(k + 1, fn(k, c))

    carry = lax.fori_loop(0, n // 2, two, carry)
    return lax.cond(n % 2 == 1, lambda c: fn(hi - 1, c), lambda c: c, carry)


def _rmsnorm(x, g):
    return x * lax.rsqrt(jnp.mean(x * x, axis=-1, keepdims=True) + EPS) * g


def _head_rows(q):
    return jnp.concatenate([q[:, h * HEAD_DIM:(h + 1) * HEAD_DIM] for h in range(N_HEADS)], axis=0)


def _head_stack(q):
    lane = lax.broadcasted_iota(I32, q.shape, 1)
    zero = jnp.zeros_like(q)
    return jnp.concatenate([jnp.where(lane // HEAD_DIM == h, q, zero) for h in range(N_HEADS)], axis=0)


def _head_merge(acc, t):
    lane = lax.broadcasted_iota(I32, (t, ATT_W), 1)
    out = jnp.zeros((t, ATT_W), F32)
    for h in range(N_HEADS):
        out = jnp.where(lane // HEAD_DIM == h, acc[h * t:(h + 1) * t], out)
    return out


def _rope(x, cos, sin):
    lane = lax.broadcasted_iota(I32, x.shape, 1)
    first = (lane % HEAD_DIM) < (HEAD_DIM // 2)
    w = x.shape[1]
    swapped = jnp.where(first, pltpu.roll(x, w - HEAD_DIM // 2, 1), pltpu.roll(x, HEAD_DIM // 2, 1))
    return x * cos + swapped * sin


def _inproj_body(x_ref, g_ref, w_ref, cos_ref, sin_ref, *out_refs, page):
    h = _rmsnorm(x_ref[...], g_ref[...]).astype(BF16)
    p = _dot(h, w_ref[...])
    cos, sin = cos_ref[...], sin_ref[...]
    scale = HEAD_DIM ** -0.5
    hg = p[:, _C_HG:_C_SB]
    sq = (p[:, _C_SB:_C_SB + ATT_W] * scale).astype(BF16)
    sk = p[:, _C_SB + ATT_W:_C_SB + 2 * ATT_W]
    sv = p[:, _C_SB + 2 * ATT_W:_C_DSA]
    cq = (_rope(p[:, _C_DSA:_C_DSA + ATT_W], cos, sin) * scale).astype(BF16)
    ck = _rope(p[:, _C_DSA + ATT_W:_C_DSA + 2 * ATT_W], cos, sin)
    cv = p[:, _C_DSA + 2 * ATT_W:_C_IQ]
    iq = (_rope(p[:, _C_IQ:_C_IK], cos, sin) * scale).astype(BF16)
    ik = _rope(p[:, _C_IK:_C_IW], cos[:, :LANES], sin[:, :LANES])
    if page is None:
        (hg_ref, sq_ref, sk_ref, sv_ref, cq_ref, ck_ref, cv_ref, iq_ref, ik_ref, iw_ref) = out_refs
        sk_ref[...] = sk
        sv_ref[...] = sv
        ck_ref[...] = ck
        cv_ref[...] = cv
        ik_ref[...] = ik[:, :HEAD_DIM]
    else:
        (hg_ref, sq_ref, skp_ref, svp_ref, skt_ref, svb_ref, cq_ref, ckp_ref, cvp_ref, ckt_ref, cvo_ref,
         iq_ref, ikt_ref, ikp_ref, iw_ref) = out_refs
        skt, svt, ckt, cvt, ikt = sk.T, sv.T, ck.T, cv.T, ik.T[:HEAD_DIM]
        for pg in range(p.shape[0] // page):
            sl = slice(pg * page, (pg + 1) * page)
            skp_ref[pg] = skt[:, sl]
            svp_ref[pg] = svt[:, sl]
            ckp_ref[pg] = ckt[:, sl]
            cvp_ref[pg] = cvt[:, sl]
            ikp_ref[pg] = ikt[:, sl]
        skt_ref[0] = skt.astype(BF16)
        ckt_ref[0] = ckt.astype(BF16)
        ikt_ref[0] = ikt.astype(BF16)
        svb_ref[...] = sv.astype(BF16)
        cvb = cv.astype(BF16)
        ones = jnp.ones((cvb.shape[0], HEAD_DIM), BF16)
        pieces = []
        for hd in range(N_HEADS):
            pieces += [cvb[:, hd * HEAD_DIM:(hd + 1) * HEAD_DIM], ones]
        cvo_ref[...] = jnp.concatenate(pieces, axis=1)
    hg_ref[...] = hg
    sq_ref[...] = sq
    cq_ref[...] = cq
    iq_ref[...] = iq
    iw_ref[...] = p[:, _C_IW:]


def _inproj(x, g, w_pad, cos, sin, block_rows, page=None, key_tiles=None):
    rows, d = x.shape
    nblk = rows // block_rows
    row = lambda w: pl.BlockSpec((block_rows, w), lambda i: (i, 0))
    f32o = lambda w: (jax.ShapeDtypeStruct((rows, w), F32), row(w))
    bfo = lambda w: (jax.ShapeDtypeStruct((rows, w), BF16), row(w))
    if page is None:
        outs = [f32o(4 * HG_W), bfo(ATT_W), f32o(ATT_W), f32o(ATT_W), bfo(ATT_W), f32o(ATT_W), f32o(ATT_W),
                bfo(ATT_W), f32o(HEAD_DIM), f32o(LANES)]
    else:
        ppb = block_rows // page
        pages = lambda w: (jax.ShapeDtypeStruct((rows // page, w, page), F32),
                           pl.BlockSpec((ppb, w, page), lambda i: (i, 0, 0)))
        sb_tile, dsa_tile = key_tiles

        def tiles(w, kt):
            bpt = kt // block_rows
            return (jax.ShapeDtypeStruct((rows // kt, w, kt), BF16),
                    pl.BlockSpec((1, w, block_rows), lambda i: (i // bpt, 0, i % bpt)))
        outs = [f32o(4 * HG_W), bfo(ATT_W), pages(ATT_W), pages(ATT_W), tiles(ATT_W, sb_tile), bfo(ATT_W),
                bfo(ATT_W), pages(ATT_W), pages(ATT_W), tiles(ATT_W, dsa_tile), bfo(2 * ATT_W),
                bfo(ATT_W), tiles(HEAD_DIM, dsa_tile), pages(HEAD_DIM), f32o(LANES)]
    return pl.pallas_call(
        functools.partial(_inproj_body, page=page), grid=(nblk,),
        in_specs=[row(d), _const_spec((1, d)), _const_spec(w_pad.shape), row(ATT_W), row(ATT_W)],
        out_specs=tuple(o[1] for o in outs), out_shape=tuple(o[0] for o in outs),
        compiler_params=_params(("arbitrary",)),
    )(x, g, w_pad, cos, sin)


def _hgrn_body(hq_ref, hf_ref, hi_ref, hgate_ref, loglb_ref, log1mlb_ref, omlb_ref, ng_ref, s0_ref,
               o_ref, s_out_ref, st_ref, *, chunk, n_chunks):
    t = pl.program_id(1)

    @pl.when(t == 0)
    def _():
        for h in range(N_HEADS):
            st_ref[h] = s0_ref[0, h].T

    c = chunk
    ri = lax.broadcasted_iota(I32, (c, c), 0)
    ci = lax.broadcasted_iota(I32, (c, c), 1)
    tri = (ri >= ci).astype(F32)
    row = lax.broadcasted_iota(I32, (c, HG_W), 0)
    loglb, log1mlb, omlb, ng = loglb_ref[...], log1mlb_ref[...], omlb_ref[...], ng_ref[...]

    def one_chunk(ic, carry):
        r0 = pl.multiple_of(ic * c, c)
        q = hq_ref[pl.ds(r0, c), :]
        x = hf_ref[pl.ds(r0, c), :]
        v = hi_ref[pl.ds(r0, c), :]
        gate = hgate_ref[pl.ds(r0, c), :]
        qf = q * jax.nn.sigmoid(q)
        log_sig = jnp.minimum(x, 0.0) - jnp.log1p(jnp.exp(-jnp.abs(x)))
        bterm = log1mlb + log_sig
        log_f = jnp.maximum(loglb, bterm) + jnp.log1p(jnp.exp(-jnp.abs(loglb - bterm)))
        kf = omlb * jax.nn.sigmoid(-x)
        b = jnp.dot(tri, log_f, preferred_element_type=F32, precision=lax.Precision.HIGHEST)
        b_last = b[c - 1:c, :]
        qe = qf * jnp.exp(b)
        kk = kf * jnp.exp(b_last - b)
        e_last = jnp.exp(b_last)
        o_intra = [jnp.zeros((c, HG_D), F32) for _ in range(N_HEADS)]
        for s in range(c):
            w = jnp.exp(jnp.where(row >= s, b - b[s:s + 1, :], -jnp.inf)) * kf[s:s + 1, :]
            pw = qf * w
            for h in range(N_HEADS):
                sl = slice(h * HG_D, (h + 1) * HG_D)
                attn = jnp.sum(pw[:, sl], axis=1, keepdims=True)
                o_intra[h] = o_intra[h] + attn * v[s:s + 1, sl]
        outs = []
        for h in range(N_HEADS):
            sl = slice(h * HG_D, (h + 1) * HG_D)
            st = st_ref[h]
            o = o_intra[h] + _dot_nt(qe[:, sl].astype(BF16), st.astype(BF16))
            upd = lax.dot_general(v[:, sl].astype(BF16), kk[:, sl].astype(BF16), (((0,), (0,)), ((), ())),
                                  preferred_element_type=F32)
            st_ref[h] = st * e_last[:, sl] + upd
            o = o * lax.rsqrt(jnp.mean(o * o, axis=1, keepdims=True) + EPS)
            outs.append(o)
        o_all = jnp.concatenate(outs, axis=1) * ng * (gate * jax.nn.sigmoid(gate))
        o_ref[pl.ds(r0, c), :] = o_all
        return carry

    _loop_pairs(0, n_chunks, one_chunk, 0)

    @pl.when(t == pl.num_programs(1) - 1)
    def _():
        for h in range(N_HEADS):
            s_out_ref[0, h] = st_ref[h].T


def _hgrn(hg4, lb_terms, ng, s0, batch, seq, block_rows, chunk):
    nt = seq // block_rows
    col = lambda cidx: pl.BlockSpec((block_rows, HG_W), lambda b, t: (b * nt + t, cidx))
    loglb, log1mlb, omlb = lb_terms
    body = functools.partial(_hgrn_body, chunk=chunk, n_chunks=block_rows // chunk)
    return pl.pallas_call(
        body, grid=(batch, nt),
        in_specs=[col(0), col(1), col(2), col(3),
                  _const_spec((1, HG_W)), _const_spec((1, HG_W)), _const_spec((1, HG_W)), _const_spec((1, HG_W)),
                  pl.BlockSpec((1, N_HEADS, HG_D, HG_D), lambda b, t: (b, 0, 0, 0))],
        out_specs=(pl.BlockSpec((block_rows, HG_W), lambda b, t: (b * nt + t, 0)),
                   pl.BlockSpec((1, N_HEADS, HG_D, HG_D), lambda b, t: (b, 0, 0, 0))),
        out_shape=(jax.ShapeDtypeStruct((batch * seq, HG_W), F32),
                   jax.ShapeDtypeStruct((batch, N_HEADS, HG_D, HG_D), F32)),
        scratch_shapes=[pltpu.VMEM((N_HEADS, HG_D, HG_D), F32)],
        compiler_params=_params(("arbitrary", "arbitrary")),
    )(hg4, hg4, hg4, hg4, loglb, log1mlb, omlb, ng, s0)


def _pages_per_step(n_pages):
    return math.gcd(n_pages, 32)


def _group_width(pps, page):
    return 2 * page if pps % 2 == 0 else page


def _chunk(refs):
    return jnp.concatenate([r[...].astype(BF16) for r in refs], axis=1)


def _page_specs(layer, n_pages, pps, page, width, reverse):
    nc = n_pages // pps

    def spec(i):
        def index(b, j, pt):
            chunk = (nc - 1 - j) if reverse else j
            return (layer, pt[b, chunk * pps + i], 0, 0)
        return pl.BlockSpec((None, None, width, page), index)
    return [spec(i) for i in range(pps)]


def _per_batch(shape):
    return pl.BlockSpec((1,) + shape, lambda b, j, pt: (b,) + (0,) * len(shape))


def _suffix_matrix(tk):
    j = lax.broadcasted_iota(I32, (2 * tk, tk), 0) % tk
    s = lax.broadcasted_iota(I32, (2 * tk, tk), 1)
    return (j > s).astype(BF16)


def _sb_exponents(z, visible, suffix):
    gw = suffix.shape[1]
    z2 = z * LOG2E
    log_sig = jnp.minimum(z2, 0.0) - jnp.log2(1.0 + jnp.exp2(-jnp.abs(z2)))
    log_keep = log_sig - z2
    if visible is not None:
        log_keep = jnp.where(visible, log_keep, 0.0)
        log_sig = jnp.where(visible, log_sig, NEG)
    exps, totals = [], []
    for g in range(z.shape[1] // gw):
        lk = log_keep[:, g * gw:(g + 1) * gw]
        hi = lk.astype(BF16)
        lo = (lk - hi.astype(F32)).astype(BF16)
        ex = _dot(jnp.concatenate([hi, lo], axis=1), suffix)
        exps.append(log_sig[:, g * gw:(g + 1) * gw] + ex)
        totals.append(ex[:, 0:1] + lk[:, 0:1])
    return exps, totals


def _sb_weights(z, visible, carry, suffix):
    exps, totals = _sb_exponents(z, visible, suffix)
    parts = [None] * len(exps)
    for g in reversed(range(len(exps))):
        parts[g] = jnp.exp2(exps[g] + carry).astype(BF16)
        carry = carry + totals[g]
    return (parts[0] if len(parts) == 1 else jnp.concatenate(parts, axis=1)), carry


def _sb_prompt_body(q_ref, kt_ref, v_ref, o_ref, acc_ref, *, tq, tk):
    qi = pl.program_id(0)
    qs = _head_stack(q_ref[...])
    suffix = _suffix_matrix(min(tk, 2 * LANES))
    rows = N_HEADS * tq
    row_pos = qi * tq + lax.broadcasted_iota(I32, (rows, tk), 0) % tq
    col = lax.broadcasted_iota(I32, (rows, tk), 1)
    acc_ref[...] = jnp.zeros_like(acc_ref)

    def tile(kb, carry, masked):
        ks = pl.multiple_of(kb * tk, tk)
        visible = (col + kb * tk < row_pos) if masked else None
        a, carry = _sb_weights(_dot(qs, kt_ref[kb]), visible, carry, suffix)
        acc_ref[...] += _dot(a, v_ref[pl.ds(ks, tk), :])
        return carry

    kb_last = (qi * tq + tq - 1) // tk
    n_full = (qi * tq) // tk
    carry = tile(kb_last, jnp.zeros((rows, 1), F32), True)
    carry = lax.fori_loop(0, kb_last - n_full, lambda j, c: tile(kb_last - 1 - j, c, True), carry)
    _loop_pairs(0, n_full, lambda j, c: tile(n_full - 1 - j, c, False), carry)
    o_ref[...] = _head_merge(acc_ref[...], tq)


def _sb_prompt(q, kt, v, tq):
    t = q.shape[0]
    tk = kt.shape[2]
    body = functools.partial(_sb_prompt_body, tq=tq, tk=tk)
    return pl.pallas_call(
        body, grid=(t // tq,),
        in_specs=[pl.BlockSpec((tq, ATT_W), lambda i: (i, 0)), _const_spec(kt.shape), _const_spec((t, ATT_W))],
        out_specs=pl.BlockSpec((tq, ATT_W), lambda i: (i, 0)),
        out_shape=jax.ShapeDtypeStruct((t, ATT_W), F32),
        scratch_shapes=[pltpu.VMEM((N_HEADS * tq, ATT_W), F32)],
        compiler_params=_params(("arbitrary",)),
    )(q, kt, v)


def _sb_sample_body(pt_ref, q_ref, *refs, pps, page):
    kp, vp = refs[:pps], refs[pps:2 * pps]
    kn_ref, vn_ref, o_ref, acc_ref, carry_ref = refs[2 * pps:]
    j = pl.program_id(1)
    tq = q_ref.shape[1]
    rows = N_HEADS * tq
    qs = _head_stack(q_ref[0])

    @pl.when(j == 0)
    def _():
        z = _dot(qs, kn_ref[0].astype(BF16))
        row = lax.broadcasted_iota(I32, (rows, page), 0) % tq
        col = lax.broadcasted_iota(I32, (rows, page), 1)
        a, carry = _sb_weights(z, col < row, jnp.zeros((rows, 1), F32), _suffix_matrix(page))
        acc_ref[...] = _dot_nt(a, vn_ref[0].astype(BF16))
        carry_ref[...] = carry

    z = _dot(qs, _chunk(kp))
    a, carry = _sb_weights(z, None, carry_ref[...], _suffix_matrix(_group_width(pps, page)))
    acc_ref[...] += _dot_nt(a, _chunk(vp))
    carry_ref[...] = carry

    @pl.when(j == pl.num_programs(1) - 1)
    def _():
        o_ref[0] = _head_merge(acc_ref[...], tq)


def _sb_sample(page_table, q, cache_k, cache_v, k_new, v_new, layer):
    nb, tq, _ = q.shape
    n_pages = page_table.shape[1]
    page = cache_k.shape[3]
    pps = _pages_per_step(n_pages)
    body = functools.partial(_sb_sample_body, pps=pps, page=page)
    grid_spec = pltpu.PrefetchScalarGridSpec(
        num_scalar_prefetch=1, grid=(nb, n_pages // pps),
        in_specs=[_per_batch((tq, ATT_W))] + _page_specs(layer, n_pages, pps, page, ATT_W, True)
        + _page_specs(layer, n_pages, pps, page, ATT_W, True) + [_per_batch((ATT_W, page)), _per_batch((ATT_W, page))],
        out_specs=_per_batch((tq, ATT_W)),
        scratch_shapes=[pltpu.VMEM((N_HEADS * tq, ATT_W), F32), pltpu.VMEM((N_HEADS * tq, 1), F32)])
    return pl.pallas_call(
        body, grid_spec=grid_spec, out_shape=jax.ShapeDtypeStruct((nb, tq, ATT_W), F32),
        compiler_params=_params(("arbitrary", "arbitrary")),
    )(page_table, q, *([cache_k] * pps), *([cache_v] * pps), k_new, v_new)


def _score_keys(dots4, iw_b, tq, visible):
    score = jnp.zeros((tq, dots4.shape[1]), F32)
    for h in range(N_HEADS):
        score = score + jnp.maximum(dots4[h * tq:(h + 1) * tq], 0.0) * iw_b[h]
    score = score + 0.0
    if visible is not None:
        score = jnp.where(visible, score, NEG)
    bits = pltpu.bitcast(score, I32)
    return jnp.where(bits < 0, bits ^ 0x7FFFFFFF, bits)


def _kth_largest(count_ge, shape, k, bits):
    def step(i, v):
        cand = v | lax.shift_left(jnp.int32(1), bits - 1 - i)
        return jnp.where(count_ge(cand) >= k, cand, v)
    return lax.fori_loop(0, bits, step, jnp.zeros(shape, I32))


def _prefix_matrix(tk):
    j = lax.broadcasted_iota(I32, (tk, tk), 0)
    s = lax.broadcasted_iota(I32, (tk, tk), 1)
    return (j < s).astype(BF16)


def _additive_mask(key, vstar, need, eq_carry, prefix, visible):
    gw = prefix.shape[0]
    eq = key == vstar
    eqf = jnp.where(eq, 1.0, 0.0)
    before = []
    for g in range(key.shape[1] // gw):
        e = eqf[:, g * gw:(g + 1) * gw]
        before.append(_dot(e.astype(BF16), prefix) + eq_carry)
        eq_carry = eq_carry + jnp.sum(e, axis=1, keepdims=True)
    before = before[0] if len(before) == 1 else jnp.concatenate(before, axis=1)
    am = jnp.where(key > vstar, 0.0, jnp.where(eq, jnp.where(before < need, 0.0, NEG), NEG))
    if visible is not None:
        am = jnp.where(visible, am, NEG)
    return am, eq_carry


def _dsa_prompt_body(iq_ref, iw_ref, cq_ref, ikt_ref, ckt_ref, cvo_ref, o_ref,
                     keys_ref, half_ref, mx_ref, acc_ref, *, tq, tk, topk):
    qi = pl.program_id(0)
    n_tiles = (qi * tq + tq - 1) // tk + 1
    n_full = (qi * tq + 1) // tk
    row_pos = qi * tq + lax.broadcasted_iota(I32, (tq, tk), 0)
    col = lax.broadcasted_iota(I32, (tq, tk), 1)

    iq4 = _head_rows(iq_ref[...])
    iw = iw_ref[...]
    iw_b = [jnp.broadcast_to(iw[:, h:h + 1], (tq, tk)) for h in range(N_HEADS)]

    def score_tile(kb, masked):
        visible = (col + kb * tk <= row_pos) if masked else None
        key = _score_keys(_dot(iq4, ikt_ref[kb]), iw_b, tq, visible)
        keys_ref[kb] = key
        half_ref[kb] = lax.shift_right_arithmetic(key, 16).astype(I16)

    def score_full(kb, c):
        score_tile(kb, False)
        return c

    def score_part(kb, c):
        score_tile(kb, True)
        return c

    _loop_pairs(0, n_full, score_full, 0)
    lax.fori_loop(n_full, n_tiles, score_part, 0)

    cw = min(tk, 2 * LANES)

    def count16(pred):
        def add(kb, cnt):
            x = half_ref[kb]
            for g in range(tk // cw):
                cnt = cnt + jnp.where(pred(x[:, g * cw:(g + 1) * cw]), jnp.int16(1), jnp.int16(0))
            return cnt
        cnt = _loop_pairs(0, n_tiles, add, jnp.zeros((tq, cw), I16))
        return jnp.sum(cnt.astype(I32), axis=1, keepdims=True)

    def half_b(v):
        return jnp.broadcast_to(v, (tq, cw)).astype(I16)

    def kth16(k):
        def count_ge(cand):
            cand_b = half_b(cand + I16_MIN)
            return count16(lambda x: x >= cand_b)
        return _kth_largest(count_ge, (tq, 1), k, 16) + I16_MIN

    vh = kth16(topk)
    vh_b = half_b(vh)
    k_low = topk - count16(lambda x: x > vh_b)
    vh_b32 = jnp.broadcast_to(vh, (tq, tk))

    def fill_low(kb, c):
        key = keys_ref[kb]
        low = (key & 0xFFFF) + I16_MIN
        in_group = lax.shift_right_arithmetic(key, 16) == vh_b32
        half_ref[kb] = jnp.where(in_group, low, I16_MIN).astype(I16)
        return c

    lax.fori_loop(0, n_tiles, fill_low, 0)
    vl = kth16(k_low)
    vl_b = half_b(vl)
    need = (k_low - count16(lambda x: x > vl_b)).astype(F32)
    vstar = lax.shift_left(vh, 16) | (vl - I16_MIN)
    vstar_b = jnp.broadcast_to(vstar, (tq, tk))
    need_b = jnp.broadcast_to(need, (tq, tk))

    cqs = _head_stack(cq_ref[...])
    prefix = _prefix_matrix(min(tk, 2 * LANES))
    mx_ref[...] = jnp.full_like(mx_ref, NEG)

    def select(kb, eq_carry, masked):
        visible = (col + kb * tk <= row_pos) if masked else None
        am, eq_carry = _additive_mask(keys_ref[kb], vstar_b, need_b, eq_carry, prefix, visible)
        keys_ref[kb] = pltpu.bitcast(am, I32)
        logits4 = _dot(cqs, ckt_ref[kb])
        for h in range(N_HEADS):
            lg = logits4[h * tq:(h + 1) * tq] + am
            top = lg[:, :LANES]
            for g in range(1, tk // LANES):
                top = jnp.maximum(top, lg[:, g * LANES:(g + 1) * LANES])
            mx_ref[h] = jnp.maximum(mx_ref[h], top)
        return eq_carry

    eq_carry = _loop_pairs(0, n_full, lambda kb, c: select(kb, c, False), jnp.zeros((tq, 1), F32))
    lax.fori_loop(n_full, n_tiles, lambda kb, c: select(kb, c, True), eq_carry)

    m_b = [jnp.broadcast_to(jnp.max(mx_ref[h], axis=1, keepdims=True), (tq, tk)) for h in range(N_HEADS)]
    acc_ref[...] = jnp.zeros_like(acc_ref)

    def attend(kb, c):
        ks = pl.multiple_of(kb * tk, tk)
        am = pltpu.bitcast(keys_ref[kb], F32)
        logits4 = _dot(cqs, ckt_ref[kb])
        vo = cvo_ref[pl.ds(ks, tk), :]
        for h in range(N_HEADS):
            p = jnp.exp(logits4[h * tq:(h + 1) * tq] + am - m_b[h]).astype(BF16)
            acc_ref[h] += _dot(p, vo[:, h * 2 * HEAD_DIM:(h + 1) * 2 * HEAD_DIM])
        return c

    _loop_pairs(0, n_tiles, attend, 0)
    outs = []
    for h in range(N_HEADS):
        acc = acc_ref[h]
        outs.append(acc[:, :HEAD_DIM] / acc[:, HEAD_DIM:HEAD_DIM + 1])
    o_ref[...] = jnp.concatenate(outs, axis=1)


def _dsa_prompt(iq, iw, cq, ikt, ckt, cvo, tq, topk):
    t = iq.shape[0]
    tk = ckt.shape[2]
    body = functools.partial(_dsa_prompt_body, tq=tq, tk=tk, topk=topk)
    row = lambda w: pl.BlockSpec((tq, w), lambda i: (i, 0))
    return pl.pallas_call(
        body, grid=(t // tq,),
        in_specs=[row(ATT_W), row(LANES), row(ATT_W),
                  _const_spec(ikt.shape), _const_spec(ckt.shape), _const_spec(cvo.shape)],
        out_specs=row(ATT_W),
        out_shape=jax.ShapeDtypeStruct((t, ATT_W), F32),
        scratch_shapes=[pltpu.VMEM((t // tk, tq, tk), I32), pltpu.VMEM((t // tk, tq, tk), I16),
                        pltpu.VMEM((N_HEADS, tq, LANES), F32), pltpu.VMEM((N_HEADS, tq, 2 * HEAD_DIM), F32)],
        compiler_params=_params(("arbitrary",)),
    )(iq, iw, cq, ikt, ckt, cvo)


def _dsa_score_sample_body(pt_ref, iq_ref, iw_ref, *refs, pps, page, topk):
    ikp = refs[:pps]
    ikn_ref, keys_ref, keysn_ref, vstar_ref, need_ref, all_ref, alln_ref = refs[pps:]
    j = pl.program_id(1)
    nc = pl.num_programs(1)
    tq = iq_ref.shape[1]
    cw = pps * page
    iq4 = _head_rows(iq_ref[0])
    iw = iw_ref[0]
    iw_b = lambda n: [jnp.broadcast_to(iw[:, h:h + 1], (tq, n)) for h in range(N_HEADS)]
    key = _score_keys(_dot(iq4, _chunk(ikp)), iw_b(cw), tq, None)
    keys_ref[0, 0] = key
    all_ref[j] = key

    @pl.when(j == nc - 1)
    def _():
        row = lax.broadcasted_iota(I32, (tq, page), 0)
        col = lax.broadcasted_iota(I32, (tq, page), 1)
        key_n = _score_keys(_dot(iq4, ikn_ref[0].astype(BF16)), iw_b(page), tq, col <= row)
        keysn_ref[0] = key_n
        alln_ref[...] = key_n

        def count(pred):
            def add(c, cnt):
                return cnt + jnp.where(pred(all_ref[c]), 1, 0)
            cnt = lax.fori_loop(0, nc, add, jnp.zeros((tq, cw), I32))
            cnt_n = jnp.where(pred(alln_ref[...]), 1, 0)
            return jnp.sum(cnt, axis=1, keepdims=True) + jnp.sum(cnt_n, axis=1, keepdims=True)

        vstar = _kth_largest(lambda cand: count(lambda x: x >= (cand ^ INT_MIN)), (tq, 1), topk, 32) ^ INT_MIN
        need = (topk - count(lambda x: x > vstar)).astype(F32)
        vstar_ref[0] = jnp.broadcast_to(vstar, (tq, page))
        need_ref[0] = jnp.broadcast_to(need, (tq, page))


def _dsa_score_sample(page_table, iq, iw, cache_ik, ik_new, layer, topk):
    nb, tq, _ = iq.shape
    n_pages = page_table.shape[1]
    page = cache_ik.shape[3]
    pps = _pages_per_step(n_pages)
    nc, cw = n_pages // pps, pps * page
    body = functools.partial(_dsa_score_sample_body, pps=pps, page=page, topk=topk)
    grid_spec = pltpu.PrefetchScalarGridSpec(
        num_scalar_prefetch=1, grid=(nb, nc),
        in_specs=[_per_batch((tq, ATT_W)), _per_batch((tq, LANES))]
        + _page_specs(layer, n_pages, pps, page, HEAD_DIM, False) + [_per_batch((HEAD_DIM, page))],
        out_specs=(pl.BlockSpec((1, 1, tq, cw), lambda b, j, pt: (b, j, 0, 0)),
                   _per_batch((tq, page)), _per_batch((tq, page)), _per_batch((tq, page))),
        scratch_shapes=[pltpu.VMEM((nc, tq, cw), I32), pltpu.VMEM((tq, page), I32)])
    return pl.pallas_call(
        body, grid_spec=grid_spec,
        out_shape=(jax.ShapeDtypeStruct((nb, nc, tq, cw), I32), jax.ShapeDtypeStruct((nb, tq, page), I32),
                   jax.ShapeDtypeStruct((nb, tq, page), I32), jax.ShapeDtypeStruct((nb, tq, page), F32)),
        compiler_params=_params(("arbitrary", "arbitrary")),
    )(page_table, iq, iw, *([cache_ik] * pps), ik_new)


def _dsa_attend_sample_body(pt_ref, cq_ref, keys_ref, keysn_ref, vstar_ref, need_ref, *refs, pps, page):
    kp, vp = refs[:pps], refs[pps:2 * pps]
    kn_ref, vn_ref, o_ref, acc_ref, m_ref, l_ref, eq_ref = refs[2 * pps:]
    j = pl.program_id(1)
    tq = cq_ref.shape[1]

    @pl.when(j == 0)
    def _():
        acc_ref[...] = jnp.zeros_like(acc_ref)
        m_ref[...] = jnp.full_like(m_ref, NEG)
        l_ref[...] = jnp.zeros_like(l_ref)
        eq_ref[...] = jnp.zeros_like(eq_ref)

    cqs = _head_stack(cq_ref[0])
    vstar = vstar_ref[0][:, 0:1]
    need = need_ref[0][:, 0:1]

    def step(key, k_tile, v_tile, visible, gw):
        am, eq_carry = _additive_mask(key, vstar, need, eq_ref[...], _prefix_matrix(gw), visible)
        eq_ref[...] = eq_carry
        am4 = jnp.concatenate([am] * N_HEADS, axis=0)
        lg = _dot(cqs, k_tile) + am4
        m_old = m_ref[...]
        m_new = jnp.maximum(m_old, jnp.max(lg, axis=1, keepdims=True))
        p = jnp.where(am4 == 0.0, jnp.exp(lg - m_new), 0.0)
        alpha = jnp.exp(m_old - m_new)
        l_ref[...] = l_ref[...] * alpha + jnp.sum(p, axis=1, keepdims=True)
        acc_ref[...] = acc_ref[...] * alpha + _dot_nt(p.astype(BF16), v_tile)
        m_ref[...] = m_new

    step(keys_ref[0, 0], _chunk(kp), _chunk(vp), None, _group_width(pps, page))

    @pl.when(j == pl.num_programs(1) - 1)
    def _():
        row = lax.broadcasted_iota(I32, (tq, page), 0)
        col = lax.broadcasted_iota(I32, (tq, page), 1)
        step(keysn_ref[0], kn_ref[0].astype(BF16), vn_ref[0].astype(BF16), col <= row, page)
        o_ref[0] = _head_merge(acc_ref[...] / l_ref[...], tq)


def _dsa_attend_sample(page_table, cq, keys, keys_new, vstar, need, cache_k, cache_v, k_new, v_new, layer):
    nb, tq, _ = cq.shape
    n_pages = page_table.shape[1]
    page = cache_k.shape[3]
    pps = _pages_per_step(n_pages)
    nc, cw = n_pages // pps, pps * page
    body = functools.partial(_dsa_attend_sample_body, pps=pps, page=page)
    grid_spec = pltpu.PrefetchScalarGridSpec(
        num_scalar_prefetch=1, grid=(nb, nc),
        in_specs=[_per_batch((tq, ATT_W)),
                  pl.BlockSpec((1, 1, tq, cw), lambda b, j, pt: (b, j, 0, 0)),
                  _per_batch((tq, page)), _per_batch((tq, page)), _per_batch((tq, page))]
        + _page_specs(layer, n_pages, pps, page, ATT_W, False) + _page_specs(layer, n_pages, pps, page, ATT_W, False)
        + [_per_batch((ATT_W, page)), _per_batch((ATT_W, page))],
        out_specs=_per_batch((tq, ATT_W)),
        scratch_shapes=[pltpu.VMEM((N_HEADS * tq, ATT_W), F32),
                        pltpu.VMEM((N_HEADS * tq, 1), F32), pltpu.VMEM((N_HEADS * tq, 1), F32),
                        pltpu.VMEM((tq, 1), F32)])
    return pl.pallas_call(
        body, grid_spec=grid_spec, out_shape=jax.ShapeDtypeStruct((nb, tq, ATT_W), F32),
        compiler_params=_params(("arbitrary", "arbitrary")),
    )(page_table, cq, keys, keys_new, vstar, need, *([cache_k] * pps), *([cache_v] * pps), k_new, v_new)


def _ffn_body(*refs, seq_rows, carried, final):
    it = iter(refs)
    x_ref, oa_ref, ob_ref, oc_ref, wout_ref, g_ref, wup_ref, cw_ref, cb_ref, wdown_ref = (next(it) for _ in range(10))
    e0_ref = e1_ref = gf_ref = carry_ref = None
    if not carried:
        e0_ref, e1_ref = next(it), next(it)
    if final:
        gf_ref = next(it)
    xo_ref, tail_ref = next(it), next(it)
    if carried:
        carry_ref = next(it)

        @pl.when(pl.program_id(0) == 0)
        def _():
            carry_ref[...] = jnp.zeros_like(carry_ref)

    mixed = jnp.concatenate([oa_ref[...], ob_ref[...], oc_ref[...]], axis=1).astype(BF16)
    x1 = x_ref[...] + _dot(mixed, wout_ref[...])
    h = _rmsnorm(x1, g_ref[...]).astype(BF16)
    up = _dot(h, wup_ref[...])
    d_ff = up.shape[1] // 2
    a, gate = up[:, :d_ff], up[:, d_ff:]
    rows = a.shape[0]
    r = lax.broadcasted_iota(I32, a.shape, 0) % seq_rows
    if carried:
        nc = carry_ref.shape[0]
        e0 = jnp.broadcast_to(carry_ref[nc - 2:nc - 1, :], a.shape)
        e1 = jnp.broadcast_to(carry_ref[nc - 1:nc, :], a.shape)
    else:
        e0, e1 = e0_ref[...], e1_ref[...]
    a_m1 = jnp.where(r == 0, e1, pltpu.roll(a, 1, 0))
    a_m2 = jnp.where(r == 0, e0, jnp.where(r == 1, e1, pltpu.roll(a, 2, 0)))
    cw = cw_ref[...]
    c = cb_ref[...] + a_m2 * cw[0:1, :] + a_m1 * cw[1:2, :] + a * cw[2:3, :]
    act = (c * jax.nn.sigmoid(c) * gate).astype(BF16)
    x2 = x1 + _dot(act, wdown_ref[...])
    xo_ref[...] = _rmsnorm(x2, gf_ref[...]) if final else x2
    tail = tail_ref.shape[0]
    tail_ref[...] = a[rows - tail:, :]
    if carried:
        nc = carry_ref.shape[0]
        carry_ref[...] = a[rows - nc:, :]


def _ffn(x, oa, ob, oc, w_out, g, w_up, conv_w, conv_b, w_down, block_rows, prev=None, final_g=None):
    rows, d = x.shape
    d_ff = w_down.shape[0]
    carried = prev is None
    final = final_g is not None
    row = lambda w: pl.BlockSpec((block_rows, w), lambda i: (i, 0))
    ins = [x, oa, ob, oc, w_out, g, w_up, conv_w, conv_b, w_down]
    in_specs = [row(d), row(HG_W), row(ATT_W), row(ATT_W), _const_spec(w_out.shape), _const_spec((1, d)),
                _const_spec(w_up.shape), _const_spec(conv_w.shape), _const_spec((1, d_ff)), _const_spec(w_down.shape)]
    if not carried:
        ins += list(prev)
        in_specs += [row(d_ff), row(d_ff)]
    if final:
        ins.append(final_g)
        in_specs.append(_const_spec((1, d)))
    if carried:
        tail_rows, tail_spec = 8, pl.BlockSpec((8, d_ff), lambda i: (0, 0))
        scratch = [pltpu.VMEM((8, d_ff), F32)]
        seq_rows = block_rows
    else:
        tail_rows, tail_spec = rows, row(d_ff)
        scratch = []
        seq_rows = 8
    body = functools.partial(_ffn_body, seq_rows=seq_rows, carried=carried, final=final)
    return pl.pallas_call(
        body, grid=(rows // block_rows,),
        in_specs=in_specs, out_specs=(row(d), tail_spec),
        out_shape=(jax.ShapeDtypeStruct((rows, d), F32), jax.ShapeDtypeStruct((tail_rows, d_ff), F32)),
        scratch_shapes=scratch,
        compiler_params=_params(("arbitrary",)),
    )(*ins)


def _rope_tables(pos):
    half = HEAD_DIM // 2
    inv = jnp.exp(-math.log(ROPE_THETA) * jnp.arange(half, dtype=F32) / half)
    ang = pos.astype(F32)[:, None] * inv[None, :]
    cos, sin = jnp.cos(ang), jnp.sin(ang)
    cos = jnp.tile(jnp.concatenate([cos, cos], axis=1), (1, N_HEADS))
    sin = jnp.tile(jnp.concatenate([-sin, sin], axis=1), (1, N_HEADS))
    return cos, sin


def _w_in_body(w_ref, o_ref, *, n_iw):
    i = pl.program_id(0)
    last = pl.num_programs(0) - 1
    x = w_ref[...]
    r = lax.broadcasted_iota(I32, (x.shape[0], x.shape[2]), 0)
    for l in range(x.shape[1]):
        xl = x[:, l, :]
        ik = jnp.where(r < HEAD_DIM, xl, 0.0)
        iw = jnp.where(r < n_iw, pltpu.roll(xl, LANES - HEAD_DIM, 0), 0.0)
        y = jnp.where(i == last, iw, jnp.where(i == last - 1, ik, xl))
        o_ref[l] = y.T.astype(BF16)


def _pad_w_in(w_in):
    depth, d, n = w_in.shape
    n_in = pl.cdiv(n, LANES)
    assert n_in * LANES == _C_IW and n > _C_IK + HEAD_DIM
    return pl.pallas_call(
        functools.partial(_w_in_body, n_iw=n - _C_IK - HEAD_DIM), grid=(IN_PAD // LANES,),
        in_specs=[pl.BlockSpec((LANES, depth, d), lambda i: (jnp.minimum(i, n_in - 1), 0, 0))],
        out_specs=pl.BlockSpec((depth, d, LANES), lambda i: (0, 0, i)),
        out_shape=jax.ShapeDtypeStruct((depth, d, IN_PAD), BF16),
        compiler_params=_params(("arbitrary",)),
    )(jnp.transpose(w_in, (2, 0, 1)))


def _new_page(a, page):
    return jnp.swapaxes(jnp.pad(a, ((0, 0), (0, page - a.shape[1]), (0, 0))), 1, 2)


def _feature_major(cache):
    depth, pool, page = cache.shape[:3]
    perm = (0, 1, 3, 4, 2) if cache.ndim == 5 else (0, 1, 3, 2)
    return jnp.transpose(cache, perm).reshape(depth, pool, -1, page)


def _slot_major(pages, lead):
    depth, n_pg, w, page = pages.shape
    if w == HEAD_DIM:
        return jnp.transpose(pages, (0, 1, 3, 2)).reshape((depth,) + lead + (n_pg, page, w))
    a = pages.reshape(depth, n_pg, N_HEADS, HEAD_DIM, page)
    return jnp.transpose(a, (0, 1, 4, 2, 3)).reshape((depth,) + lead + (n_pg, page, N_HEADS, HEAD_DIM))


def kernel(x_prompt, x_sample, cache_sb_k, cache_sb_v, cache_dsa_k, cache_dsa_v, cache_idx_k, state_hgrn, state_conv, page_table, hg_lower_bounds, attn_norm, w_in, hg_norm, w_out, ffn_norm, w_up, conv_w, conv_b, w_down, final_norm):
    depth = w_in.shape[0]
    nb_p, seq, d = x_prompt.shape
    nb_s, seq_s, _ = x_sample.shape
    assert nb_p == 1, "the prompt group is one sequence"
    page = cache_sb_k.shape[2]
    n_pages = page_table.shape[1]
    past = n_pages * page
    d_ff = w_down.shape[1]
    rows_s = nb_s * seq_s

    lb = jnp.cumsum(jax.nn.softmax(hg_lower_bounds.astype(F32), axis=0), axis=0)
    lb = (lb - lb[0]).reshape(depth, 1, HG_W)
    cos_p, sin_p = _rope_tables(jnp.arange(seq, dtype=I32))
    cos_s, sin_s = _rope_tables(jnp.tile(past + jnp.arange(seq_s, dtype=I32), nb_s))

    cache_sb_k, cache_sb_v = _feature_major(cache_sb_k), _feature_major(cache_sb_v)
    cache_dsa_k, cache_dsa_v = _feature_major(cache_dsa_k), _feature_major(cache_dsa_v)
    cache_idx_k = _feature_major(cache_idx_k)

    tq = 128
    block_rows = 256
    key_tiles = (min(512, seq), min(1024, seq))
    assert all(seq % kt == 0 and kt % block_rows == 0 for kt in key_tiles)
    topk_p = min(TOPK_MAX, seq // 4)
    topk_s = min(TOPK_MAX, (past + seq_s) // 4)

    w_in_pad = _pad_w_in(w_in)
    xp = x_prompt.reshape(seq, d)
    xs = x_sample.reshape(rows_s, d)
    p_rows, s_rows = [], []
    for l in range(depth):
        w_in_l = w_in_pad[l]
        g_attn = attn_norm[l].reshape(1, d)
        lb_terms = (jnp.log(lb[l]), jnp.log1p(-lb[l]), 1.0 - lb[l])
        ng = jnp.tile(hg_norm[l], N_HEADS).reshape(1, HG_W)
        w_out_l, w_up_l, w_down_l = w_out[l].astype(BF16), w_up[l].astype(BF16), w_down[l].astype(BF16)
        g_ffn = ffn_norm[l].reshape(1, d)
        cb = conv_b[l].reshape(1, d_ff)
        last = l == depth - 1
        final_g = final_norm.reshape(1, d) if last else None

        (hg4, sq, skp, svp, skt, svb, cq, ckp, cvp, ckt, cvo, iq, ikt, ikp, iw) = _inproj(
            xp, g_attn, w_in_l, cos_p, sin_p, block_rows, page=page, key_tiles=key_tiles)
        o_a, hg_new = _hgrn(hg4, lb_terms, ng, jnp.zeros((1, N_HEADS, HG_D, HG_D), F32), 1, seq, block_rows, 16)
        o_b = _sb_prompt(sq, skt, svb, tq)
        o_c = _dsa_prompt(iq, iw, cq, ikt, ckt, cvo, tq, topk_p)
        xp, tail = _ffn(xp, o_a, o_b, o_c, w_out_l, g_ffn, w_up_l, conv_w[l], cb, w_down_l, block_rows, final_g=final_g)
        p_rows.append((skp, svp, ckp, cvp, ikp, hg_new, tail[8 - (conv_w.shape[1] - 1):]))

        (hg4, sq, sk, sv, cq, ck, cv, iq, ik, iw) = _inproj(xs, g_attn, w_in_l, cos_s, sin_s, rows_s)
        o_a, hg_new = _hgrn(hg4, lb_terms, ng, state_hgrn[l], nb_s, seq_s, seq_s, seq_s)
        b3 = lambda a: a.reshape(nb_s, seq_s, a.shape[-1])
        new = lambda a: _new_page(b3(a), page)
        o_b = _sb_sample(page_table, b3(sq), cache_sb_k, cache_sb_v, new(sk), new(sv), l)
        keys, keys_new, vstar, need = _dsa_score_sample(page_table, b3(iq), b3(iw), cache_idx_k, new(ik), l, topk_s)
        o_c = _dsa_attend_sample(page_table, b3(cq), keys, keys_new, vstar, need, cache_dsa_k, cache_dsa_v,
                                 new(ck), new(cv), l)
        prev = (jnp.repeat(state_conv[l][:, 0], seq_s, axis=0), jnp.repeat(state_conv[l][:, 1], seq_s, axis=0))
        xs, a_rows = _ffn(xs, o_a, o_b.reshape(rows_s, ATT_W), o_c.reshape(rows_s, ATT_W), w_out_l, g_ffn, w_up_l,
                          conv_w[l], cb, w_down_l, rows_s, prev=prev, final_g=final_g)
        conv_new = a_rows.reshape(nb_s, seq_s, d_ff)[:, seq_s - (conv_w.shape[1] - 1):]
        s_rows.append((sk, sv, ck, cv, ik, hg_new, conv_new))

    stack = lambda rows, j: jnp.stack([r[j] for r in rows])
    heads = lambda a: _slot_major(a, (1,))
    heads_s = lambda a: a.reshape(depth, nb_s, seq_s, N_HEADS, HEAD_DIM)
    return (xp.reshape(1, seq, d), xs.reshape(nb_s, seq_s, d),
            heads(stack(p_rows, 0)), heads(stack(p_rows, 1)), heads(stack(p_rows, 2)), heads(stack(p_rows, 3)),
            heads(stack(p_rows, 4)),
            stack(p_rows, 5), stack(p_rows, 6).reshape(depth, 1, conv_w.shape[1] - 1, d_ff),
            heads_s(stack(s_rows, 0)), heads_s(stack(s_rows, 1)), heads_s(stack(s_rows, 2)), heads_s(stack(s_rows, 3)),
            stack(s_rows, 4).reshape(depth, nb_s, seq_s, HEAD_DIM),
            stack(s_rows, 5), stack(s_rows, 6))
```

```python
import functools
import math

import jax
import jax.numpy as jnp
from jax import lax
from jax.experimental import pallas as pl
from jax.experimental.pallas import tpu as pltpu

F32 = jnp.float32
BF16 = jnp.bfloat16
I32 = jnp.int32
I16 = jnp.int16

HEAD_DIM = 64
N_HEADS = 4
HG_D = 128
ATT_W = N_HEADS * HEAD_DIM
HG_W = N_HEADS * HG_D
TOPK_MAX = 256
ROPE_THETA = 10000.0
EPS = 1e-6
NEG = -1e30
LOG2E = 1.4426950408889634
INT_MIN = -2147483648
I16_MIN = -32768
VMEM_LIMIT_BYTES = 56 * 1024 * 1024
LANES = 128

_C_HG = 0
_C_SB = 4 * HG_W
_C_DSA = _C_SB + 3 * ATT_W
_C_IQ = _C_DSA + 3 * ATT_W
_C_IK = _C_IQ + ATT_W
_C_IW = _C_IK + LANES
IN_PAD = _C_IW + LANES


def _const_spec(shape):
    return pl.BlockSpec(shape, lambda *_: (0,) * len(shape), pipeline_mode=pl.Buffered(1))


def _params(sem):
    return pltpu.CompilerParams(dimension_semantics=sem, vmem_limit_bytes=VMEM_LIMIT_BYTES)


def _dot(a, b):
    return jnp.dot(a, b, preferred_element_type=F32)


def _dot_nt(a, b):
    return lax.dot_general(a, b, (((1,), (1,)), ((), ())), preferred_element_type=F32)


def _loop_pairs(lo, hi, fn, carry):
    n = hi - lo

    def two(i, c):
        k = lo + 2 * i
        return fn(k + 1, fn(k, c))

    carry = lax.fori_loop(0, n // 2, two, carry)
    return lax.cond(n % 2 == 1, lambda c: fn(hi - 1, c), lambda c: c, carry)


def _rmsnorm(x, g):
    return x * lax.rsqrt(jnp.mean(x * x, axis=-1, keepdims=True) + EPS) * g


def _head_rows(q):
    return jnp.concatenate([q[:, h * HEAD_DIM:(h + 1) * HEAD_DIM] for h in range(N_HEADS)], axis=0)


def _head_stack(q):
    lane = lax.broadcasted_iota(I32, q.shape, 1)
    zero = jnp.zeros_like(q)
    return jnp.concatenate([jnp.where(lane // HEAD_DIM == h, q, zero) for h in range(N_HEADS)], axis=0)


def _head_merge(acc, t):
    lane = lax.broadcasted_iota(I32, (t, ATT_W), 1)
    out = jnp.zeros((t, ATT_W), F32)
    for h in range(N_HEADS):
        out = jnp.where(lane // HEAD_DIM == h, acc[h * t:(h + 1) * t], out)
    return out


def _rope(x, cos, sin):
    lane = lax.broadcasted_iota(I32, x.shape, 1)
    first = (lane % HEAD_DIM) < (HEAD_DIM // 2)
    w = x.shape[1]
    swapped = jnp.where(first, pltpu.roll(x, w - HEAD_DIM // 2, 1), pltpu.roll(x, HEAD_DIM // 2, 1))
    return x * cos + swapped * sin


def _inproj_body(x_ref, g_ref, w_ref, cos_ref, sin_ref, *out_refs, page):
    h = _rmsnorm(x_ref[...], g_ref[...]).astype(BF16)
    p = _dot(h, w_ref[...])
    cos, sin = cos_ref[...], sin_ref[...]
    scale = HEAD_DIM ** -0.5
    hg = p[:, _C_HG:_C_SB]
    sq = (p[:, _C_SB:_C_SB + ATT_W] * scale).astype(BF16)
    sk = p[:, _C_SB + ATT_W:_C_SB + 2 * ATT_W]
    sv = p[:, _C_SB + 2 * ATT_W:_C_DSA]
    cq = (_rope(p[:, _C_DSA:_C_DSA + ATT_W], cos, sin) * scale).astype(BF16)
    ck = _rope(p[:, _C_DSA + ATT_W:_C_DSA + 2 * ATT_W], cos, sin)
    cv = p[:, _C_DSA + 2 * ATT_W:_C_IQ]
    iq = (_rope(p[:, _C_IQ:_C_IK], cos, sin) * scale).astype(BF16)
    ik = _rope(p[:, _C_IK:_C_IW], cos[:, :LANES], sin[:, :LANES])
    if page is None:
        (hg_ref, sq_ref, sk_ref, sv_ref, cq_ref, ck_ref, cv_ref, iq_ref, ik_ref, iw_ref) = out_refs
        sk_ref[...] = sk
        sv_ref[...] = sv
        ck_ref[...] = ck
        cv_ref[...] = cv
        ik_ref[...] = ik[:, :HEAD_DIM]
    else:
        (hg_ref, sq_ref, skp_ref, svp_ref, skt_ref, svb_ref, cq_ref, ckp_ref, cvp_ref, ckt_ref, cvo_ref,
         iq_ref, ikt_ref, ikp_ref, iw_ref) = out_refs
        skt, svt, ckt, cvt, ikt = sk.T, sv.T, ck.T, cv.T, ik.T[:HEAD_DIM]
        for pg in range(p.shape[0] // page):
            sl = slice(pg * page, (pg + 1) * page)
            skp_ref[pg] = skt[:, sl]
            svp_ref[pg] = svt[:, sl]
            ckp_ref[pg] = ckt[:, sl]
            cvp_ref[pg] = cvt[:, sl]
            ikp_ref[pg] = ikt[:, sl]
        skt_ref[0] = skt.astype(BF16)
        ckt_ref[0] = ckt.astype(BF16)
        ikt_ref[0] = ikt.astype(BF16)
        svb_ref[...] = sv.astype(BF16)
        cvb = cv.astype(BF16)
        ones = jnp.ones((cvb.shape[0], HEAD_DIM), BF16)
        pieces = []
        for hd in range(N_HEADS):
            pieces += [cvb[:, hd * HEAD_DIM:(hd + 1) * HEAD_DIM], ones]
        cvo_ref[...] = jnp.concatenate(pieces, axis=1)
    hg_ref[...] = hg
    sq_ref[...] = sq
    cq_ref[...] = cq
    iq_ref[...] = iq
    iw_ref[...] = p[:, _C_IW:]


def _inproj(x, g, w_pad, cos, sin, block_rows, page=None, key_tiles=None):
    rows, d = x.shape
    nblk = rows // block_rows
    row = lambda w: pl.BlockSpec((block_rows, w), lambda i: (i, 0))
    f32o = lambda w: (jax.ShapeDtypeStruct((rows, w), F32), row(w))
    bfo = lambda w: (jax.ShapeDtypeStruct((rows, w), BF16), row(w))
    if page is None:
        outs = [f32o(4 * HG_W), bfo(ATT_W), f32o(ATT_W), f32o(ATT_W), bfo(ATT_W), f32o(ATT_W), f32o(ATT_W),
                bfo(ATT_W), f32o(HEAD_DIM), f32o(LANES)]
    else:
        ppb = block_rows // page
        pages = lambda w: (jax.ShapeDtypeStruct((rows // page, w, page), F32),
                           pl.BlockSpec((ppb, w, page), lambda i: (i, 0, 0)))
        sb_tile, dsa_tile = key_tiles

        def tiles(w, kt):
            bpt = kt // block_rows
            return (jax.ShapeDtypeStruct((rows // kt, w, kt), BF16),
                    pl.BlockSpec((1, w, block_rows), lambda i: (i // bpt, 0, i % bpt)))
        outs = [f32o(4 * HG_W), bfo(ATT_W), pages(ATT_W), pages(ATT_W), tiles(ATT_W, sb_tile), bfo(ATT_W),
                bfo(ATT_W), pages(ATT_W), pages(ATT_W), tiles(ATT_W, dsa_tile), bfo(2 * ATT_W),
                bfo(ATT_W), tiles(HEAD_DIM, dsa_tile), pages(HEAD_DIM), f32o(LANES)]
    return pl.pallas_call(
        functools.partial(_inproj_body, page=page), grid=(nblk,),
        in_specs=[row(d), _const_spec((1, d)), _const_spec(w_pad.shape), row(ATT_W), row(ATT_W)],
        out_specs=tuple(o[1] for o in outs), out_shape=tuple(o[0] for o in outs),
        compiler_params=_params(("arbitrary",)),
    )(x, g, w_pad, cos, sin)


def _hgrn_body(hq_ref, hf_ref, hi_ref, hgate_ref, loglb_ref, log1mlb_ref, omlb_ref, ng_ref, s0_ref,
               o_ref, s_out_ref, st_ref, *, chunk, n_chunks):
    t = pl.program_id(1)

    @pl.when(t == 0)
    def _():
        for h in range(N_HEADS):
            st_ref[h] = s0_ref[0, h].T

    c = chunk
    ri = lax.broadcasted_iota(I32, (c, c), 0)
    ci = lax.broadcasted_iota(I32, (c, c), 1)
    tri = (ri >= ci).astype(F32)
    row = lax.broadcasted_iota(I32, (c, HG_W), 0)
    loglb, log1mlb, omlb, ng = loglb_ref[...], log1mlb_ref[...], omlb_ref[...], ng_ref[...]

    def one_chunk(ic, carry):
        r0 = pl.multiple_of(ic * c, c)
        q = hq_ref[pl.ds(r0, c), :]
        x = hf_ref[pl.ds(r0, c), :]
        v = hi_ref[pl.ds(r0, c), :]
        gate = hgate_ref[pl.ds(r0, c), :]
        qf = q * jax.nn.sigmoid(q)
        log_sig = jnp.minimum(x, 0.0) - jnp.log1p(jnp.exp(-jnp.abs(x)))
        bterm = log1mlb + log_sig
        log_f = jnp.maximum(loglb, bterm) + jnp.log1p(jnp.exp(-jnp.abs(loglb - bterm)))
        kf = omlb * jax.nn.sigmoid(-x)
        b = jnp.dot(tri, log_f, preferred_element_type=F32, precision=lax.Precision.HIGHEST)
        b_last = b[c - 1:c, :]
        qe = qf * jnp.exp(b)
        kk = kf * jnp.exp(b_last - b)
        e_last = jnp.exp(b_last)
        o_intra = [jnp.zeros((c, HG_D), F32) for _ in range(N_HEADS)]
        for s in range(c):
            w = jnp.exp(jnp.where(row >= s, b - b[s:s + 1, :], -jnp.inf)) * kf[s:s + 1, :]
            pw = qf * w
            for h in range(N_HEADS):
                sl = slice(h * HG_D, (h + 1) * HG_D)
                attn = jnp.sum(pw[:, sl], axis=1, keepdims=True)
                o_intra[h] = o_intra[h] + attn * v[s:s + 1, sl]
        outs = []
        for h in range(N_HEADS):
            sl = slice(h * HG_D, (h + 1) * HG_D)
            st = st_ref[h]
            o = o_intra[h] + _dot_nt(qe[:, sl].astype(BF16), st.astype(BF16))
            upd = lax.dot_general(v[:, sl].astype(BF16), kk[:, sl].astype(BF16), (((0,), (0,)), ((), ())),
                                  preferred_element_type=F32)
            st_ref[h] = st * e_last[:, sl] + upd
            o = o * lax.rsqrt(jnp.mean(o * o, axis=1, keepdims=True) + EPS)
            outs.append(o)
        o_all = jnp.concatenate(outs, axis=1) * ng * (gate * jax.nn.sigmoid(gate))
        o_ref[pl.ds(r0, c), :] = o_all
        return carry

    _loop_pairs(0, n_chunks, one_chunk, 0)

    @pl.when(t == pl.num_programs(1) - 1)
    def _():
        for h in range(N_HEADS):
            s_out_ref[0, h] = st_ref[h].T


def _hgrn(hg4, lb_terms, ng, s0, batch, seq, block_rows, chunk):
    nt = seq // block_rows
    col = lambda cidx: pl.BlockSpec((block_rows, HG_W), lambda b, t: (b * nt + t, cidx))
    loglb, log1mlb, omlb = lb_terms
    body = functools.partial(_hgrn_body, chunk=chunk, n_chunks=block_rows // chunk)
    return pl.pallas_call(
        body, grid=(batch, nt),
        in_specs=[col(0), col(1), col(2), col(3),
                  _const_spec((1, HG_W)), _const_spec((1, HG_W)), _const_spec((1, HG_W)), _const_spec((1, HG_W)),
                  pl.BlockSpec((1, N_HEADS, HG_D, HG_D), lambda b, t: (b, 0, 0, 0))],
        out_specs=(pl.BlockSpec((block_rows, HG_W), lambda b, t: (b * nt + t, 0)),
                   pl.BlockSpec((1, N_HEADS, HG_D, HG_D), lambda b, t: (b, 0, 0, 0))),
        out_shape=(jax.ShapeDtypeStruct((batch * seq, HG_W), F32),
                   jax.ShapeDtypeStruct((batch, N_HEADS, HG_D, HG_D), F32)),
        scratch_shapes=[pltpu.VMEM((N_HEADS, HG_D, HG_D), F32)],
        compiler_params=_params(("arbitrary", "arbitrary")),
    )(hg4, hg4, hg4, hg4, loglb, log1mlb, omlb, ng, s0)


def _pages_per_step(n_pages):
    return math.gcd(n_pages, 32)


def _group_width(pps, page):
    return 2 * page if pps % 2 == 0 else page


def _chunk(refs):
    return jnp.concatenate([r[...].astype(BF16) for r in refs], axis=1)


def _page_specs(layer, n_pages, pps, page, width, reverse):
    nc = n_pages // pps

    def spec(i):
        def index(b, j, pt):
            chunk = (nc - 1 - j) if reverse else j
            return (layer, pt[b, chunk * pps + i], 0, 0)
        return pl.BlockSpec((None, None, width, page), index)
    return [spec(i) for i in range(pps)]


def _per_batch(shape):
    return pl.BlockSpec((1,) + shape, lambda b, j, pt: (b,) + (0,) * len(shape))


def _suffix_matrix(tk):
    j = lax.broadcasted_iota(I32, (2 * tk, tk), 0) % tk
    s = lax.broadcasted_iota(I32, (2 * tk, tk), 1)
    return (j > s).astype(BF16)


def _sb_exponents(z, visible, suffix):
    gw = suffix.shape[1]
    z2 = z * LOG2E
    log_sig = jnp.minimum(z2, 0.0) - jnp.log2(1.0 + jnp.exp2(-jnp.abs(z2)))
    log_keep = log_sig - z2
    if visible is not None:
        log_keep = jnp.where(visible, log_keep, 0.0)
        log_sig = jnp.where(visible, log_sig, NEG)
    exps, totals = [], []
    for g in range(z.shape[1] // gw):
        lk = log_keep[:, g * gw:(g + 1) * gw]
        hi = lk.astype(BF16)
        lo = (lk - hi.astype(F32)).astype(BF16)
        ex = _dot(jnp.concatenate([hi, lo], axis=1), suffix)
        exps.append(log_sig[:, g * gw:(g + 1) * gw] + ex)
        totals.append(ex[:, 0:1] + lk[:, 0:1])
    return exps, totals


def _sb_weights(z, visible, carry, suffix):
    exps, totals = _sb_exponents(z, visible, suffix)
    parts = [None] * len(exps)
    for g in reversed(range(len(exps))):
        parts[g] = jnp.exp2(exps[g] + carry).astype(BF16)
        carry = carry + totals[g]
    return (parts[0] if len(parts) == 1 else jnp.concatenate(parts, axis=1)), carry


def _sb_prompt_body(q_ref, kt_ref, v_ref, o_ref, acc_ref, *, tq, tk):
    qi = pl.program_id(0)
    qs = _head_stack(q_ref[...])
    suffix = _suffix_matrix(min(tk, 2 * LANES))
    rows = N_HEADS * tq
    row_pos = qi * tq + lax.broadcasted_iota(I32, (rows, tk), 0) % tq
    col = lax.broadcasted_iota(I32, (rows, tk), 1)
    acc_ref[...] = jnp.zeros_like(acc_ref)

    def tile(kb, carry, masked):
        ks = pl.multiple_of(kb * tk, tk)
        visible = (col + kb * tk < row_pos) if masked else None
        a, carry = _sb_weights(_dot(qs, kt_ref[kb]), visible, carry, suffix)
        acc_ref[...] += _dot(a, v_ref[pl.ds(ks, tk), :])
        return carry

    kb_last = (qi * tq + tq - 1) // tk
    n_full = (qi * tq) // tk
    carry = tile(kb_last, jnp.zeros((rows, 1), F32), True)
    carry = lax.fori_loop(0, kb_last - n_full, lambda j, c: tile(kb_last - 1 - j, c, True), carry)
    _loop_pairs(0, n_full, lambda j, c: tile(n_full - 1 - j, c, False), carry)
    o_ref[...] = _head_merge(acc_ref[...], tq)


def _sb_prompt(q, kt, v, tq):
    t = q.shape[0]
    tk = kt.shape[2]
    body = functools.partial(_sb_prompt_body, tq=tq, tk=tk)
    return pl.pallas_call(
        body, grid=(t // tq,),
        in_specs=[pl.BlockSpec((tq, ATT_W), lambda i: (i, 0)), _const_spec(kt.shape), _const_spec((t, ATT_W))],
        out_specs=pl.BlockSpec((tq, ATT_W), lambda i: (i, 0)),
        out_shape=jax.ShapeDtypeStruct((t, ATT_W), F32),
        scratch_shapes=[pltpu.VMEM((N_HEADS * tq, ATT_W), F32)],
        compiler_params=_params(("arbitrary",)),
    )(q, kt, v)


def _sb_sample_body(pt_ref, q_ref, *refs, pps, page):
    kp, vp = refs[:pps], refs[pps:2 * pps]
    kn_ref, vn_ref, o_ref, acc_ref, carry_ref = refs[2 * pps:]
    j = pl.program_id(1)
    tq = q_ref.shape[1]
    rows = N_HEADS * tq
    qs = _head_stack(q_ref[0])

    @pl.when(j == 0)
    def _():
        z = _dot(qs, kn_ref[0].astype(BF16))
        row = lax.broadcasted_iota(I32, (rows, page), 0) % tq
        col = lax.broadcasted_iota(I32, (rows, page), 1)
        a, carry = _sb_weights(z, col < row, jnp.zeros((rows, 1), F32), _suffix_matrix(page))
        acc_ref[...] = _dot_nt(a, vn_ref[0].astype(BF16))
        carry_ref[...] = carry

    z = _dot(qs, _chunk(kp))
    a, carry = _sb_weights(z, None, carry_ref[...], _suffix_matrix(_group_width(pps, page)))
    acc_ref[...] += _dot_nt(a, _chunk(vp))
    carry_ref[...] = carry

    @pl.when(j == pl.num_programs(1) - 1)
    def _():
        o_ref[0] = _head_merge(acc_ref[...], tq)


def _sb_sample(page_table, q, cache_k, cache_v, k_new, v_new, layer):
    nb, tq, _ = q.shape
    n_pages = page_table.shape[1]
    page = cache_k.shape[3]
    pps = _pages_per_step(n_pages)
    body = functools.partial(_sb_sample_body, pps=pps, page=page)
    grid_spec = pltpu.PrefetchScalarGridSpec(
        num_scalar_prefetch=1, grid=(nb, n_pages // pps),
        in_specs=[_per_batch((tq, ATT_W))] + _page_specs(layer, n_pages, pps, page, ATT_W, True)
        + _page_specs(layer, n_pages, pps, page, ATT_W, True) + [_per_batch((ATT_W, page)), _per_batch((ATT_W, page))],
        out_specs=_per_batch((tq, ATT_W)),
        scratch_shapes=[pltpu.VMEM((N_HEADS * tq, ATT_W), F32), pltpu.VMEM((N_HEADS * tq, 1), F32)])
    return pl.pallas_call(
        body, grid_spec=grid_spec, out_shape=jax.ShapeDtypeStruct((nb, tq, ATT_W), F32),
        compiler_params=_params(("arbitrary", "arbitrary")),
    )(page_table, q, *([cache_k] * pps), *([cache_v] * pps), k_new, v_new)


def _score_keys(dots4, iw_b, tq, visible):
    score = jnp.zeros((tq, dots4.shape[1]), F32)
    for h in range(N_HEADS):
        score = score + jnp.maximum(dots4[h * tq:(h + 1) * tq], 0.0) * iw_b[h]
    score = score + 0.0
    if visible is not None:
        score = jnp.where(visible, score, NEG)
    bits = pltpu.bitcast(score, I32)
    return jnp.where(bits < 0, bits ^ 0x7FFFFFFF, bits)


def _kth_largest(count_ge, shape, k, bits):
    def step(i, v):
        cand = v | lax.shift_left(jnp.int32(1), bits - 1 - i)
        return jnp.where(count_ge(cand) >= k, cand, v)
    return lax.fori_loop(0, bits, step, jnp.zeros(shape, I32))


def _prefix_matrix(tk):
    j = lax.broadcasted_iota(I32, (tk, tk), 0)
    s = lax.broadcasted_iota(I32, (tk, tk), 1)
    return (j < s).astype(BF16)


def _additive_mask(key, vstar, need, eq_carry, prefix, visible):
    gw = prefix.shape[0]
    eq = key == vstar
    eqf = jnp.where(eq, 1.0, 0.0)
    before = []
    for g in range(key.shape[1] // gw):
        e = eqf[:, g * gw:(g + 1) * gw]
        before.append(_dot(e.astype(BF16), prefix) + eq_carry)
        eq_carry = eq_carry + jnp.sum(e, axis=1, keepdims=True)
    before = before[0] if len(before) == 1 else jnp.concatenate(before, axis=1)
    am = jnp.where(key > vstar, 0.0, jnp.where(eq, jnp.where(before < need, 0.0, NEG), NEG))
    if visible is not None:
        am = jnp.where(visible, am, NEG)
    return am, eq_carry


def _dsa_prompt_body(iq_ref, iw_ref, cq_ref, ikt_ref, ckt_ref, cvo_ref, o_ref,
                     keys_ref, half_ref, mx_ref, acc_ref, *, tq, tk, topk):
    qi = pl.program_id(0)
    n_tiles = (qi * tq + tq - 1) // tk + 1
    n_full = (qi * tq + 1) // tk
    row_pos = qi * tq + lax.broadcasted_iota(I32, (tq, tk), 0)
    col = lax.broadcasted_iota(I32, (tq, tk), 1)

    iq4 = _head_rows(iq_ref[...])
    iw = iw_ref[...]
    iw_b = [jnp.broadcast_to(iw[:, h:h + 1], (tq, tk)) for h in range(N_HEADS)]

    def score_tile(kb, masked):
        visible = (col + kb * tk <= row_pos) if masked else None
        key = _score_keys(_dot(iq4, ikt_ref[kb]), iw_b, tq, visible)
        keys_ref[kb] = key
        half_ref[kb] = lax.shift_right_arithmetic(key, 16).astype(I16)

    def score_full(kb, c):
        score_tile(kb, False)
        return c

    def score_part(kb, c):
        score_tile(kb, True)
        return c

    _loop_pairs(0, n_full, score_full, 0)
    lax.fori_loop(n_full, n_tiles, score_part, 0)

    cw = min(tk, 2 * LANES)

    def count16(pred):
        def add(kb, cnt):
            x = half_ref[kb]
            for g in range(tk // cw):
                cnt = cnt + jnp.where(pred(x[:, g * cw:(g + 1) * cw]), jnp.int16(1), jnp.int16(0))
            return cnt
        cnt = _loop_pairs(0, n_tiles, add, jnp.zeros((tq, cw), I16))
        return jnp.sum(cnt.astype(I32), axis=1, keepdims=True)

    def half_b(v):
        return jnp.broadcast_to(v, (tq, cw)).astype(I16)

    def kth16(k, lo, hi):
        passes = jnp.max((32 - lax.clz(hi - lo)).astype(F32)).astype(I32)

        def step(i, bounds):
            lo, hi = bounds
            mid = lo + lax.shift_right_arithmetic(hi - lo + 1, 1)
            mid_b = half_b(mid)
            ok = count16(lambda x: x >= mid_b) >= k
            return jnp.where(ok, mid, lo), jnp.where(ok, hi, mid - 1)

        return lax.fori_loop(0, passes, step, (lo, hi))[0]

    def lane_max(kb, m):
        x = half_ref[kb]
        for g in range(tk // cw):
            xg = x[:, g * cw:(g + 1) * cw]
            m = jnp.where(xg > m, xg, m)
        return m

    lane_top = _loop_pairs(0, n_tiles, lane_max, jnp.full((tq, cw), I16_MIN, I16)).astype(I32).astype(F32)
    top_hi = jnp.max(lane_top, axis=1, keepdims=True).astype(I32)
    top_lo = jnp.min(lane_top, axis=1, keepdims=True).astype(I32) if topk <= cw else jnp.full((tq, 1), I16_MIN, I32)
    vh = kth16(topk, top_lo, top_hi)
    vh_b = half_b(vh)
    k_low = topk - count16(lambda x: x > vh_b)
    vh_b32 = jnp.broadcast_to(vh, (tq, tk))

    def fill_low(kb, c):
        key = keys_ref[kb]
        low = (key & 0xFFFF) + I16_MIN
        in_group = lax.shift_right_arithmetic(key, 16) == vh_b32
        half_ref[kb] = jnp.where(in_group, low, I16_MIN).astype(I16)
        return c

    lax.fori_loop(0, n_tiles, fill_low, 0)
    vl = kth16(k_low, jnp.full((tq, 1), I16_MIN, I32), jnp.full((tq, 1), -I16_MIN - 1, I32))
    vl_b = half_b(vl)
    need = (k_low - count16(lambda x: x > vl_b)).astype(F32)
    vstar = lax.shift_left(vh, 16) | (vl - I16_MIN)
    vstar_b = jnp.broadcast_to(vstar, (tq, tk))
    need_b = jnp.broadcast_to(need, (tq, tk))

    cqs = _head_stack(cq_ref[...])
    prefix = _prefix_matrix(min(tk, 2 * LANES))
    mx_ref[...] = jnp.full_like(mx_ref, NEG)

    def select(kb, eq_carry, masked):
        visible = (col + kb * tk <= row_pos) if masked else None
        am, eq_carry = _additive_mask(keys_ref[kb], vstar_b, need_b, eq_carry, prefix, visible)
        keys_ref[kb] = pltpu.bitcast(am, I32)
        logits4 = _dot(cqs, ckt_ref[kb])
        for h in range(N_HEADS):
            lg = logits4[h * tq:(h + 1) * tq] + am
            top = lg[:, :LANES]
            for g in range(1, tk // LANES):
                top = jnp.maximum(top, lg[:, g * LANES:(g + 1) * LANES])
            mx_ref[h] = jnp.maximum(mx_ref[h], top)
        return eq_carry

    eq_carry = _loop_pairs(0, n_full, lambda kb, c: select(kb, c, False), jnp.zeros((tq, 1), F32))
    lax.fori_loop(n_full, n_tiles, lambda kb, c: select(kb, c, True), eq_carry)

    m_b = [jnp.broadcast_to(jnp.max(mx_ref[h], axis=1, keepdims=True), (tq, tk)) for h in range(N_HEADS)]
    acc_ref[...] = jnp.zeros_like(acc_ref)

    def attend(kb, c):
        ks = pl.multiple_of(kb * tk, tk)
        am = pltpu.bitcast(keys_ref[kb], F32)
        logits4 = _dot(cqs, ckt_ref[kb])
        vo = cvo_ref[pl.ds(ks, tk), :]
        for h in range(N_HEADS):
            p = jnp.exp(logits4[h * tq:(h + 1) * tq] + am - m_b[h]).astype(BF16)
            acc_ref[h] += _dot(p, vo[:, h * 2 * HEAD_DIM:(h + 1) * 2 * HEAD_DIM])
        return c

    _loop_pairs(0, n_tiles, attend, 0)
    outs = []
    for h in range(N_HEADS):
        acc = acc_ref[h]
        outs.append(acc[:, :HEAD_DIM] / acc[:, HEAD_DIM:HEAD_DIM + 1])
    o_ref[...] = jnp.concatenate(outs, axis=1)


def _dsa_prompt(iq, iw, cq, ikt, ckt, cvo, tq, topk):
    t = iq.shape[0]
    tk = ckt.shape[2]
    body = functools.partial(_dsa_prompt_body, tq=tq, tk=tk, topk=topk)
    row = lambda w: pl.BlockSpec((tq, w), lambda i: (i, 0))
    return pl.pallas_call(
        body, grid=(t // tq,),
        in_specs=[row(ATT_W), row(LANES), row(ATT_W),
                  _const_spec(ikt.shape), _const_spec(ckt.shape), _const_spec(cvo.shape)],
        out_specs=row(ATT_W),
        out_shape=jax.ShapeDtypeStruct((t, ATT_W), F32),
        scratch_shapes=[pltpu.VMEM((t // tk, tq, tk), I32), pltpu.VMEM((t // tk, tq, tk), I16),
                        pltpu.VMEM((N_HEADS, tq, LANES), F32), pltpu.VMEM((N_HEADS, tq, 2 * HEAD_DIM), F32)],
        compiler_params=_params(("arbitrary",)),
    )(iq, iw, cq, ikt, ckt, cvo)


def _dsa_score_sample_body(pt_ref, iq_ref, iw_ref, *refs, pps, page, topk):
    ikp = refs[:pps]
    ikn_ref, keys_ref, keysn_ref, vstar_ref, need_ref, all_ref, alln_ref = refs[pps:]
    j = pl.program_id(1)
    nc = pl.num_programs(1)
    tq = iq_ref.shape[1]
    cw = pps * page
    iq4 = _head_rows(iq_ref[0])
    iw = iw_ref[0]
    iw_b = lambda n: [jnp.broadcast_to(iw[:, h:h + 1], (tq, n)) for h in range(N_HEADS)]
    key = _score_keys(_dot(iq4, _chunk(ikp)), iw_b(cw), tq, None)
    keys_ref[0, 0] = key
    all_ref[j] = key

    @pl.when(j == nc - 1)
    def _():
        row = lax.broadcasted_iota(I32, (tq, page), 0)
        col = lax.broadcasted_iota(I32, (tq, page), 1)
        key_n = _score_keys(_dot(iq4, ikn_ref[0].astype(BF16)), iw_b(page), tq, col <= row)
        keysn_ref[0] = key_n
        alln_ref[...] = key_n

        def count(pred):
            def add(c, cnt):
                return cnt + jnp.where(pred(all_ref[c]), 1, 0)
            cnt = lax.fori_loop(0, nc, add, jnp.zeros((tq, cw), I32))
            cnt_n = jnp.where(pred(alln_ref[...]), 1, 0)
            return jnp.sum(cnt, axis=1, keepdims=True) + jnp.sum(cnt_n, axis=1, keepdims=True)

        vstar = _kth_largest(lambda cand: count(lambda x: x >= (cand ^ INT_MIN)), (tq, 1), topk, 32) ^ INT_MIN
        need = (topk - count(lambda x: x > vstar)).astype(F32)
        vstar_ref[0] = jnp.broadcast_to(vstar, (tq, page))
        need_ref[0] = jnp.broadcast_to(need, (tq, page))


def _dsa_score_sample(page_table, iq, iw, cache_ik, ik_new, layer, topk):
    nb, tq, _ = iq.shape
    n_pages = page_table.shape[1]
    page = cache_ik.shape[3]
    pps = _pages_per_step(n_pages)
    nc, cw = n_pages // pps, pps * page
    body = functools.partial(_dsa_score_sample_body, pps=pps, page=page, topk=topk)
    grid_spec = pltpu.PrefetchScalarGridSpec(
        num_scalar_prefetch=1, grid=(nb, nc),
        in_specs=[_per_batch((tq, ATT_W)), _per_batch((tq, LANES))]
        + _page_specs(layer, n_pages, pps, page, HEAD_DIM, False) + [_per_batch((HEAD_DIM, page))],
        out_specs=(pl.BlockSpec((1, 1, tq, cw), lambda b, j, pt: (b, j, 0, 0)),
                   _per_batch((tq, page)), _per_batch((tq, page)), _per_batch((tq, page))),
        scratch_shapes=[pltpu.VMEM((nc, tq, cw), I32), pltpu.VMEM((tq, page), I32)])
    return pl.pallas_call(
        body, grid_spec=grid_spec,
        out_shape=(jax.ShapeDtypeStruct((nb, nc, tq, cw), I32), jax.ShapeDtypeStruct((nb, tq, page), I32),
                   jax.ShapeDtypeStruct((nb, tq, page), I32), jax.ShapeDtypeStruct((nb, tq, page), F32)),
        compiler_params=_params(("arbitrary", "arbitrary")),
    )(page_table, iq, iw, *([cache_ik] * pps), ik_new)


def _dsa_attend_sample_body(pt_ref, cq_ref, keys_ref, keysn_ref, vstar_ref, need_ref, *refs, pps, page):
    kp, vp = refs[:pps], refs[pps:2 * pps]
    kn_ref, vn_ref, o_ref, acc_ref, m_ref, l_ref, eq_ref = refs[2 * pps:]
    j = pl.program_id(1)
    tq = cq_ref.shape[1]

    @pl.when(j == 0)
    def _():
        acc_ref[...] = jnp.zeros_like(acc_ref)
        m_ref[...] = jnp.full_like(m_ref, NEG)
        l_ref[...] = jnp.zeros_like(l_ref)
        eq_ref[...] = jnp.zeros_like(eq_ref)

    cqs = _head_stack(cq_ref[0])
    vstar = vstar_ref[0][:, 0:1]
    need = need_ref[0][:, 0:1]

    def step(key, k_tile, v_tile, visible, gw):
        am, eq_carry = _additive_mask(key, vstar, need, eq_ref[...], _prefix_matrix(gw), visible)
        eq_ref[...] = eq_carry
        am4 = jnp.concatenate([am] * N_HEADS, axis=0)
        lg = _dot(cqs, k_tile) + am4
        m_old = m_ref[...]
        m_new = jnp.maximum(m_old, jnp.max(lg, axis=1, keepdims=True))
        p = jnp.where(am4 == 0.0, jnp.exp(lg - m_new), 0.0)
        alpha = jnp.exp(m_old - m_new)
        l_ref[...] = l_ref[...] * alpha + jnp.sum(p, axis=1, keepdims=True)
        acc_ref[...] = acc_ref[...] * alpha + _dot_nt(p.astype(BF16), v_tile)
        m_ref[...] = m_new

    step(keys_ref[0, 0], _chunk(kp), _chunk(vp), None, _group_width(pps, page))

    @pl.when(j == pl.num_programs(1) - 1)
    def _():
        row = lax.broadcasted_iota(I32, (tq, page), 0)
        col = lax.broadcasted_iota(I32, (tq, page), 1)
        step(keysn_ref[0], kn_ref[0].astype(BF16), vn_ref[0].astype(BF16), col <= row, page)
        o_ref[0] = _head_merge(acc_ref[...] / l_ref[...], tq)


def _dsa_attend_sample(page_table, cq, keys, keys_new, vstar, need, cache_k, cache_v, k_new, v_new, layer):
    nb, tq, _ = cq.shape
    n_pages = page_table.shape[1]
    page = cache_k.shape[3]
    pps = _pages_per_step(n_pages)
    nc, cw = n_pages // pps, pps * page
    body = functools.partial(_dsa_attend_sample_body, pps=pps, page=page)
    grid_spec = pltpu.PrefetchScalarGridSpec(
        num_scalar_prefetch=1, grid=(nb, nc),
        in_specs=[_per_batch((tq, ATT_W)),
                  pl.BlockSpec((1, 1, tq, cw), lambda b, j, pt: (b, j, 0, 0)),
                  _per_batch((tq, page)), _per_batch((tq, page)), _per_batch((tq, page))]
        + _page_specs(layer, n_pages, pps, page, ATT_W, False) + _page_specs(layer, n_pages, pps, page, ATT_W, False)
        + [_per_batch((ATT_W, page)), _per_batch((ATT_W, page))],
        out_specs=_per_batch((tq, ATT_W)),
        scratch_shapes=[pltpu.VMEM((N_HEADS * tq, ATT_W), F32),
                        pltpu.VMEM((N_HEADS * tq, 1), F32), pltpu.VMEM((N_HEADS * tq, 1), F32),
                        pltpu.VMEM((tq, 1), F32)])
    return pl.pallas_call(
        body, grid_spec=grid_spec, out_shape=jax.ShapeDtypeStruct((nb, tq, ATT_W), F32),
        compiler_params=_params(("arbitrary", "arbitrary")),
    )(page_table, cq, keys, keys_new, vstar, need, *([cache_k] * pps), *([cache_v] * pps), k_new, v_new)


def _ffn_body(*refs, seq_rows, carried, final):
    it = iter(refs)
    x_ref, oa_ref, ob_ref, oc_ref, wout_ref, g_ref, wup_ref, cw_ref, cb_ref, wdown_ref = (next(it) for _ in range(10))
    e0_ref = e1_ref = gf_ref = carry_ref = None
    if not carried:
        e0_ref, e1_ref = next(it), next(it)
    if final:
        gf_ref = next(it)
    xo_ref, tail_ref = next(it), next(it)
    if carried:
        carry_ref = next(it)

        @pl.when(pl.program_id(0) == 0)
        def _():
            carry_ref[...] = jnp.zeros_like(carry_ref)

    mixed = jnp.concatenate([oa_ref[...], ob_ref[...], oc_ref[...]], axis=1).astype(BF16)
    x1 = x_ref[...] + _dot(mixed, wout_ref[...])
    h = _rmsnorm(x1, g_ref[...]).astype(BF16)
    up = _dot(h, wup_ref[...])
    d_ff = up.shape[1] // 2
    a, gate = up[:, :d_ff], up[:, d_ff:]
    rows = a.shape[0]
    r = lax.broadcasted_iota(I32, a.shape, 0) % seq_rows
    if carried:
        nc = carry_ref.shape[0]
        e0 = jnp.broadcast_to(carry_ref[nc - 2:nc - 1, :], a.shape)
        e1 = jnp.broadcast_to(carry_ref[nc - 1:nc, :], a.shape)
    else:
        e0, e1 = e0_ref[...], e1_ref[...]
    a_m1 = jnp.where(r == 0, e1, pltpu.roll(a, 1, 0))
    a_m2 = jnp.where(r == 0, e0, jnp.where(r == 1, e1, pltpu.roll(a, 2, 0)))
    cw = cw_ref[...]
    c = cb_ref[...] + a_m2 * cw[0:1, :] + a_m1 * cw[1:2, :] + a * cw[2:3, :]
    act = (c * jax.nn.sigmoid(c) * gate).astype(BF16)
    x2 = x1 + _dot(act, wdown_ref[...])
    xo_ref[...] = _rmsnorm(x2, gf_ref[...]) if final else x2
    tail = tail_ref.shape[0]
    tail_ref[...] = a[rows - tail:, :]
    if carried:
        nc = carry_ref.shape[0]
        carry_ref[...] = a[rows - nc:, :]


def _ffn(x, oa, ob, oc, w_out, g, w_up, conv_w, conv_b, w_down, block_rows, prev=None, final_g=None):
    rows, d = x.shape
    d_ff = w_down.shape[0]
    carried = prev is None
    final = final_g is not None
    row = lambda w: pl.BlockSpec((block_rows, w), lambda i: (i, 0))
    ins = [x, oa, ob, oc, w_out, g, w_up, conv_w, conv_b, w_down]
    in_specs = [row(d), row(HG_W), row(ATT_W), row(ATT_W), _const_spec(w_out.shape), _const_spec((1, d)),
                _const_spec(w_up.shape), _const_spec(conv_w.shape), _const_spec((1, d_ff)), _const_spec(w_down.shape)]
    if not carried:
        ins += list(prev)
        in_specs += [row(d_ff), row(d_ff)]
    if final:
        ins.append(final_g)
        in_specs.append(_const_spec((1, d)))
    if carried:
        tail_rows, tail_spec = 8, pl.BlockSpec((8, d_ff), lambda i: (0, 0))
        scratch = [pltpu.VMEM((8, d_ff), F32)]
        seq_rows = block_rows
    else:
        tail_rows, tail_spec = rows, row(d_ff)
        scratch = []
        seq_rows = 8
    body = functools.partial(_ffn_body, seq_rows=seq_rows, carried=carried, final=final)
    return pl.pallas_call(
        body, grid=(rows // block_rows,),
        in_specs=in_specs, out_specs=(row(d), tail_spec),
        out_shape=(jax.ShapeDtypeStruct((rows, d), F32), jax.ShapeDtypeStruct((tail_rows, d_ff), F32)),
        scratch_shapes=scratch,
        compiler_params=_params(("arbitrary",)),
    )(*ins)


def _rope_tables(pos):
    half = HEAD_DIM // 2
    inv = jnp.exp(-math.log(ROPE_THETA) * jnp.arange(half, dtype=F32) / half)
    ang = pos.astype(F32)[:, None] * inv[None, :]
    cos, sin = jnp.cos(ang), jnp.sin(ang)
    cos = jnp.tile(jnp.concatenate([cos, cos], axis=1), (1, N_HEADS))
    sin = jnp.tile(jnp.concatenate([-sin, sin], axis=1), (1, N_HEADS))
    return cos, sin


def _w_in_body(w_ref, o_ref, *, n_iw):
    i = pl.program_id(0)
    last = pl.num_programs(0) - 1
    x = w_ref[...]
    r = lax.broadcasted_iota(I32, (x.shape[0], x.shape[2]), 0)
    for l in range(x.shape[1]):
        xl = x[:, l, :]
        ik = jnp.where(r < HEAD_DIM, xl, 0.0)
        iw = jnp.where(r < n_iw, pltpu.roll(xl, LANES - HEAD_DIM, 0), 0.0)
        y = jnp.where(i == last, iw, jnp.where(i == last - 1, ik, xl))
        o_ref[l] = y.T.astype(BF16)


def _pad_w_in(w_in):
    depth, d, n = w_in.shape
    n_in = pl.cdiv(n, LANES)
    assert n_in * LANES == _C_IW and n > _C_IK + HEAD_DIM
    return pl.pallas_call(
        functools.partial(_w_in_body, n_iw=n - _C_IK - HEAD_DIM), grid=(IN_PAD // LANES,),
        in_specs=[pl.BlockSpec((LANES, depth, d), lambda i: (jnp.minimum(i, n_in - 1), 0, 0))],
        out_specs=pl.BlockSpec((depth, d, LANES), lambda i: (0, 0, i)),
        out_shape=jax.ShapeDtypeStruct((depth, d, IN_PAD), BF16),
        compiler_params=_params(("arbitrary",)),
    )(jnp.transpose(w_in, (2, 0, 1)))


def _new_page(a, page):
    return jnp.swapaxes(jnp.pad(a, ((0, 0), (0, page - a.shape[1]), (0, 0))), 1, 2)


def _feature_major(cache):
    depth, pool, page = cache.shape[:3]
    perm = (0, 1, 3, 4, 2) if cache.ndim == 5 else (0, 1, 3, 2)
    return jnp.transpose(cache, perm).reshape(depth, pool, -1, page)


def _slot_major(pages, lead):
    depth, n_pg, w, page = pages.shape
    if w == HEAD_DIM:
        return jnp.transpose(pages, (0, 1, 3, 2)).reshape((depth,) + lead + (n_pg, page, w))
    a = pages.reshape(depth, n_pg, N_HEADS, HEAD_DIM, page)
    return jnp.transpose(a, (0, 1, 4, 2, 3)).reshape((depth,) + lead + (n_pg, page, N_HEADS, HEAD_DIM))


def kernel(x_prompt, x_sample, cache_sb_k, cache_sb_v, cache_dsa_k, cache_dsa_v, cache_idx_k, state_hgrn, state_conv, page_table, hg_lower_bounds, attn_norm, w_in, hg_norm, w_out, ffn_norm, w_up, conv_w, conv_b, w_down, final_norm):
    depth = w_in.shape[0]
    nb_p, seq, d = x_prompt.shape
    nb_s, seq_s, _ = x_sample.shape
    assert nb_p == 1, "the prompt group is one sequence"
    page = cache_sb_k.shape[2]
    n_pages = page_table.shape[1]
    past = n_pages * page
    d_ff = w_down.shape[1]
    rows_s = nb_s * seq_s

    lb = jnp.cumsum(jax.nn.softmax(hg_lower_bounds.astype(F32), axis=0), axis=0)
    lb = (lb - lb[0]).reshape(depth, 1, HG_W)
    cos_p, sin_p = _rope_tables(jnp.arange(seq, dtype=I32))
    cos_s, sin_s = _rope_tables(jnp.tile(past + jnp.arange(seq_s, dtype=I32), nb_s))

    cache_sb_k, cache_sb_v = _feature_major(cache_sb_k), _feature_major(cache_sb_v)
    cache_dsa_k, cache_dsa_v = _feature_major(cache_dsa_k), _feature_major(cache_dsa_v)
    cache_idx_k = _feature_major(cache_idx_k)

    tq = 128
    block_rows = 256
    key_tiles = (min(512, seq), min(1024, seq))
    assert all(seq % kt == 0 and kt % block_rows == 0 for kt in key_tiles)
    topk_p = min(TOPK_MAX, seq // 4)
    topk_s = min(TOPK_MAX, (past + seq_s) // 4)

    w_in_pad = _pad_w_in(w_in)
    xp = x_prompt.reshape(seq, d)
    xs = x_sample.reshape(rows_s, d)
    p_rows, s_rows = [], []
    for l in range(depth):
        w_in_l = w_in_pad[l]
        g_attn = attn_norm[l].reshape(1, d)
        lb_terms = (jnp.log(lb[l]), jnp.log1p(-lb[l]), 1.0 - lb[l])
        ng = jnp.tile(hg_norm[l], N_HEADS).reshape(1, HG_W)
        w_out_l, w_up_l, w_down_l = w_out[l].astype(BF16), w_up[l].astype(BF16), w_down[l].astype(BF16)
        g_ffn = ffn_norm[l].reshape(1, d)
        cb = conv_b[l].reshape(1, d_ff)
        last = l == depth - 1
        final_g = final_norm.reshape(1, d) if last else None

        (hg4, sq, skp, svp, skt, svb, cq, ckp, cvp, ckt, cvo, iq, ikt, ikp, iw) = _inproj(
            xp, g_attn, w_in_l, cos_p, sin_p, block_rows, page=page, key_tiles=key_tiles)
        o_a, hg_new = _hgrn(hg4, lb_terms, ng, jnp.zeros((1, N_HEADS, HG_D, HG_D), F32), 1, seq, block_rows, 16)
        o_b = _sb_prompt(sq, skt, svb, tq)
        o_c = _dsa_prompt(iq, iw, cq, ikt, ckt, cvo, tq, topk_p)
        xp, tail = _ffn(xp, o_a, o_b, o_c, w_out_l, g_ffn, w_up_l, conv_w[l], cb, w_down_l, block_rows, final_g=final_g)
        p_rows.append((skp, svp, ckp, cvp, ikp, hg_new, tail[8 - (conv_w.shape[1] - 1):]))

        (hg4, sq, sk, sv, cq, ck, cv, iq, ik, iw) = _inproj(xs, g_attn, w_in_l, cos_s, sin_s, rows_s)
        o_a, hg_new = _hgrn(hg4, lb_terms, ng, state_hgrn[l], nb_s, seq_s, seq_s, seq_s)
        b3 = lambda a: a.reshape(nb_s, seq_s, a.shape[-1])
        new = lambda a: _new_page(b3(a), page)
        o_b = _sb_sample(page_table, b3(sq), cache_sb_k, cache_sb_v, new(sk), new(sv), l)
        keys, keys_new, vstar, need = _dsa_score_sample(page_table, b3(iq), b3(iw), cache_idx_k, new(ik), l, topk_s)
        o_c = _dsa_attend_sample(page_table, b3(cq), keys, keys_new, vstar, need, cache_dsa_k, cache_dsa_v,
                                 new(ck), new(cv), l)
        prev = (jnp.repeat(state_conv[l][:, 0], seq_s, axis=0), jnp.repeat(state_conv[l][:, 1], seq_s, axis=0))
        xs, a_rows = _ffn(xs, o_a, o_b.reshape(rows_s, ATT_W), o_c.reshape(rows_s, ATT_W), w_out_l, g_ffn, w_up_l,
                          conv_w[l], cb, w_down_l, rows_s, prev=prev, final_g=final_g)
        conv_new = a_rows.reshape(nb_s, seq_s, d_ff)[:, seq_s - (conv_w.shape[1] - 1):]
        s_rows.append((sk, sv, ck, cv, ik, hg_new, conv_new))

    stack = lambda rows, j: jnp.stack([r[j] for r in rows])
    heads = lambda a: _slot_major(a, (1,))
    heads_s = lambda a: a.reshape(depth, nb_s, seq_s, N_HEADS, HEAD_DIM)
    return (xp.reshape(1, seq, d), xs.reshape(nb_s, seq_s, d),
            heads(stack(p_rows, 0)), heads(stack(p_rows, 1)), heads(stack(p_rows, 2)), heads(stack(p_rows, 3)),
            heads(stack(p_rows, 4)),
            stack(p_rows, 5), stack(p_rows, 6).reshape(depth, 1, conv_w.shape[1] - 1, d_ff),
            heads_s(stack(s_rows, 0)), heads_s(stack(s_rows, 1)), heads_s(stack(s_rows, 2)), heads_s(stack(s_rows, 3)),
            stack(s_rows, 4).reshape(depth, nb_s, seq_s, HEAD_DIM),
            stack(s_rows, 5), stack(s_rows, 6))
```

```python
import functools
import math

import jax
import jax.numpy as jnp
from jax import lax
from jax.experimental import pallas as pl
from jax.experimental.pallas import tpu as pltpu

F32 = jnp.float32
BF16 = jnp.bfloat16
I32 = jnp.int32
I16 = jnp.int16

HEAD_DIM = 64
N_HEADS = 4
HG_D = 128
ATT_W = N_HEADS * HEAD_DIM
HG_W = N_HEADS * HG_D
TOPK_MAX = 256
ROPE_THETA = 10000.0
EPS = 1e-6
NEG = -1e30
LOG2E = 1.4426950408889634
INT_MIN = -2147483648
I16_MIN = -32768
VMEM_LIMIT_BYTES = 56 * 1024 * 1024
LANES = 128

_C_HG = 0
_C_SB = 4 * HG_W
_C_DSA = _C_SB + 3 * ATT_W
_C_IQ = _C_DSA + 3 * ATT_W
_C_IK = _C_IQ + ATT_W
_C_IW = _C_IK + LANES
IN_PAD = _C_IW + LANES


def _const_spec(shape):
    return pl.BlockSpec(shape, lambda *_: (0,) * len(shape), pipeline_mode=pl.Buffered(1))


def _params(sem):
    return pltpu.CompilerParams(dimension_semantics=sem, vmem_limit_bytes=VMEM_LIMIT_BYTES)


def _dot(a, b):
    return jnp.dot(a, b, preferred_element_type=F32)


def _dot_nt(a, b):
    return lax.dot_general(a, b, (((1,), (1,)), ((), ())), preferred_element_type=F32)


def _loop_quads(lo, hi, fn, carry):
    n = hi - lo

    def four(i, c):
        k = lo + 4 * i
        return fn(k + 3, fn(k + 2, fn(k + 1, fn(k, c))))

    carry = lax.fori_loop(0, n // 4, four, carry)
    return lax.fori_loop(lo + 4 * (n // 4), hi, fn, carry)


def _loop_pairs(lo, hi, fn, carry):
    n = hi - lo

    def two(i, c):
        k = lo + 2 * i
        return fn(k + 1, fn(k, c))

    carry = lax.fori_loop(0, n // 2, two, carry)
    return lax.cond(n % 2 == 1, lambda c: fn(hi - 1, c), lambda c: c, carry)


def _rmsnorm(x, g):
    return x * lax.rsqrt(jnp.mean(x * x, axis=-1, keepdims=True) + EPS) * g


def _head_rows(q):
    return jnp.concatenate([q[:, h * HEAD_DIM:(h + 1) * HEAD_DIM] for h in range(N_HEADS)], axis=0)


def _head_stack(q):
    lane = lax.broadcasted_iota(I32, q.shape, 1)
    zero = jnp.zeros_like(q)
    return jnp.concatenate([jnp.where(lane // HEAD_DIM == h, q, zero) for h in range(N_HEADS)], axis=0)


def _head_merge(acc, t):
    lane = lax.broadcasted_iota(I32, (t, ATT_W), 1)
    out = jnp.zeros((t, ATT_W), F32)
    for h in range(N_HEADS):
        out = jnp.where(lane // HEAD_DIM == h, acc[h * t:(h + 1) * t], out)
    return out


def _rope(x, cos, sin):
    lane = lax.broadcasted_iota(I32, x.shape, 1)
    first = (lane % HEAD_DIM) < (HEAD_DIM // 2)
    w = x.shape[1]
    swapped = jnp.where(first, pltpu.roll(x, w - HEAD_DIM // 2, 1), pltpu.roll(x, HEAD_DIM // 2, 1))
    return x * cos + swapped * sin


def _inproj_body(x_ref, g_ref, w_ref, cos_ref, sin_ref, *out_refs, page):
    h = _rmsnorm(x_ref[...], g_ref[...]).astype(BF16)
    p = _dot(h, w_ref[...])
    cos, sin = cos_ref[...], sin_ref[...]
    scale = HEAD_DIM ** -0.5
    hg = p[:, _C_HG:_C_SB]
    sq = (p[:, _C_SB:_C_SB + ATT_W] * scale).astype(BF16)
    sk = p[:, _C_SB + ATT_W:_C_SB + 2 * ATT_W]
    sv = p[:, _C_SB + 2 * ATT_W:_C_DSA]
    cq = (_rope(p[:, _C_DSA:_C_DSA + ATT_W], cos, sin) * scale).astype(BF16)
    ck = _rope(p[:, _C_DSA + ATT_W:_C_DSA + 2 * ATT_W], cos, sin)
    cv = p[:, _C_DSA + 2 * ATT_W:_C_IQ]
    iq = (_rope(p[:, _C_IQ:_C_IK], cos, sin) * scale).astype(BF16)
    ik = _rope(p[:, _C_IK:_C_IW], cos[:, :LANES], sin[:, :LANES])
    if page is None:
        (hg_ref, sq_ref, sk_ref, sv_ref, cq_ref, ck_ref, cv_ref, iq_ref, ik_ref, iw_ref) = out_refs
        sk_ref[...] = sk
        sv_ref[...] = sv
        ck_ref[...] = ck
        cv_ref[...] = cv
        ik_ref[...] = ik[:, :HEAD_DIM]
    else:
        (hg_ref, sq_ref, skp_ref, svp_ref, skt_ref, svb_ref, cq_ref, ckp_ref, cvp_ref, ckt_ref, cvo_ref,
         iq_ref, ikt_ref, ikp_ref, iw_ref) = out_refs
        skt, svt, ckt, cvt, ikt = sk.T, sv.T, ck.T, cv.T, ik.T[:HEAD_DIM]
        for pg in range(p.shape[0] // page):
            sl = slice(pg * page, (pg + 1) * page)
            skp_ref[pg] = skt[:, sl]
            svp_ref[pg] = svt[:, sl]
            ckp_ref[pg] = ckt[:, sl]
            cvp_ref[pg] = cvt[:, sl]
            ikp_ref[pg] = ikt[:, sl]
        skt_ref[0] = skt.astype(BF16)
        ckt_ref[0] = ckt.astype(BF16)
        ikt_ref[0] = ikt.astype(BF16)
        svb_ref[...] = sv.astype(BF16)
        cvb = cv.astype(BF16)
        ones = jnp.ones((cvb.shape[0], HEAD_DIM), BF16)
        pieces = []
        for hd in range(N_HEADS):
            pieces += [cvb[:, hd * HEAD_DIM:(hd + 1) * HEAD_DIM], ones]
        cvo_ref[...] = jnp.concatenate(pieces, axis=1)
    hg_ref[...] = hg
    sq_ref[...] = sq
    cq_ref[...] = cq
    iq_ref[...] = iq
    iw_ref[...] = p[:, _C_IW:]


def _inproj(x, g, w_pad, cos, sin, block_rows, page=None, key_tiles=None):
    rows, d = x.shape
    nblk = rows // block_rows
    row = lambda w: pl.BlockSpec((block_rows, w), lambda i: (i, 0))
    f32o = lambda w: (jax.ShapeDtypeStruct((rows, w), F32), row(w))
    bfo = lambda w: (jax.ShapeDtypeStruct((rows, w), BF16), row(w))
    if page is None:
        outs = [f32o(4 * HG_W), bfo(ATT_W), f32o(ATT_W), f32o(ATT_W), bfo(ATT_W), f32o(ATT_W), f32o(ATT_W),
                bfo(ATT_W), f32o(HEAD_DIM), f32o(LANES)]
    else:
        ppb = block_rows // page
        pages = lambda w: (jax.ShapeDtypeStruct((rows // page, w, page), F32),
                           pl.BlockSpec((ppb, w, page), lambda i: (i, 0, 0)))
        sb_tile, dsa_tile = key_tiles

        def tiles(w, kt):
            bpt = kt // block_rows
            return (jax.ShapeDtypeStruct((rows // kt, w, kt), BF16),
                    pl.BlockSpec((1, w, block_rows), lambda i: (i // bpt, 0, i % bpt)))
        outs = [f32o(4 * HG_W), bfo(ATT_W), pages(ATT_W), pages(ATT_W), tiles(ATT_W, sb_tile), bfo(ATT_W),
                bfo(ATT_W), pages(ATT_W), pages(ATT_W), tiles(ATT_W, dsa_tile), bfo(2 * ATT_W),
                bfo(ATT_W), tiles(HEAD_DIM, dsa_tile), pages(HEAD_DIM), f32o(LANES)]
    return pl.pallas_call(
        functools.partial(_inproj_body, page=page), grid=(nblk,),
        in_specs=[row(d), _const_spec((1, d)), _const_spec(w_pad.shape), row(ATT_W), row(ATT_W)],
        out_specs=tuple(o[1] for o in outs), out_shape=tuple(o[0] for o in outs),
        compiler_params=_params(("arbitrary",)),
    )(x, g, w_pad, cos, sin)


def _hgrn_body(hq_ref, hf_ref, hi_ref, hgate_ref, loglb_ref, log1mlb_ref, omlb_ref, ng_ref, s0_ref,
               o_ref, s_out_ref, st_ref, *, chunk, n_chunks):
    t = pl.program_id(1)

    @pl.when(t == 0)
    def _():
        for h in range(N_HEADS):
            st_ref[h] = s0_ref[0, h].T

    c = chunk
    ri = lax.broadcasted_iota(I32, (c, c), 0)
    ci = lax.broadcasted_iota(I32, (c, c), 1)
    tri = (ri >= ci).astype(F32)
    row = lax.broadcasted_iota(I32, (c, HG_W), 0)
    loglb, log1mlb, omlb, ng = loglb_ref[...], log1mlb_ref[...], omlb_ref[...], ng_ref[...]

    def one_chunk(ic, carry):
        r0 = pl.multiple_of(ic * c, c)
        q = hq_ref[pl.ds(r0, c), :]
        x = hf_ref[pl.ds(r0, c), :]
        v = hi_ref[pl.ds(r0, c), :]
        gate = hgate_ref[pl.ds(r0, c), :]
        qf = q * jax.nn.sigmoid(q)
        log_sig = jnp.minimum(x, 0.0) - jnp.log1p(jnp.exp(-jnp.abs(x)))
        bterm = log1mlb + log_sig
        log_f = jnp.maximum(loglb, bterm) + jnp.log1p(jnp.exp(-jnp.abs(loglb - bterm)))
        kf = omlb * jax.nn.sigmoid(-x)
        b = jnp.dot(tri, log_f, preferred_element_type=F32, precision=lax.Precision.HIGHEST)
        b_last = b[c - 1:c, :]
        qe = qf * jnp.exp(b)
        kk = kf * jnp.exp(b_last - b)
        e_last = jnp.exp(b_last)
        o_intra = [jnp.zeros((c, HG_D), F32) for _ in range(N_HEADS)]
        for s in range(c):
            w = jnp.exp(jnp.where(row >= s, b - b[s:s + 1, :], -jnp.inf)) * kf[s:s + 1, :]
            pw = qf * w
            for h in range(N_HEADS):
                sl = slice(h * HG_D, (h + 1) * HG_D)
                attn = jnp.sum(pw[:, sl], axis=1, keepdims=True)
                o_intra[h] = o_intra[h] + attn * v[s:s + 1, sl]
        outs = []
        for h in range(N_HEADS):
            sl = slice(h * HG_D, (h + 1) * HG_D)
            st = st_ref[h]
            o = o_intra[h] + _dot_nt(qe[:, sl].astype(BF16), st.astype(BF16))
            upd = lax.dot_general(v[:, sl].astype(BF16), kk[:, sl].astype(BF16), (((0,), (0,)), ((), ())),
                                  preferred_element_type=F32)
            st_ref[h] = st * e_last[:, sl] + upd
            o = o * lax.rsqrt(jnp.mean(o * o, axis=1, keepdims=True) + EPS)
            outs.append(o)
        o_all = jnp.concatenate(outs, axis=1) * ng * (gate * jax.nn.sigmoid(gate))
        o_ref[pl.ds(r0, c), :] = o_all
        return carry

    _loop_quads(0, n_chunks, one_chunk, 0)

    @pl.when(t == pl.num_programs(1) - 1)
    def _():
        for h in range(N_HEADS):
            s_out_ref[0, h] = st_ref[h].T


def _hgrn(hg4, lb_terms, ng, s0, batch, seq, block_rows, chunk):
    nt = seq // block_rows
    col = lambda cidx: pl.BlockSpec((block_rows, HG_W), lambda b, t: (b * nt + t, cidx))
    loglb, log1mlb, omlb = lb_terms
    body = functools.partial(_hgrn_body, chunk=chunk, n_chunks=block_rows // chunk)
    return pl.pallas_call(
        body, grid=(batch, nt),
        in_specs=[col(0), col(1), col(2), col(3),
                  _const_spec((1, HG_W)), _const_spec((1, HG_W)), _const_spec((1, HG_W)), _const_spec((1, HG_W)),
                  pl.BlockSpec((1, N_HEADS, HG_D, HG_D), lambda b, t: (b, 0, 0, 0))],
        out_specs=(pl.BlockSpec((block_rows, HG_W), lambda b, t: (b * nt + t, 0)),
                   pl.BlockSpec((1, N_HEADS, HG_D, HG_D), lambda b, t: (b, 0, 0, 0))),
        out_shape=(jax.ShapeDtypeStruct((batch * seq, HG_W), F32),
                   jax.ShapeDtypeStruct((batch, N_HEADS, HG_D, HG_D), F32)),
        scratch_shapes=[pltpu.VMEM((N_HEADS, HG_D, HG_D), F32)],
        compiler_params=_params(("arbitrary", "arbitrary")),
    )(hg4, hg4, hg4, hg4, loglb, log1mlb, omlb, ng, s0)


def _pages_per_step(n_pages):
    return math.gcd(n_pages, 64)


def _group_width(pps, page):
    return 2 * page if pps % 2 == 0 else page


def _chunk(refs):
    return jnp.concatenate([r[...].astype(BF16) for r in refs], axis=1)


def _page_specs(layer, n_pages, pps, page, width, reverse):
    nc = n_pages // pps

    def spec(i):
        def index(b, j, pt):
            chunk = (nc - 1 - j) if reverse else j
            return (layer, pt[b, chunk * pps + i], 0, 0)
        return pl.BlockSpec((None, None, width, page), index)
    return [spec(i) for i in range(pps)]


def _per_batch(shape):
    return pl.BlockSpec((1,) + shape, lambda b, j, pt: (b,) + (0,) * len(shape))


def _suffix_matrix(tk):
    j = lax.broadcasted_iota(I32, (2 * tk, tk), 0) % tk
    s = lax.broadcasted_iota(I32, (2 * tk, tk), 1)
    return (j > s).astype(BF16)


def _sb_exponents(z, visible, suffix):
    gw = suffix.shape[1]
    z2 = z * LOG2E
    log_sig = jnp.minimum(z2, 0.0) - jnp.log2(1.0 + jnp.exp2(-jnp.abs(z2)))
    log_keep = log_sig - z2
    if visible is not None:
        log_keep = jnp.where(visible, log_keep, 0.0)
        log_sig = jnp.where(visible, log_sig, NEG)
    exps, totals = [], []
    for g in range(z.shape[1] // gw):
        lk = log_keep[:, g * gw:(g + 1) * gw]
        hi = lk.astype(BF16)
        lo = (lk - hi.astype(F32)).astype(BF16)
        ex = _dot(jnp.concatenate([hi, lo], axis=1), suffix)
        exps.append(log_sig[:, g * gw:(g + 1) * gw] + ex)
        totals.append(ex[:, 0:1] + lk[:, 0:1])
    return exps, totals


def _sb_weights(z, visible, carry, suffix):
    exps, totals = _sb_exponents(z, visible, suffix)
    parts = [None] * len(exps)
    for g in reversed(range(len(exps))):
        parts[g] = jnp.exp2(exps[g] + carry).astype(BF16)
        carry = carry + totals[g]
    return (parts[0] if len(parts) == 1 else jnp.concatenate(parts, axis=1)), carry


def _sb_prompt_body(q_ref, kt_ref, v_ref, o_ref, acc_ref, *, tq, tk):
    qi = pl.program_id(0)
    qs = _head_stack(q_ref[...])
    suffix = _suffix_matrix(min(tk, 2 * LANES))
    rows = N_HEADS * tq
    row_pos = qi * tq + lax.broadcasted_iota(I32, (rows, tk), 0) % tq
    col = lax.broadcasted_iota(I32, (rows, tk), 1)
    acc_ref[...] = jnp.zeros_like(acc_ref)

    def tile(kb, carry, masked):
        ks = pl.multiple_of(kb * tk, tk)
        visible = (col + kb * tk < row_pos) if masked else None
        a, carry = _sb_weights(_dot(qs, kt_ref[kb]), visible, carry, suffix)
        acc_ref[...] += _dot(a, v_ref[pl.ds(ks, tk), :])
        return carry

    kb_last = (qi * tq + tq - 1) // tk
    n_full = (qi * tq) // tk
    carry = tile(kb_last, jnp.zeros((rows, 1), F32), True)
    carry = lax.fori_loop(0, kb_last - n_full, lambda j, c: tile(kb_last - 1 - j, c, True), carry)
    _loop_pairs(0, n_full, lambda j, c: tile(n_full - 1 - j, c, False), carry)
    o_ref[...] = _head_merge(acc_ref[...], tq)


def _sb_prompt(q, kt, v, tq):
    t = q.shape[0]
    tk = kt.shape[2]
    body = functools.partial(_sb_prompt_body, tq=tq, tk=tk)
    return pl.pallas_call(
        body, grid=(t // tq,),
        in_specs=[pl.BlockSpec((tq, ATT_W), lambda i: (i, 0)), _const_spec(kt.shape), _const_spec((t, ATT_W))],
        out_specs=pl.BlockSpec((tq, ATT_W), lambda i: (i, 0)),
        out_shape=jax.ShapeDtypeStruct((t, ATT_W), F32),
        scratch_shapes=[pltpu.VMEM((N_HEADS * tq, ATT_W), F32)],
        compiler_params=_params(("arbitrary",)),
    )(q, kt, v)


def _sb_sample_body(pt_ref, q_ref, *refs, pps, page):
    kp, vp = refs[:pps], refs[pps:2 * pps]
    kn_ref, vn_ref, o_ref, acc_ref, carry_ref = refs[2 * pps:]
    j = pl.program_id(1)
    tq = q_ref.shape[1]
    rows = N_HEADS * tq
    qs = _head_stack(q_ref[0])

    @pl.when(j == 0)
    def _():
        z = _dot(qs, kn_ref[0].astype(BF16))
        row = lax.broadcasted_iota(I32, (rows, page), 0) % tq
        col = lax.broadcasted_iota(I32, (rows, page), 1)
        a, carry = _sb_weights(z, col < row, jnp.zeros((rows, 1), F32), _suffix_matrix(page))
        acc_ref[...] = _dot_nt(a, vn_ref[0].astype(BF16))
        carry_ref[...] = carry

    z = _dot(qs, _chunk(kp))
    a, carry = _sb_weights(z, None, carry_ref[...], _suffix_matrix(_group_width(pps, page)))
    acc_ref[...] += _dot_nt(a, _chunk(vp))
    carry_ref[...] = carry

    @pl.when(j == pl.num_programs(1) - 1)
    def _():
        o_ref[0] = _head_merge(acc_ref[...], tq)


def _sb_sample(page_table, q, cache_k, cache_v, k_new, v_new, layer):
    nb, tq, _ = q.shape
    n_pages = page_table.shape[1]
    page = cache_k.shape[3]
    pps = _pages_per_step(n_pages)
    body = functools.partial(_sb_sample_body, pps=pps, page=page)
    grid_spec = pltpu.PrefetchScalarGridSpec(
        num_scalar_prefetch=1, grid=(nb, n_pages // pps),
        in_specs=[_per_batch((tq, ATT_W))] + _page_specs(layer, n_pages, pps, page, ATT_W, True)
        + _page_specs(layer, n_pages, pps, page, ATT_W, True) + [_per_batch((ATT_W, page)), _per_batch((ATT_W, page))],
        out_specs=_per_batch((tq, ATT_W)),
        scratch_shapes=[pltpu.VMEM((N_HEADS * tq, ATT_W), F32), pltpu.VMEM((N_HEADS * tq, 1), F32)])
    return pl.pallas_call(
        body, grid_spec=grid_spec, out_shape=jax.ShapeDtypeStruct((nb, tq, ATT_W), F32),
        compiler_params=_params(("arbitrary", "arbitrary")),
    )(page_table, q, *([cache_k] * pps), *([cache_v] * pps), k_new, v_new)


def _score_keys(dots4, iw_b, tq, visible):
    score = jnp.zeros((tq, dots4.shape[1]), F32)
    for h in range(N_HEADS):
        score = score + jnp.maximum(dots4[h * tq:(h + 1) * tq], 0.0) * iw_b[h]
    score = score + 0.0
    if visible is not None:
        score = jnp.where(visible, score, NEG)
    bits = pltpu.bitcast(score, I32)
    return jnp.where(bits < 0, bits ^ 0x7FFFFFFF, bits)


def _kth_largest(count_ge, shape, k, bits):
    def step(i, v):
        cand = v | lax.shift_left(jnp.int32(1), bits - 1 - i)
        return jnp.where(count_ge(cand) >= k, cand, v)
    return lax.fori_loop(0, bits, step, jnp.zeros(shape, I32))


def _prefix_matrix(tk):
    j = lax.broadcasted_iota(I32, (tk, tk), 0)
    s = lax.broadcasted_iota(I32, (tk, tk), 1)
    return (j < s).astype(BF16)


def _additive_mask(key, vstar, need, eq_carry, prefix, visible):
    gw = prefix.shape[0]
    eq = key == vstar
    eqf = jnp.where(eq, 1.0, 0.0)
    before = []
    for g in range(key.shape[1] // gw):
        e = eqf[:, g * gw:(g + 1) * gw]
        before.append(_dot(e.astype(BF16), prefix) + eq_carry)
        eq_carry = eq_carry + jnp.sum(e, axis=1, keepdims=True)
    before = before[0] if len(before) == 1 else jnp.concatenate(before, axis=1)
    am = jnp.where(key > vstar, 0.0, jnp.where(eq, jnp.where(before < need, 0.0, NEG), NEG))
    if visible is not None:
        am = jnp.where(visible, am, NEG)
    return am, eq_carry


def _dsa_prompt_body(iq_ref, iw_ref, cq_ref, ikt_ref, ckt_ref, cvo_ref, o_ref,
                     keys_ref, half_ref, mx_ref, acc_ref, *, tq, tk, topk):
    qi = pl.program_id(0)
    n_tiles = (qi * tq + tq - 1) // tk + 1
    n_full = (qi * tq + 1) // tk
    row_pos = qi * tq + lax.broadcasted_iota(I32, (tq, tk), 0)
    col = lax.broadcasted_iota(I32, (tq, tk), 1)

    iq4 = _head_rows(iq_ref[...])
    iw = iw_ref[...]
    iw_b = [jnp.broadcast_to(iw[:, h:h + 1], (tq, tk)) for h in range(N_HEADS)]

    def score_tile(kb, masked):
        visible = (col + kb * tk <= row_pos) if masked else None
        key = _score_keys(_dot(iq4, ikt_ref[kb]), iw_b, tq, visible)
        keys_ref[kb] = key
        half_ref[kb] = lax.shift_right_arithmetic(key, 16).astype(I16)

    def score_full(kb, c):
        score_tile(kb, False)
        return c

    def score_part(kb, c):
        score_tile(kb, True)
        return c

    _loop_pairs(0, n_full, score_full, 0)
    lax.fori_loop(n_full, n_tiles, score_part, 0)

    cw = min(tk, 2 * LANES)

    def count16(pred):
        def add(kb, cnt):
            x = half_ref[kb]
            for g in range(tk // cw):
                cnt = cnt + jnp.where(pred(x[:, g * cw:(g + 1) * cw]), jnp.int16(1), jnp.int16(0))
            return cnt
        cnt = _loop_pairs(0, n_tiles, add, jnp.zeros((tq, cw), I16))
        return jnp.sum(cnt.astype(I32), axis=1, keepdims=True)

    def half_b(v):
        return jnp.broadcast_to(v, (tq, cw)).astype(I16)

    def kth16(k):
        def count_ge(cand):
            cand_b = half_b(cand + I16_MIN)
            return count16(lambda x: x >= cand_b)
        return _kth_largest(count_ge, (tq, 1), k, 16) + I16_MIN

    vh = kth16(topk)
    vh_b = half_b(vh)
    k_low = topk - count16(lambda x: x > vh_b)
    vh_b32 = jnp.broadcast_to(vh, (tq, tk))

    def fill_low(kb, c):
        key = keys_ref[kb]
        low = (key & 0xFFFF) + I16_MIN
        in_group = lax.shift_right_arithmetic(key, 16) == vh_b32
        half_ref[kb] = jnp.where(in_group, low, I16_MIN).astype(I16)
        return c

    lax.fori_loop(0, n_tiles, fill_low, 0)
    vl = kth16(k_low)
    vl_b = half_b(vl)
    need = (k_low - count16(lambda x: x > vl_b)).astype(F32)
    vstar = lax.shift_left(vh, 16) | (vl - I16_MIN)
    vstar_b = jnp.broadcast_to(vstar, (tq, tk))
    need_b = jnp.broadcast_to(need, (tq, tk))

    cqs = _head_stack(cq_ref[...])
    prefix = _prefix_matrix(min(tk, 2 * LANES))
    mx_ref[...] = jnp.full_like(mx_ref, NEG)

    def select(kb, eq_carry, masked):
        visible = (col + kb * tk <= row_pos) if masked else None
        am, eq_carry = _additive_mask(keys_ref[kb], vstar_b, need_b, eq_carry, prefix, visible)
        keys_ref[kb] = pltpu.bitcast(am, I32)
        logits4 = _dot(cqs, ckt_ref[kb])
        for h in range(N_HEADS):
            lg = logits4[h * tq:(h + 1) * tq] + am
            top = lg[:, :LANES]
            for g in range(1, tk // LANES):
                top = jnp.maximum(top, lg[:, g * LANES:(g + 1) * LANES])
            mx_ref[h] = jnp.maximum(mx_ref[h], top)
        return eq_carry

    eq_carry = _loop_pairs(0, n_full, lambda kb, c: select(kb, c, False), jnp.zeros((tq, 1), F32))
    lax.fori_loop(n_full, n_tiles, lambda kb, c: select(kb, c, True), eq_carry)

    m_b = [jnp.broadcast_to(jnp.max(mx_ref[h], axis=1, keepdims=True), (tq, tk)) for h in range(N_HEADS)]
    acc_ref[...] = jnp.zeros_like(acc_ref)

    def attend(kb, c):
        ks = pl.multiple_of(kb * tk, tk)
        am = pltpu.bitcast(keys_ref[kb], F32)
        logits4 = _dot(cqs, ckt_ref[kb])
        vo = cvo_ref[pl.ds(ks, tk), :]
        for h in range(N_HEADS):
            p = jnp.exp(logits4[h * tq:(h + 1) * tq] + am - m_b[h]).astype(BF16)
            acc_ref[h] += _dot(p, vo[:, h * 2 * HEAD_DIM:(h + 1) * 2 * HEAD_DIM])
        return c

    _loop_pairs(0, n_tiles, attend, 0)
    outs = []
    for h in range(N_HEADS):
        acc = acc_ref[h]
        outs.append(acc[:, :HEAD_DIM] / acc[:, HEAD_DIM:HEAD_DIM + 1])
    o_ref[...] = jnp.concatenate(outs, axis=1)


def _dsa_prompt(iq, iw, cq, ikt, ckt, cvo, tq, topk):
    t = iq.shape[0]
    tk = ckt.shape[2]
    body = functools.partial(_dsa_prompt_body, tq=tq, tk=tk, topk=topk)
    row = lambda w: pl.BlockSpec((tq, w), lambda i: (i, 0))
    return pl.pallas_call(
        body, grid=(t // tq,),
        in_specs=[row(ATT_W), row(LANES), row(ATT_W),
                  _const_spec(ikt.shape), _const_spec(ckt.shape), _const_spec(cvo.shape)],
        out_specs=row(ATT_W),
        out_shape=jax.ShapeDtypeStruct((t, ATT_W), F32),
        scratch_shapes=[pltpu.VMEM((t // tk, tq, tk), I32), pltpu.VMEM((t // tk, tq, tk), I16),
                        pltpu.VMEM((N_HEADS, tq, LANES), F32), pltpu.VMEM((N_HEADS, tq, 2 * HEAD_DIM), F32)],
        compiler_params=_params(("arbitrary",)),
    )(iq, iw, cq, ikt, ckt, cvo)


def _dsa_score_sample_body(pt_ref, iq_ref, iw_ref, *refs, pps, page, topk):
    ikp = refs[:pps]
    ikn_ref, keys_ref, keysn_ref, vstar_ref, need_ref, all_ref, alln_ref = refs[pps:]
    j = pl.program_id(1)
    nc = pl.num_programs(1)
    tq = iq_ref.shape[1]
    cw = pps * page
    iq4 = _head_rows(iq_ref[0])
    iw = iw_ref[0]
    iw_b = lambda n: [jnp.broadcast_to(iw[:, h:h + 1], (tq, n)) for h in range(N_HEADS)]
    key = _score_keys(_dot(iq4, _chunk(ikp)), iw_b(cw), tq, None)
    keys_ref[0, 0] = key
    all_ref[j] = key

    @pl.when(j == nc - 1)
    def _():
        row = lax.broadcasted_iota(I32, (tq, page), 0)
        col = lax.broadcasted_iota(I32, (tq, page), 1)
        key_n = _score_keys(_dot(iq4, ikn_ref[0].astype(BF16)), iw_b(page), tq, col <= row)
        keysn_ref[0] = key_n
        alln_ref[...] = key_n

        def count(pred):
            def add(c, cnt):
                return cnt + jnp.where(pred(all_ref[c]), 1, 0)
            cnt = lax.fori_loop(0, nc, add, jnp.zeros((tq, cw), I32))
            cnt_n = jnp.where(pred(alln_ref[...]), 1, 0)
            return jnp.sum(cnt, axis=1, keepdims=True) + jnp.sum(cnt_n, axis=1, keepdims=True)

        vstar = _kth_largest(lambda cand: count(lambda x: x >= (cand ^ INT_MIN)), (tq, 1), topk, 32) ^ INT_MIN
        need = (topk - count(lambda x: x > vstar)).astype(F32)
        vstar_ref[0] = jnp.broadcast_to(vstar, (tq, page))
        need_ref[0] = jnp.broadcast_to(need, (tq, page))


def _dsa_score_sample(page_table, iq, iw, cache_ik, ik_new, layer, topk):
    nb, tq, _ = iq.shape
    n_pages = page_table.shape[1]
    page = cache_ik.shape[3]
    pps = _pages_per_step(n_pages)
    nc, cw = n_pages // pps, pps * page
    body = functools.partial(_dsa_score_sample_body, pps=pps, page=page, topk=topk)
    grid_spec = pltpu.PrefetchScalarGridSpec(
        num_scalar_prefetch=1, grid=(nb, nc),
        in_specs=[_per_batch((tq, ATT_W)), _per_batch((tq, LANES))]
        + _page_specs(layer, n_pages, pps, page, HEAD_DIM, False) + [_per_batch((HEAD_DIM, page))],
        out_specs=(pl.BlockSpec((1, 1, tq, cw), lambda b, j, pt: (b, j, 0, 0)),
                   _per_batch((tq, page)), _per_batch((tq, page)), _per_batch((tq, page))),
        scratch_shapes=[pltpu.VMEM((nc, tq, cw), I32), pltpu.VMEM((tq, page), I32)])
    return pl.pallas_call(
        body, grid_spec=grid_spec,
        out_shape=(jax.ShapeDtypeStruct((nb, nc, tq, cw), I32), jax.ShapeDtypeStruct((nb, tq, page), I32),
                   jax.ShapeDtypeStruct((nb, tq, page), I32), jax.ShapeDtypeStruct((nb, tq, page), F32)),
        compiler_params=_params(("arbitrary", "arbitrary")),
    )(page_table, iq, iw, *([cache_ik] * pps), ik_new)


def _dsa_attend_sample_body(pt_ref, cq_ref, keys_ref, keysn_ref, vstar_ref, need_ref, *refs, pps, page):
    kp, vp = refs[:pps], refs[pps:2 * pps]
    kn_ref, vn_ref, o_ref, acc_ref, m_ref, l_ref, eq_ref = refs[2 * pps:]
    j = pl.program_id(1)
    tq = cq_ref.shape[1]

    @pl.when(j == 0)
    def _():
        acc_ref[...] = jnp.zeros_like(acc_ref)
        m_ref[...] = jnp.full_like(m_ref, NEG)
        l_ref[...] = jnp.zeros_like(l_ref)
        eq_ref[...] = jnp.zeros_like(eq_ref)

    cqs = _head_stack(cq_ref[0])
    vstar = vstar_ref[0][:, 0:1]
    need = need_ref[0][:, 0:1]

    def step(key, k_tile, v_tile, visible, gw):
        am, eq_carry = _additive_mask(key, vstar, need, eq_ref[...], _prefix_matrix(gw), visible)
        eq_ref[...] = eq_carry
        am4 = jnp.concatenate([am] * N_HEADS, axis=0)
        lg = _dot(cqs, k_tile) + am4
        m_old = m_ref[...]
        m_new = jnp.maximum(m_old, jnp.max(lg, axis=1, keepdims=True))
        p = jnp.where(am4 == 0.0, jnp.exp(lg - m_new), 0.0)
        alpha = jnp.exp(m_old - m_new)
        l_ref[...] = l_ref[...] * alpha + jnp.sum(p, axis=1, keepdims=True)
        acc_ref[...] = acc_ref[...] * alpha + _dot_nt(p.astype(BF16), v_tile)
        m_ref[...] = m_new

    step(keys_ref[0, 0], _chunk(kp), _chunk(vp), None, _group_width(pps, page))

    @pl.when(j == pl.num_programs(1) - 1)
    def _():
        row = lax.broadcasted_iota(I32, (tq, page), 0)
        col = lax.broadcasted_iota(I32, (tq, page), 1)
        step(keysn_ref[0], kn_ref[0].astype(BF16), vn_ref[0].astype(BF16), col <= row, page)
        o_ref[0] = _head_merge(acc_ref[...] / l_ref[...], tq)


def _dsa_attend_sample(page_table, cq, keys, keys_new, vstar, need, cache_k, cache_v, k_new, v_new, layer):
    nb, tq, _ = cq.shape
    n_pages = page_table.shape[1]
    page = cache_k.shape[3]
    pps = _pages_per_step(n_pages)
    nc, cw = n_pages // pps, pps * page
    body = functools.partial(_dsa_attend_sample_body, pps=pps, page=page)
    grid_spec = pltpu.PrefetchScalarGridSpec(
        num_scalar_prefetch=1, grid=(nb, nc),
        in_specs=[_per_batch((tq, ATT_W)),
                  pl.BlockSpec((1, 1, tq, cw), lambda b, j, pt: (b, j, 0, 0)),
                  _per_batch((tq, page)), _per_batch((tq, page)), _per_batch((tq, page))]
        + _page_specs(layer, n_pages, pps, page, ATT_W, False) + _page_specs(layer, n_pages, pps, page, ATT_W, False)
        + [_per_batch((ATT_W, page)), _per_batch((ATT_W, page))],
        out_specs=_per_batch((tq, ATT_W)),
        scratch_shapes=[pltpu.VMEM((N_HEADS * tq, ATT_W), F32),
                        pltpu.VMEM((N_HEADS * tq, 1), F32), pltpu.VMEM((N_HEADS * tq, 1), F32),
                        pltpu.VMEM((tq, 1), F32)])
    return pl.pallas_call(
        body, grid_spec=grid_spec, out_shape=jax.ShapeDtypeStruct((nb, tq, ATT_W), F32),
        compiler_params=_params(("arbitrary", "arbitrary")),
    )(page_table, cq, keys, keys_new, vstar, need, *([cache_k] * pps), *([cache_v] * pps), k_new, v_new)


def _ffn_body(*refs, seq_rows, carried, final):
    it = iter(refs)
    x_ref, oa_ref, ob_ref, oc_ref, wout_ref, g_ref, wup_ref, cw_ref, cb_ref, wdown_ref = (next(it) for _ in range(10))
    e0_ref = e1_ref = gf_ref = carry_ref = None
    if not carried:
        e0_ref, e1_ref = next(it), next(it)
    if final:
        gf_ref = next(it)
    xo_ref, tail_ref = next(it), next(it)
    if carried:
        carry_ref = next(it)

        @pl.when(pl.program_id(0) == 0)
        def _():
            carry_ref[...] = jnp.zeros_like(carry_ref)

    mixed = jnp.concatenate([oa_ref[...], ob_ref[...], oc_ref[...]], axis=1).astype(BF16)
    x1 = x_ref[...] + _dot(mixed, wout_ref[...])
    h = _rmsnorm(x1, g_ref[...]).astype(BF16)
    up = _dot(h, wup_ref[...])
    d_ff = up.shape[1] // 2
    a, gate = up[:, :d_ff], up[:, d_ff:]
    rows = a.shape[0]
    r = lax.broadcasted_iota(I32, a.shape, 0) % seq_rows
    if carried:
        nc = carry_ref.shape[0]
        e0 = jnp.broadcast_to(carry_ref[nc - 2:nc - 1, :], a.shape)
        e1 = jnp.broadcast_to(carry_ref[nc - 1:nc, :], a.shape)
    else:
        e0, e1 = e0_ref[...], e1_ref[...]
    a_m1 = jnp.where(r == 0, e1, pltpu.roll(a, 1, 0))
    a_m2 = jnp.where(r == 0, e0, jnp.where(r == 1, e1, pltpu.roll(a, 2, 0)))
    cw = cw_ref[...]
    c = cb_ref[...] + a_m2 * cw[0:1, :] + a_m1 * cw[1:2, :] + a * cw[2:3, :]
    act = (c * jax.nn.sigmoid(c) * gate).astype(BF16)
    x2 = x1 + _dot(act, wdown_ref[...])
    xo_ref[...] = _rmsnorm(x2, gf_ref[...]) if final else x2
    tail = tail_ref.shape[0]
    tail_ref[...] = a[rows - tail:, :]
    if carried:
        nc = carry_ref.shape[0]
        carry_ref[...] = a[rows - nc:, :]


def _ffn(x, oa, ob, oc, w_out, g, w_up, conv_w, conv_b, w_down, block_rows, prev=None, final_g=None):
    rows, d = x.shape
    d_ff = w_down.shape[0]
    carried = prev is None
    final = final_g is not None
    row = lambda w: pl.BlockSpec((block_rows, w), lambda i: (i, 0))
    ins = [x, oa, ob, oc, w_out, g, w_up, conv_w, conv_b, w_down]
    in_specs = [row(d), row(HG_W), row(ATT_W), row(ATT_W), _const_spec(w_out.shape), _const_spec((1, d)),
                _const_spec(w_up.shape), _const_spec(conv_w.shape), _const_spec((1, d_ff)), _const_spec(w_down.shape)]
    if not carried:
        ins += list(prev)
        in_specs += [row(d_ff), row(d_ff)]
    if final:
        ins.append(final_g)
        in_specs.append(_const_spec((1, d)))
    if carried:
        tail_rows, tail_spec = 8, pl.BlockSpec((8, d_ff), lambda i: (0, 0))
        scratch = [pltpu.VMEM((8, d_ff), F32)]
        seq_rows = block_rows
    else:
        tail_rows, tail_spec = rows, row(d_ff)
        scratch = []
        seq_rows = 8
    body = functools.partial(_ffn_body, seq_rows=seq_rows, carried=carried, final=final)
    return pl.pallas_call(
        body, grid=(rows // block_rows,),
        in_specs=in_specs, out_specs=(row(d), tail_spec),
        out_shape=(jax.ShapeDtypeStruct((rows, d), F32), jax.ShapeDtypeStruct((tail_rows, d_ff), F32)),
        scratch_shapes=scratch,
        compiler_params=_params(("arbitrary",)),
    )(*ins)


def _rope_tables(pos):
    half = HEAD_DIM // 2
    inv = jnp.exp(-math.log(ROPE_THETA) * jnp.arange(half, dtype=F32) / half)
    ang = pos.astype(F32)[:, None] * inv[None, :]
    cos, sin = jnp.cos(ang), jnp.sin(ang)
    cos = jnp.tile(jnp.concatenate([cos, cos], axis=1), (1, N_HEADS))
    sin = jnp.tile(jnp.concatenate([-sin, sin], axis=1), (1, N_HEADS))
    return cos, sin


def _w_in_body(w_ref, o_ref, *, n_iw):
    i = pl.program_id(0)
    last = pl.num_programs(0) - 1
    x = w_ref[...]
    r = lax.broadcasted_iota(I32, (x.shape[0], x.shape[2]), 0)
    for l in range(x.shape[1]):
        xl = x[:, l, :]
        ik = jnp.where(r < HEAD_DIM, xl, 0.0)
        iw = jnp.where(r < n_iw, pltpu.roll(xl, LANES - HEAD_DIM, 0), 0.0)
        y = jnp.where(i == last, iw, jnp.where(i == last - 1, ik, xl))
        o_ref[l] = y.T.astype(BF16)


def _pad_w_in(w_in):
    depth, d, n = w_in.shape
    n_in = pl.cdiv(n, LANES)
    assert n_in * LANES == _C_IW and n > _C_IK + HEAD_DIM
    return pl.pallas_call(
        functools.partial(_w_in_body, n_iw=n - _C_IK - HEAD_DIM), grid=(IN_PAD // LANES,),
        in_specs=[pl.BlockSpec((LANES, depth, d), lambda i: (jnp.minimum(i, n_in - 1), 0, 0))],
        out_specs=pl.BlockSpec((depth, d, LANES), lambda i: (0, 0, i)),
        out_shape=jax.ShapeDtypeStruct((depth, d, IN_PAD), BF16),
        compiler_params=_params(("arbitrary",)),
    )(jnp.transpose(w_in, (2, 0, 1)))


def _new_page(a, page):
    return jnp.swapaxes(jnp.pad(a, ((0, 0), (0, page - a.shape[1]), (0, 0))), 1, 2)


def _feature_major(cache):
    depth, pool, page = cache.shape[:3]
    perm = (0, 1, 3, 4, 2) if cache.ndim == 5 else (0, 1, 3, 2)
    return jnp.transpose(cache, perm).reshape(depth, pool, -1, page)


def _slot_major(pages, lead):
    depth, n_pg, w, page = pages.shape
    if w == HEAD_DIM:
        return jnp.transpose(pages, (0, 1, 3, 2)).reshape((depth,) + lead + (n_pg, page, w))
    a = pages.reshape(depth, n_pg, N_HEADS, HEAD_DIM, page)
    return jnp.transpose(a, (0, 1, 4, 2, 3)).reshape((depth,) + lead + (n_pg, page, N_HEADS, HEAD_DIM))


def kernel(x_prompt, x_sample, cache_sb_k, cache_sb_v, cache_dsa_k, cache_dsa_v, cache_idx_k, state_hgrn, state_conv, page_table, hg_lower_bounds, attn_norm, w_in, hg_norm, w_out, ffn_norm, w_up, conv_w, conv_b, w_down, final_norm):
    depth = w_in.shape[0]
    nb_p, seq, d = x_prompt.shape
    nb_s, seq_s, _ = x_sample.shape
    assert nb_p == 1, "the prompt group is one sequence"
    page = cache_sb_k.shape[2]
    n_pages = page_table.shape[1]
    past = n_pages * page
    d_ff = w_down.shape[1]
    rows_s = nb_s * seq_s

    lb = jnp.cumsum(jax.nn.softmax(hg_lower_bounds.astype(F32), axis=0), axis=0)
    lb = (lb - lb[0]).reshape(depth, 1, HG_W)
    cos_p, sin_p = _rope_tables(jnp.arange(seq, dtype=I32))
    cos_s, sin_s = _rope_tables(jnp.tile(past + jnp.arange(seq_s, dtype=I32), nb_s))

    cache_sb_k, cache_sb_v = _feature_major(cache_sb_k), _feature_major(cache_sb_v)
    cache_dsa_k, cache_dsa_v = _feature_major(cache_dsa_k), _feature_major(cache_dsa_v)
    cache_idx_k = _feature_major(cache_idx_k)

    tq = 128
    block_rows = 256
    key_tiles = (min(512, seq), min(1024, seq))
    assert all(seq % kt == 0 and kt % block_rows == 0 for kt in key_tiles)
    topk_p = min(TOPK_MAX, seq // 4)
    topk_s = min(TOPK_MAX, (past + seq_s) // 4)

    w_in_pad = _pad_w_in(w_in)
    xp = x_prompt.reshape(seq, d)
    xs = x_sample.reshape(rows_s, d)
    p_rows, s_rows = [], []
    for l in range(depth):
        w_in_l = w_in_pad[l]
        g_attn = attn_norm[l].reshape(1, d)
        lb_terms = (jnp.log(lb[l]), jnp.log1p(-lb[l]), 1.0 - lb[l])
        ng = jnp.tile(hg_norm[l], N_HEADS).reshape(1, HG_W)
        w_out_l, w_up_l, w_down_l = w_out[l].astype(BF16), w_up[l].astype(BF16), w_down[l].astype(BF16)
        g_ffn = ffn_norm[l].reshape(1, d)
        cb = conv_b[l].reshape(1, d_ff)
        last = l == depth - 1
        final_g = final_norm.reshape(1, d) if last else None

        (hg4, sq, skp, svp, skt, svb, cq, ckp, cvp, ckt, cvo, iq, ikt, ikp, iw) = _inproj(
            xp, g_attn, w_in_l, cos_p, sin_p, block_rows, page=page, key_tiles=key_tiles)
        o_a, hg_new = _hgrn(hg4, lb_terms, ng, jnp.zeros((1, N_HEADS, HG_D, HG_D), F32), 1, seq, block_rows, 16)
        o_b = _sb_prompt(sq, skt, svb, tq)
        o_c = _dsa_prompt(iq, iw, cq, ikt, ckt, cvo, tq, topk_p)
        xp, tail = _ffn(xp, o_a, o_b, o_c, w_out_l, g_ffn, w_up_l, conv_w[l], cb, w_down_l, block_rows, final_g=final_g)
        p_rows.append((skp, svp, ckp, cvp, ikp, hg_new, tail[8 - (conv_w.shape[1] - 1):]))

        (hg4, sq, sk, sv, cq, ck, cv, iq, ik, iw) = _inproj(xs, g_attn, w_in_l, cos_s, sin_s, rows_s)
        o_a, hg_new = _hgrn(hg4, lb_terms, ng, state_hgrn[l], nb_s, seq_s, seq_s, seq_s)
        b3 = lambda a: a.reshape(nb_s, seq_s, a.shape[-1])
        new = lambda a: _new_page(b3(a), page)
        o_b = _sb_sample(page_table, b3(sq), cache_sb_k, cache_sb_v, new(sk), new(sv), l)
        keys, keys_new, vstar, need = _dsa_score_sample(page_table, b3(iq), b3(iw), cache_idx_k, new(ik), l, topk_s)
        o_c = _dsa_attend_sample(page_table, b3(cq), keys, keys_new, vstar, need, cache_dsa_k, cache_dsa_v,
                                 new(ck), new(cv), l)
        prev = (jnp.repeat(state_conv[l][:, 0], seq_s, axis=0), jnp.repeat(state_conv[l][:, 1], seq_s, axis=0))
        xs, a_rows = _ffn(xs, o_a, o_b.reshape(rows_s, ATT_W), o_c.reshape(rows_s, ATT_W), w_out_l, g_ffn, w_up_l,
                          conv_w[l], cb, w_down_l, rows_s, prev=prev, final_g=final_g)
        conv_new = a_rows.reshape(nb_s, seq_s, d_ff)[:, seq_s - (conv_w.shape[1] - 1):]
        s_rows.append((sk, sv, ck, cv, ik, hg_new, conv_new))

    stack = lambda rows, j: jnp.stack([r[j] for r in rows])
    heads = lambda a: _slot_major(a, (1,))
    heads_s = lambda a: a.reshape(depth, nb_s, seq_s, N_HEADS, HEAD_DIM)
    return (xp.reshape(1, seq, d), xs.reshape(nb_s, seq_s, d),
            heads(stack(p_rows, 0)), heads(stack(p_rows, 1)), heads(stack(p_rows, 2)), heads(stack(p_rows, 3)),
            heads(stack(p_rows, 4)),
            stack(p_rows, 5), stack(p_rows, 6).reshape(depth, 1, conv_w.shape[1] - 1, d_ff),
            heads_s(stack(s_rows, 0)), heads_s(stack(s_rows, 1)), heads_s(stack(s_rows, 2)), heads_s(stack(s_rows, 3)),
            stack(s_rows, 4).reshape(depth, nb_s, seq_s, HEAD_DIM),
            stack(s_rows, 5), stack(s_rows, 6))
```

```python
import functools
import math

import jax
import jax.numpy as jnp
from jax import lax
from jax.experimental import pallas as pl
from jax.experimental.pallas import tpu as pltpu

F32 = jnp.float32
BF16 = jnp.bfloat16
I32 = jnp.int32
I16 = jnp.int16

HEAD_DIM = 64
N_HEADS = 4
HG_D = 128
ATT_W = N_HEADS * HEAD_DIM
HG_W = N_HEADS * HG_D
TOPK_MAX = 256
ROPE_THETA = 10000.0
EPS = 1e-6
NEG = -1e30
LOG2E = 1.4426950408889634
INT_MIN = -2147483648
I16_MIN = -32768
VMEM_LIMIT_BYTES = 56 * 1024 * 1024
LANES = 128

_C_HG = 0
_C_SB = 4 * HG_W
_C_DSA = _C_SB + 3 * ATT_W
_C_IQ = _C_DSA + 3 * ATT_W
_C_IK = _C_IQ + ATT_W
_C_IW = _C_IK + LANES
IN_PAD = _C_IW + LANES


def _const_spec(shape):
    return pl.BlockSpec(shape, lambda *_: (0,) * len(shape), pipeline_mode=pl.Buffered(1))


def _params(sem):
    return pltpu.CompilerParams(dimension_semantics=sem, vmem_limit_bytes=VMEM_LIMIT_BYTES)


def _dot(a, b):
    return jnp.dot(a, b, preferred_element_type=F32)


def _dot_nt(a, b):
    return lax.dot_general(a, b, (((1,), (1,)), ((), ())), preferred_element_type=F32)


def _loop_pairs(lo, hi, fn, carry):
    n = hi - lo

    def two(i, c):
        k = lo + 2 * i
        return fn(k + 1, fn(k, c))

    carry = lax.fori_loop(0, n // 2, two, carry)
    return lax.cond(n % 2 == 1, lambda c: fn(hi - 1, c), lambda c: c, carry)


def _loop_quads(lo, hi, fn, carry):
    n = hi - lo

    def four(i, c):
        k = lo + 4 * i
        return fn(k + 3, fn(k + 2, fn(k + 1, fn(k, c))))

    carry = lax.fori_loop(0, n // 4, four, carry)
    return _loop_pairs(lo + 4 * (n // 4), hi, fn, carry)


def _rmsnorm(x, g):
    return x * lax.rsqrt(jnp.mean(x * x, axis=-1, keepdims=True) + EPS) * g


def _head_rows(q):
    return jnp.concatenate([q[:, h * HEAD_DIM:(h + 1) * HEAD_DIM] for h in range(N_HEADS)], axis=0)


def _head_stack(q):
    lane = lax.broadcasted_iota(I32, q.shape, 1)
    zero = jnp.zeros_like(q)
    return jnp.concatenate([jnp.where(lane // HEAD_DIM == h, q, zero) for h in range(N_HEADS)], axis=0)


def _head_merge(acc, t):
    lane = lax.broadcasted_iota(I32, (t, ATT_W), 1)
    out = jnp.zeros((t, ATT_W), F32)
    for h in range(N_HEADS):
        out = jnp.where(lane // HEAD_DIM == h, acc[h * t:(h + 1) * t], out)
    return out


def _rope(x, cos, sin):
    lane = lax.broadcasted_iota(I32, x.shape, 1)
    first = (lane % HEAD_DIM) < (HEAD_DIM // 2)
    w = x.shape[1]
    swapped = jnp.where(first, pltpu.roll(x, w - HEAD_DIM // 2, 1), pltpu.roll(x, HEAD_DIM // 2, 1))
    return x * cos + swapped * sin


def _inproj_body(x_ref, g_ref, w_ref, cos_ref, sin_ref, *out_refs, page):
    h = _rmsnorm(x_ref[...], g_ref[...]).astype(BF16)
    p = _dot(h, w_ref[...])
    cos, sin = cos_ref[...], sin_ref[...]
    scale = HEAD_DIM ** -0.5
    hg = p[:, _C_HG:_C_SB]
    sq = (p[:, _C_SB:_C_SB + ATT_W] * scale).astype(BF16)
    sk = p[:, _C_SB + ATT_W:_C_SB + 2 * ATT_W]
    sv = p[:, _C_SB + 2 * ATT_W:_C_DSA]
    cq = (_rope(p[:, _C_DSA:_C_DSA + ATT_W], cos, sin) * scale).astype(BF16)
    ck = _rope(p[:, _C_DSA + ATT_W:_C_DSA + 2 * ATT_W], cos, sin)
    cv = p[:, _C_DSA + 2 * ATT_W:_C_IQ]
    iq = (_rope(p[:, _C_IQ:_C_IK], cos, sin) * scale).astype(BF16)
    ik = _rope(p[:, _C_IK:_C_IW], cos[:, :LANES], sin[:, :LANES])
    if page is None:
        (hg_ref, sq_ref, sk_ref, sv_ref, cq_ref, ck_ref, cv_ref, iq_ref, ik_ref, iw_ref) = out_refs
        sk_ref[...] = sk
        sv_ref[...] = sv
        ck_ref[...] = ck
        cv_ref[...] = cv
        ik_ref[...] = ik[:, :HEAD_DIM]
    else:
        (hg_ref, sq_ref, skp_ref, svp_ref, skt_ref, svb_ref, cq_ref, ckp_ref, cvp_ref, ckt_ref, cvo_ref,
         iq_ref, ikt_ref, ikp_ref, iw_ref) = out_refs
        skt, svt, ckt, cvt, ikt = sk.T, sv.T, ck.T, cv.T, ik.T[:HEAD_DIM]
        for pg in range(p.shape[0] // page):
            sl = slice(pg * page, (pg + 1) * page)
            skp_ref[pg] = skt[:, sl]
            svp_ref[pg] = svt[:, sl]
            ckp_ref[pg] = ckt[:, sl]
            cvp_ref[pg] = cvt[:, sl]
            ikp_ref[pg] = ikt[:, sl]
        skt_ref[0] = skt.astype(BF16)
        ckt_ref[0] = ckt.astype(BF16)
        ikt_ref[0] = ikt.astype(BF16)
        svb_ref[...] = sv.astype(BF16)
        cvb = cv.astype(BF16)
        ones = jnp.ones((cvb.shape[0], HEAD_DIM), BF16)
        pieces = []
        for hd in range(N_HEADS):
            pieces += [cvb[:, hd * HEAD_DIM:(hd + 1) * HEAD_DIM], ones]
        cvo_ref[...] = jnp.concatenate(pieces, axis=1)
    hg_ref[...] = hg
    sq_ref[...] = sq
    cq_ref[...] = cq
    iq_ref[...] = iq
    iw_ref[...] = p[:, _C_IW:]


def _inproj(x, g, w_pad, cos, sin, block_rows, page=None, key_tiles=None):
    rows, d = x.shape
    nblk = rows // block_rows
    row = lambda w: pl.BlockSpec((block_rows, w), lambda i: (i, 0))
    f32o = lambda w: (jax.ShapeDtypeStruct((rows, w), F32), row(w))
    bfo = lambda w: (jax.ShapeDtypeStruct((rows, w), BF16), row(w))
    if page is None:
        outs = [f32o(4 * HG_W), bfo(ATT_W), f32o(ATT_W), f32o(ATT_W), bfo(ATT_W), f32o(ATT_W), f32o(ATT_W),
                bfo(ATT_W), f32o(HEAD_DIM), f32o(LANES)]
    else:
        ppb = block_rows // page
        pages = lambda w: (jax.ShapeDtypeStruct((rows // page, w, page), F32),
                           pl.BlockSpec((ppb, w, page), lambda i: (i, 0, 0)))
        sb_tile, dsa_tile = key_tiles

        def tiles(w, kt):
            bpt = kt // block_rows
            return (jax.ShapeDtypeStruct((rows // kt, w, kt), BF16),
                    pl.BlockSpec((1, w, block_rows), lambda i: (i // bpt, 0, i % bpt)))
        outs = [f32o(4 * HG_W), bfo(ATT_W), pages(ATT_W), pages(ATT_W), tiles(ATT_W, sb_tile), bfo(ATT_W),
                bfo(ATT_W), pages(ATT_W), pages(ATT_W), tiles(ATT_W, dsa_tile), bfo(2 * ATT_W),
                bfo(ATT_W), tiles(HEAD_DIM, dsa_tile), pages(HEAD_DIM), f32o(LANES)]
    return pl.pallas_call(
        functools.partial(_inproj_body, page=page), grid=(nblk,),
        in_specs=[row(d), _const_spec((1, d)), _const_spec(w_pad.shape), row(ATT_W), row(ATT_W)],
        out_specs=tuple(o[1] for o in outs), out_shape=tuple(o[0] for o in outs),
        compiler_params=_params(("arbitrary",)),
    )(x, g, w_pad, cos, sin)


def _hgrn_body(hq_ref, hf_ref, hi_ref, hgate_ref, loglb_ref, log1mlb_ref, omlb_ref, ng_ref, s0_ref,
               o_ref, s_out_ref, st_ref, *, chunk, n_chunks):
    t = pl.program_id(1)

    @pl.when(t == 0)
    def _():
        for h in range(N_HEADS):
            st_ref[h] = s0_ref[0, h].T

    c = chunk
    ri = lax.broadcasted_iota(I32, (c, c), 0)
    ci = lax.broadcasted_iota(I32, (c, c), 1)
    tri = (ri >= ci).astype(F32)
    row = lax.broadcasted_iota(I32, (c, HG_W), 0)
    loglb, log1mlb, omlb, ng = loglb_ref[...], log1mlb_ref[...], omlb_ref[...], ng_ref[...]

    def one_chunk(ic, carry):
        r0 = pl.multiple_of(ic * c, c)
        q = hq_ref[pl.ds(r0, c), :]
        x = hf_ref[pl.ds(r0, c), :]
        v = hi_ref[pl.ds(r0, c), :]
        gate = hgate_ref[pl.ds(r0, c), :]
        qf = q * jax.nn.sigmoid(q)
        log_sig = jnp.minimum(x, 0.0) - jnp.log1p(jnp.exp(-jnp.abs(x)))
        bterm = log1mlb + log_sig
        log_f = jnp.maximum(loglb, bterm) + jnp.log1p(jnp.exp(-jnp.abs(loglb - bterm)))
        kf = omlb * jax.nn.sigmoid(-x)
        b = jnp.dot(tri, log_f, preferred_element_type=F32, precision=lax.Precision.HIGHEST)
        b_last = b[c - 1:c, :]
        qe = qf * jnp.exp(b)
        kk = kf * jnp.exp(b_last - b)
        e_last = jnp.exp(b_last)
        o_intra = [jnp.zeros((c, HG_D), F32) for _ in range(N_HEADS)]
        for s in range(c):
            w = jnp.exp(jnp.where(row >= s, b - b[s:s + 1, :], -jnp.inf)) * kf[s:s + 1, :]
            pw = qf * w
            for h in range(N_HEADS):
                sl = slice(h * HG_D, (h + 1) * HG_D)
                attn = jnp.sum(pw[:, sl], axis=1, keepdims=True)
                o_intra[h] = o_intra[h] + attn * v[s:s + 1, sl]
        outs = []
        for h in range(N_HEADS):
            sl = slice(h * HG_D, (h + 1) * HG_D)
            st = st_ref[h]
            o = o_intra[h] + _dot_nt(qe[:, sl].astype(BF16), st.astype(BF16))
            upd = lax.dot_general(v[:, sl].astype(BF16), kk[:, sl].astype(BF16), (((0,), (0,)), ((), ())),
                                  preferred_element_type=F32)
            st_ref[h] = st * e_last[:, sl] + upd
            o = o * lax.rsqrt(jnp.mean(o * o, axis=1, keepdims=True) + EPS)
            outs.append(o)
        o_all = jnp.concatenate(outs, axis=1) * ng * (gate * jax.nn.sigmoid(gate))
        o_ref[pl.ds(r0, c), :] = o_all
        return carry

    _loop_quads(0, n_chunks, one_chunk, 0)

    @pl.when(t == pl.num_programs(1) - 1)
    def _():
        for h in range(N_HEADS):
            s_out_ref[0, h] = st_ref[h].T


def _hgrn(hg4, lb_terms, ng, s0, batch, seq, block_rows, chunk):
    nt = seq // block_rows
    col = lambda cidx: pl.BlockSpec((block_rows, HG_W), lambda b, t: (b * nt + t, cidx))
    loglb, log1mlb, omlb = lb_terms
    body = functools.partial(_hgrn_body, chunk=chunk, n_chunks=block_rows // chunk)
    return pl.pallas_call(
        body, grid=(batch, nt),
        in_specs=[col(0), col(1), col(2), col(3),
                  _const_spec((1, HG_W)), _const_spec((1, HG_W)), _const_spec((1, HG_W)), _const_spec((1, HG_W)),
                  pl.BlockSpec((1, N_HEADS, HG_D, HG_D), lambda b, t: (b, 0, 0, 0))],
        out_specs=(pl.BlockSpec((block_rows, HG_W), lambda b, t: (b * nt + t, 0)),
                   pl.BlockSpec((1, N_HEADS, HG_D, HG_D), lambda b, t: (b, 0, 0, 0))),
        out_shape=(jax.ShapeDtypeStruct((batch * seq, HG_W), F32),
                   jax.ShapeDtypeStruct((batch, N_HEADS, HG_D, HG_D), F32)),
        scratch_shapes=[pltpu.VMEM((N_HEADS, HG_D, HG_D), F32)],
        compiler_params=_params(("arbitrary", "arbitrary")),
    )(hg4, hg4, hg4, hg4, loglb, log1mlb, omlb, ng, s0)


def _pages_per_step(n_pages):
    return math.gcd(n_pages, 64)


def _group_width(pps, page):
    return 2 * page if pps % 2 == 0 else page


def _chunk(refs):
    return jnp.concatenate([r[...].astype(BF16) for r in refs], axis=1)


def _page_specs(layer, n_pages, pps, page, width, reverse):
    nc = n_pages // pps

    def spec(i):
        def index(b, j, pt):
            chunk = (nc - 1 - j) if reverse else j
            return (layer, pt[b, chunk * pps + i], 0, 0)
        return pl.BlockSpec((None, None, width, page), index)
    return [spec(i) for i in range(pps)]


def _per_batch(shape):
    return pl.BlockSpec((1,) + shape, lambda b, j, pt: (b,) + (0,) * len(shape))


def _suffix_matrix(tk):
    j = lax.broadcasted_iota(I32, (2 * tk, tk), 0) % tk
    s = lax.broadcasted_iota(I32, (2 * tk, tk), 1)
    return (j > s).astype(BF16)


def _sb_exponents(z, visible, suffix):
    gw = suffix.shape[1]
    z2 = z * LOG2E
    log_sig = jnp.minimum(z2, 0.0) - jnp.log2(1.0 + jnp.exp2(-jnp.abs(z2)))
    log_keep = log_sig - z2
    if visible is not None:
        log_keep = jnp.where(visible, log_keep, 0.0)
        log_sig = jnp.where(visible, log_sig, NEG)
    exps, totals = [], []
    for g in range(z.shape[1] // gw):
        lk = log_keep[:, g * gw:(g + 1) * gw]
        hi = lk.astype(BF16)
        lo = (lk - hi.astype(F32)).astype(BF16)
        ex = _dot(jnp.concatenate([hi, lo], axis=1), suffix)
        exps.append(log_sig[:, g * gw:(g + 1) * gw] + ex)
        totals.append(ex[:, 0:1] + lk[:, 0:1])
    return exps, totals


def _sb_weights(z, visible, carry, suffix):
    exps, totals = _sb_exponents(z, visible, suffix)
    parts = [None] * len(exps)
    for g in reversed(range(len(exps))):
        parts[g] = jnp.exp2(exps[g] + carry).astype(BF16)
        carry = carry + totals[g]
    return (parts[0] if len(parts) == 1 else jnp.concatenate(parts, axis=1)), carry


def _sb_prompt_body(q_ref, kt_ref, v_ref, o_ref, acc_ref, *, tq, tk):
    qi = pl.program_id(0)
    qs = _head_stack(q_ref[...])
    suffix = _suffix_matrix(min(tk, 2 * LANES))
    rows = N_HEADS * tq
    row_pos = qi * tq + lax.broadcasted_iota(I32, (rows, tk), 0) % tq
    col = lax.broadcasted_iota(I32, (rows, tk), 1)
    acc_ref[...] = jnp.zeros_like(acc_ref)

    def tile(kb, carry, masked):
        ks = pl.multiple_of(kb * tk, tk)
        visible = (col + kb * tk < row_pos) if masked else None
        a, carry = _sb_weights(_dot(qs, kt_ref[kb]), visible, carry, suffix)
        acc_ref[...] += _dot(a, v_ref[pl.ds(ks, tk), :])
        return carry

    kb_last = (qi * tq + tq - 1) // tk
    n_full = (qi * tq) // tk
    carry = tile(kb_last, jnp.zeros((rows, 1), F32), True)
    carry = lax.fori_loop(0, kb_last - n_full, lambda j, c: tile(kb_last - 1 - j, c, True), carry)
    _loop_quads(0, n_full, lambda j, c: tile(n_full - 1 - j, c, False), carry)
    o_ref[...] = _head_merge(acc_ref[...], tq)


def _sb_prompt(q, kt, v, tq):
    t = q.shape[0]
    tk = kt.shape[2]
    body = functools.partial(_sb_prompt_body, tq=tq, tk=tk)
    return pl.pallas_call(
        body, grid=(t // tq,),
        in_specs=[pl.BlockSpec((tq, ATT_W), lambda i: (i, 0)), _const_spec(kt.shape), _const_spec((t, ATT_W))],
        out_specs=pl.BlockSpec((tq, ATT_W), lambda i: (i, 0)),
        out_shape=jax.ShapeDtypeStruct((t, ATT_W), F32),
        scratch_shapes=[pltpu.VMEM((N_HEADS * tq, ATT_W), F32)],
        compiler_params=_params(("arbitrary",)),
    )(q, kt, v)


def _sb_sample_body(pt_ref, q_ref, *refs, pps, page):
    kp, vp = refs[:pps], refs[pps:2 * pps]
    kn_ref, vn_ref, o_ref, acc_ref, carry_ref = refs[2 * pps:]
    j = pl.program_id(1)
    tq = q_ref.shape[1]
    rows = N_HEADS * tq
    qs = _head_stack(q_ref[0])

    @pl.when(j == 0)
    def _():
        z = _dot(qs, kn_ref[0].astype(BF16))
        row = lax.broadcasted_iota(I32, (rows, page), 0) % tq
        col = lax.broadcasted_iota(I32, (rows, page), 1)
        a, carry = _sb_weights(z, col < row, jnp.zeros((rows, 1), F32), _suffix_matrix(page))
        acc_ref[...] = _dot_nt(a, vn_ref[0].astype(BF16))
        carry_ref[...] = carry

    z = _dot(qs, _chunk(kp))
    a, carry = _sb_weights(z, None, carry_ref[...], _suffix_matrix(_group_width(pps, page)))
    acc_ref[...] += _dot_nt(a, _chunk(vp))
    carry_ref[...] = carry

    @pl.when(j == pl.num_programs(1) - 1)
    def _():
        o_ref[0] = _head_merge(acc_ref[...], tq)


def _sb_sample(page_table, q, cache_k, cache_v, k_new, v_new, layer):
    nb, tq, _ = q.shape
    n_pages = page_table.shape[1]
    page = cache_k.shape[3]
    pps = _pages_per_step(n_pages)
    body = functools.partial(_sb_sample_body, pps=pps, page=page)
    grid_spec = pltpu.PrefetchScalarGridSpec(
        num_scalar_prefetch=1, grid=(nb, n_pages // pps),
        in_specs=[_per_batch((tq, ATT_W))] + _page_specs(layer, n_pages, pps, page, ATT_W, True)
        + _page_specs(layer, n_pages, pps, page, ATT_W, True) + [_per_batch((ATT_W, page)), _per_batch((ATT_W, page))],
        out_specs=_per_batch((tq, ATT_W)),
        scratch_shapes=[pltpu.VMEM((N_HEADS * tq, ATT_W), F32), pltpu.VMEM((N_HEADS * tq, 1), F32)])
    return pl.pallas_call(
        body, grid_spec=grid_spec, out_shape=jax.ShapeDtypeStruct((nb, tq, ATT_W), F32),
        compiler_params=_params(("arbitrary", "arbitrary")),
    )(page_table, q, *([cache_k] * pps), *([cache_v] * pps), k_new, v_new)


def _score_keys(dots4, iw_b, tq, visible):
    score = jnp.zeros((tq, dots4.shape[1]), F32)
    for h in range(N_HEADS):
        score = score + jnp.maximum(dots4[h * tq:(h + 1) * tq], 0.0) * iw_b[h]
    score = score + 0.0
    if visible is not None:
        score = jnp.where(visible, score, NEG)
    bits = pltpu.bitcast(score, I32)
    return jnp.where(bits < 0, bits ^ 0x7FFFFFFF, bits)


def _kth_largest(count_ge, shape, k, bits):
    def step(i, v):
        cand = v | lax.shift_left(jnp.int32(1), bits - 1 - i)
        return jnp.where(count_ge(cand) >= k, cand, v)
    return lax.fori_loop(0, bits, step, jnp.zeros(shape, I32))


def _prefix_matrix(tk):
    j = lax.broadcasted_iota(I32, (tk, tk), 0)
    s = lax.broadcasted_iota(I32, (tk, tk), 1)
    return (j < s).astype(BF16)


def _additive_mask(key, vstar, need, eq_carry, prefix, visible):
    gw = prefix.shape[0]
    eq = key == vstar
    eqf = jnp.where(eq, 1.0, 0.0)
    before = []
    for g in range(key.shape[1] // gw):
        e = eqf[:, g * gw:(g + 1) * gw]
        before.append(_dot(e.astype(BF16), prefix) + eq_carry)
        eq_carry = eq_carry + jnp.sum(e, axis=1, keepdims=True)
    before = before[0] if len(before) == 1 else jnp.concatenate(before, axis=1)
    am = jnp.where(key > vstar, 0.0, jnp.where(eq, jnp.where(before < need, 0.0, NEG), NEG))
    if visible is not None:
        am = jnp.where(visible, am, NEG)
    return am, eq_carry


def _dsa_prompt_body(iq_ref, iw_ref, cq_ref, ikt_ref, ckt_ref, cvo_ref, o_ref,
                     keys_ref, half_ref, mx_ref, acc_ref, *, tq, tk, topk):
    qi = pl.program_id(0)
    n_tiles = (qi * tq + tq - 1) // tk + 1
    n_full = (qi * tq + 1) // tk
    row_pos = qi * tq + lax.broadcasted_iota(I32, (tq, tk), 0)
    col = lax.broadcasted_iota(I32, (tq, tk), 1)

    iq4 = _head_rows(iq_ref[...])
    iw = iw_ref[...]
    iw_b = [jnp.broadcast_to(iw[:, h:h + 1], (tq, tk)) for h in range(N_HEADS)]

    def score_tile(kb, masked):
        visible = (col + kb * tk <= row_pos) if masked else None
        key = _score_keys(_dot(iq4, ikt_ref[kb]), iw_b, tq, visible)
        keys_ref[kb] = key
        half_ref[kb] = lax.shift_right_arithmetic(key, 16).astype(I16)

    def score_full(kb, c):
        score_tile(kb, False)
        return c

    def score_part(kb, c):
        score_tile(kb, True)
        return c

    _loop_pairs(0, n_full, score_full, 0)
    lax.fori_loop(n_full, n_tiles, score_part, 0)

    cw = min(tk, 2 * LANES)

    def count16(pred):
        def add(kb, cnt):
            x = half_ref[kb]
            for g in range(tk // cw):
                cnt = cnt + jnp.where(pred(x[:, g * cw:(g + 1) * cw]), jnp.int16(1), jnp.int16(0))
            return cnt
        cnt = _loop_pairs(0, n_tiles, add, jnp.zeros((tq, cw), I16))
        return jnp.sum(cnt.astype(I32), axis=1, keepdims=True)

    def half_b(v):
        return jnp.broadcast_to(v, (tq, cw)).astype(I16)

    def kth16(k):
        def count_ge(cand):
            cand_b = half_b(cand + I16_MIN)
            return count16(lambda x: x >= cand_b)
        return _kth_largest(count_ge, (tq, 1), k, 16) + I16_MIN

    vh = kth16(topk)
    vh_b = half_b(vh)
    k_low = topk - count16(lambda x: x > vh_b)
    vh_b32 = jnp.broadcast_to(vh, (tq, tk))

    def fill_low(kb, c):
        key = keys_ref[kb]
        low = (key & 0xFFFF) + I16_MIN
        in_group = lax.shift_right_arithmetic(key, 16) == vh_b32
        half_ref[kb] = jnp.where(in_group, low, I16_MIN).astype(I16)
        return c

    lax.fori_loop(0, n_tiles, fill_low, 0)
    vl = kth16(k_low)
    vl_b = half_b(vl)
    need = (k_low - count16(lambda x: x > vl_b)).astype(F32)
    vstar = lax.shift_left(vh, 16) | (vl - I16_MIN)
    vstar_b = jnp.broadcast_to(vstar, (tq, tk))
    need_b = jnp.broadcast_to(need, (tq, tk))

    cqs = _head_stack(cq_ref[...])
    prefix = _prefix_matrix(min(tk, 2 * LANES))
    mx_ref[...] = jnp.full_like(mx_ref, NEG)

    def select(kb, eq_carry, masked):
        visible = (col + kb * tk <= row_pos) if masked else None
        am, eq_carry = _additive_mask(keys_ref[kb], vstar_b, need_b, eq_carry, prefix, visible)
        keys_ref[kb] = pltpu.bitcast(am, I32)
        logits4 = _dot(cqs, ckt_ref[kb])
        for h in range(N_HEADS):
            lg = logits4[h * tq:(h + 1) * tq] + am
            top = lg[:, :LANES]
            for g in range(1, tk // LANES):
                top = jnp.maximum(top, lg[:, g * LANES:(g + 1) * LANES])
            mx_ref[h] = jnp.maximum(mx_ref[h], top)
        return eq_carry

    eq_carry = _loop_pairs(0, n_full, lambda kb, c: select(kb, c, False), jnp.zeros((tq, 1), F32))
    lax.fori_loop(n_full, n_tiles, lambda kb, c: select(kb, c, True), eq_carry)

    m_b = [jnp.broadcast_to(jnp.max(mx_ref[h], axis=1, keepdims=True), (tq, tk)) for h in range(N_HEADS)]
    acc_ref[...] = jnp.zeros_like(acc_ref)

    def attend(kb, c):
        ks = pl.multiple_of(kb * tk, tk)
        am = pltpu.bitcast(keys_ref[kb], F32)
        logits4 = _dot(cqs, ckt_ref[kb])
        vo = cvo_ref[pl.ds(ks, tk), :]
        for h in range(N_HEADS):
            p = jnp.exp(logits4[h * tq:(h + 1) * tq] + am - m_b[h]).astype(BF16)
            acc_ref[h] += _dot(p, vo[:, h * 2 * HEAD_DIM:(h + 1) * 2 * HEAD_DIM])
        return c

    _loop_pairs(0, n_tiles, attend, 0)
    outs = []
    for h in range(N_HEADS):
        acc = acc_ref[h]
        outs.append(acc[:, :HEAD_DIM] / acc[:, HEAD_DIM:HEAD_DIM + 1])
    o_ref[...] = jnp.concatenate(outs, axis=1)


def _dsa_prompt(iq, iw, cq, ikt, ckt, cvo, tq, topk):
    t = iq.shape[0]
    tk = ckt.shape[2]
    body = functools.partial(_dsa_prompt_body, tq=tq, tk=tk, topk=topk)
    row = lambda w: pl.BlockSpec((tq, w), lambda i: (i, 0))
    return pl.pallas_call(
        body, grid=(t // tq,),
        in_specs=[row(ATT_W), row(LANES), row(ATT_W),
                  _const_spec(ikt.shape), _const_spec(ckt.shape), _const_spec(cvo.shape)],
        out_specs=row(ATT_W),
        out_shape=jax.ShapeDtypeStruct((t, ATT_W), F32),
        scratch_shapes=[pltpu.VMEM((t // tk, tq, tk), I32), pltpu.VMEM((t // tk, tq, tk), I16),
                        pltpu.VMEM((N_HEADS, tq, LANES), F32), pltpu.VMEM((N_HEADS, tq, 2 * HEAD_DIM), F32)],
        compiler_params=_params(("arbitrary",)),
    )(iq, iw, cq, ikt, ckt, cvo)


def _dsa_score_sample_body(pt_ref, iq_ref, iw_ref, *refs, pps, page, topk):
    ikp = refs[:pps]
    ikn_ref, keys_ref, keysn_ref, vstar_ref, need_ref, all_ref, alln_ref = refs[pps:]
    j = pl.program_id(1)
    nc = pl.num_programs(1)
    tq = iq_ref.shape[1]
    cw = pps * page
    iq4 = _head_rows(iq_ref[0])
    iw = iw_ref[0]
    iw_b = lambda n: [jnp.broadcast_to(iw[:, h:h + 1], (tq, n)) for h in range(N_HEADS)]
    key = _score_keys(_dot(iq4, _chunk(ikp)), iw_b(cw), tq, None)
    keys_ref[0, 0] = key
    all_ref[j] = key

    @pl.when(j == nc - 1)
    def _():
        row = lax.broadcasted_iota(I32, (tq, page), 0)
        col = lax.broadcasted_iota(I32, (tq, page), 1)
        key_n = _score_keys(_dot(iq4, ikn_ref[0].astype(BF16)), iw_b(page), tq, col <= row)
        keysn_ref[0] = key_n
        alln_ref[...] = key_n

        def count(pred):
            def add(c, cnt):
                return cnt + jnp.where(pred(all_ref[c]), 1, 0)
            cnt = lax.fori_loop(0, nc, add, jnp.zeros((tq, cw), I32))
            cnt_n = jnp.where(pred(alln_ref[...]), 1, 0)
            return jnp.sum(cnt, axis=1, keepdims=True) + jnp.sum(cnt_n, axis=1, keepdims=True)

        vstar = _kth_largest(lambda cand: count(lambda x: x >= (cand ^ INT_MIN)), (tq, 1), topk, 32) ^ INT_MIN
        need = (topk - count(lambda x: x > vstar)).astype(F32)
        vstar_ref[0] = jnp.broadcast_to(vstar, (tq, page))
        need_ref[0] = jnp.broadcast_to(need, (tq, page))


def _dsa_score_sample(page_table, iq, iw, cache_ik, ik_new, layer, topk):
    nb, tq, _ = iq.shape
    n_pages = page_table.shape[1]
    page = cache_ik.shape[3]
    pps = _pages_per_step(n_pages)
    nc, cw = n_pages // pps, pps * page
    body = functools.partial(_dsa_score_sample_body, pps=pps, page=page, topk=topk)
    grid_spec = pltpu.PrefetchScalarGridSpec(
        num_scalar_prefetch=1, grid=(nb, nc),
        in_specs=[_per_batch((tq, ATT_W)), _per_batch((tq, LANES))]
        + _page_specs(layer, n_pages, pps, page, HEAD_DIM, False) + [_per_batch((HEAD_DIM, page))],
        out_specs=(pl.BlockSpec((1, 1, tq, cw), lambda b, j, pt: (b, j, 0, 0)),
                   _per_batch((tq, page)), _per_batch((tq, page)), _per_batch((tq, page))),
        scratch_shapes=[pltpu.VMEM((nc, tq, cw), I32), pltpu.VMEM((tq, page), I32)])
    return pl.pallas_call(
        body, grid_spec=grid_spec,
        out_shape=(jax.ShapeDtypeStruct((nb, nc, tq, cw), I32), jax.ShapeDtypeStruct((nb, tq, page), I32),
                   jax.ShapeDtypeStruct((nb, tq, page), I32), jax.ShapeDtypeStruct((nb, tq, page), F32)),
        compiler_params=_params(("arbitrary", "arbitrary")),
    )(page_table, iq, iw, *([cache_ik] * pps), ik_new)


def _dsa_attend_sample_body(pt_ref, cq_ref, keys_ref, keysn_ref, vstar_ref, need_ref, *refs, pps, page):
    kp, vp = refs[:pps], refs[pps:2 * pps]
    kn_ref, vn_ref, o_ref, acc_ref, m_ref, l_ref, eq_ref = refs[2 * pps:]
    j = pl.program_id(1)
    tq = cq_ref.shape[1]

    @pl.when(j == 0)
    def _():
        acc_ref[...] = jnp.zeros_like(acc_ref)
        m_ref[...] = jnp.full_like(m_ref, NEG)
        l_ref[...] = jnp.zeros_like(l_ref)
        eq_ref[...] = jnp.zeros_like(eq_ref)

    cqs = _head_stack(cq_ref[0])
    vstar = vstar_ref[0][:, 0:1]
    need = need_ref[0][:, 0:1]

    def step(key, k_tile, v_tile, visible, gw):
        am, eq_carry = _additive_mask(key, vstar, need, eq_ref[...], _prefix_matrix(gw), visible)
        eq_ref[...] = eq_carry
        am4 = jnp.concatenate([am] * N_HEADS, axis=0)
        lg = _dot(cqs, k_tile) + am4
        m_old = m_ref[...]
        m_new = jnp.maximum(m_old, jnp.max(lg, axis=1, keepdims=True))
        p = jnp.where(am4 == 0.0, jnp.exp(lg - m_new), 0.0)
        alpha = jnp.exp(m_old - m_new)
        l_ref[...] = l_ref[...] * alpha + jnp.sum(p, axis=1, keepdims=True)
        acc_ref[...] = acc_ref[...] * alpha + _dot_nt(p.astype(BF16), v_tile)
        m_ref[...] = m_new

    step(keys_ref[0, 0], _chunk(kp), _chunk(vp), None, _group_width(pps, page))

    @pl.when(j == pl.num_programs(1) - 1)
    def _():
        row = lax.broadcasted_iota(I32, (tq, page), 0)
        col = lax.broadcasted_iota(I32, (tq, page), 1)
        step(keysn_ref[0], kn_ref[0].astype(BF16), vn_ref[0].astype(BF16), col <= row, page)
        o_ref[0] = _head_merge(acc_ref[...] / l_ref[...], tq)


def _dsa_attend_sample(page_table, cq, keys, keys_new, vstar, need, cache_k, cache_v, k_new, v_new, layer):
    nb, tq, _ = cq.shape
    n_pages = page_table.shape[1]
    page = cache_k.shape[3]
    pps = _pages_per_step(n_pages)
    nc, cw = n_pages // pps, pps * page
    body = functools.partial(_dsa_attend_sample_body, pps=pps, page=page)
    grid_spec = pltpu.PrefetchScalarGridSpec(
        num_scalar_prefetch=1, grid=(nb, nc),
        in_specs=[_per_batch((tq, ATT_W)),
                  pl.BlockSpec((1, 1, tq, cw), lambda b, j, pt: (b, j, 0, 0)),
                  _per_batch((tq, page)), _per_batch((tq, page)), _per_batch((tq, page))]
        + _page_specs(layer, n_pages, pps, page, ATT_W, False) + _page_specs(layer, n_pages, pps, page, ATT_W, False)
        + [_per_batch((ATT_W, page)), _per_batch((ATT_W, page))],
        out_specs=_per_batch((tq, ATT_W)),
        scratch_shapes=[pltpu.VMEM((N_HEADS * tq, ATT_W), F32),
                        pltpu.VMEM((N_HEADS * tq, 1), F32), pltpu.VMEM((N_HEADS * tq, 1), F32),
                        pltpu.VMEM((tq, 1), F32)])
    return pl.pallas_call(
        body, grid_spec=grid_spec, out_shape=jax.ShapeDtypeStruct((nb, tq, ATT_W), F32),
        compiler_params=_params(("arbitrary", "arbitrary")),
    )(page_table, cq, keys, keys_new, vstar, need, *([cache_k] * pps), *([cache_v] * pps), k_new, v_new)


def _ffn_body(*refs, seq_rows, carried, final):
    it = iter(refs)
    x_ref, oa_ref, ob_ref, oc_ref, wout_ref, g_ref, wup_ref, cw_ref, cb_ref, wdown_ref = (next(it) for _ in range(10))
    e0_ref = e1_ref = gf_ref = carry_ref = None
    if not carried:
        e0_ref, e1_ref = next(it), next(it)
    if final:
        gf_ref = next(it)
    xo_ref, tail_ref = next(it), next(it)
    if carried:
        carry_ref = next(it)

        @pl.when(pl.program_id(0) == 0)
        def _():
            carry_ref[...] = jnp.zeros_like(carry_ref)

    mixed = jnp.concatenate([oa_ref[...], ob_ref[...], oc_ref[...]], axis=1).astype(BF16)
    x1 = x_ref[...] + _dot(mixed, wout_ref[...])
    h = _rmsnorm(x1, g_ref[...]).astype(BF16)
    up = _dot(h, wup_ref[...])
    d_ff = up.shape[1] // 2
    a, gate = up[:, :d_ff], up[:, d_ff:]
    rows = a.shape[0]
    r = lax.broadcasted_iota(I32, a.shape, 0) % seq_rows
    if carried:
        nc = carry_ref.shape[0]
        e0 = jnp.broadcast_to(carry_ref[nc - 2:nc - 1, :], a.shape)
        e1 = jnp.broadcast_to(carry_ref[nc - 1:nc, :], a.shape)
    else:
        e0, e1 = e0_ref[...], e1_ref[...]
    a_m1 = jnp.where(r == 0, e1, pltpu.roll(a, 1, 0))
    a_m2 = jnp.where(r == 0, e0, jnp.where(r == 1, e1, pltpu.roll(a, 2, 0)))
    cw = cw_ref[...]
    c = cb_ref[...] + a_m2 * cw[0:1, :] + a_m1 * cw[1:2, :] + a * cw[2:3, :]
    act = (c * jax.nn.sigmoid(c) * gate).astype(BF16)
    x2 = x1 + _dot(act, wdown_ref[...])
    xo_ref[...] = _rmsnorm(x2, gf_ref[...]) if final else x2
    tail = tail_ref.shape[0]
    tail_ref[...] = a[rows - tail:, :]
    if carried:
        nc = carry_ref.shape[0]
        carry_ref[...] = a[rows - nc:, :]


def _ffn(x, oa, ob, oc, w_out, g, w_up, conv_w, conv_b, w_down, block_rows, prev=None, final_g=None):
    rows, d = x.shape
    d_ff = w_down.shape[0]
    carried = prev is None
    final = final_g is not None
    row = lambda w: pl.BlockSpec((block_rows, w), lambda i: (i, 0))
    ins = [x, oa, ob, oc, w_out, g, w_up, conv_w, conv_b, w_down]
    in_specs = [row(d), row(HG_W), row(ATT_W), row(ATT_W), _const_spec(w_out.shape), _const_spec((1, d)),
                _const_spec(w_up.shape), _const_spec(conv_w.shape), _const_spec((1, d_ff)), _const_spec(w_down.shape)]
    if not carried:
        ins += list(prev)
        in_specs += [row(d_ff), row(d_ff)]
    if final:
        ins.append(final_g)
        in_specs.append(_const_spec((1, d)))
    if carried:
        tail_rows, tail_spec = 8, pl.BlockSpec((8, d_ff), lambda i: (0, 0))
        scratch = [pltpu.VMEM((8, d_ff), F32)]
        seq_rows = block_rows
    else:
        tail_rows, tail_spec = rows, row(d_ff)
        scratch = []
        seq_rows = 8
    body = functools.partial(_ffn_body, seq_rows=seq_rows, carried=carried, final=final)
    return pl.pallas_call(
        body, grid=(rows // block_rows,),
        in_specs=in_specs, out_specs=(row(d), tail_spec),
        out_shape=(jax.ShapeDtypeStruct((rows, d), F32), jax.ShapeDtypeStruct((tail_rows, d_ff), F32)),
        scratch_shapes=scratch,
        compiler_params=_params(("arbitrary",)),
    )(*ins)


def _rope_tables(pos):
    half = HEAD_DIM // 2
    inv = jnp.exp(-math.log(ROPE_THETA) * jnp.arange(half, dtype=F32) / half)
    ang = pos.astype(F32)[:, None] * inv[None, :]
    cos, sin = jnp.cos(ang), jnp.sin(ang)
    cos = jnp.tile(jnp.concatenate([cos, cos], axis=1), (1, N_HEADS))
    sin = jnp.tile(jnp.concatenate([-sin, sin], axis=1), (1, N_HEADS))
    return cos, sin


def _w_in_body(w_ref, o_ref, *, n_iw):
    i = pl.program_id(0)
    last = pl.num_programs(0) - 1
    x = w_ref[...]
    r = lax.broadcasted_iota(I32, (x.shape[0], x.shape[2]), 0)
    for l in range(x.shape[1]):
        xl = x[:, l, :]
        ik = jnp.where(r < HEAD_DIM, xl, 0.0)
        iw = jnp.where(r < n_iw, pltpu.roll(xl, LANES - HEAD_DIM, 0), 0.0)
        y = jnp.where(i == last, iw, jnp.where(i == last - 1, ik, xl))
        o_ref[l] = y.T.astype(BF16)


def _pad_w_in(w_in):
    depth, d, n = w_in.shape
    n_in = pl.cdiv(n, LANES)
    assert n_in * LANES == _C_IW and n > _C_IK + HEAD_DIM
    return pl.pallas_call(
        functools.partial(_w_in_body, n_iw=n - _C_IK - HEAD_DIM), grid=(IN_PAD // LANES,),
        in_specs=[pl.BlockSpec((LANES, depth, d), lambda i: (jnp.minimum(i, n_in - 1), 0, 0))],
        out_specs=pl.BlockSpec((depth, d, LANES), lambda i: (0, 0, i)),
        out_shape=jax.ShapeDtypeStruct((depth, d, IN_PAD), BF16),
        compiler_params=_params(("arbitrary",)),
    )(jnp.transpose(w_in, (2, 0, 1)))


def _new_page(a, page):
    return jnp.swapaxes(jnp.pad(a, ((0, 0), (0, page - a.shape[1]), (0, 0))), 1, 2)


def _feature_major(cache):
    depth, pool, page = cache.shape[:3]
    perm = (0, 1, 3, 4, 2) if cache.ndim == 5 else (0, 1, 3, 2)
    return jnp.transpose(cache, perm).reshape(depth, pool, -1, page)


def _slot_major(pages, lead):
    depth, n_pg, w, page = pages.shape
    if w == HEAD_DIM:
        return jnp.transpose(pages, (0, 1, 3, 2)).reshape((depth,) + lead + (n_pg, page, w))
    a = pages.reshape(depth, n_pg, N_HEADS, HEAD_DIM, page)
    return jnp.transpose(a, (0, 1, 4, 2, 3)).reshape((depth,) + lead + (n_pg, page, N_HEADS, HEAD_DIM))


def kernel(x_prompt, x_sample, cache_sb_k, cache_sb_v, cache_dsa_k, cache_dsa_v, cache_idx_k, state_hgrn, state_conv, page_table, hg_lower_bounds, attn_norm, w_in, hg_norm, w_out, ffn_norm, w_up, conv_w, conv_b, w_down, final_norm):
    depth = w_in.shape[0]
    nb_p, seq, d = x_prompt.shape
    nb_s, seq_s, _ = x_sample.shape
    assert nb_p == 1, "the prompt group is one sequence"
    page = cache_sb_k.shape[2]
    n_pages = page_table.shape[1]
    past = n_pages * page
    d_ff = w_down.shape[1]
    rows_s = nb_s * seq_s

    lb = jnp.cumsum(jax.nn.softmax(hg_lower_bounds.astype(F32), axis=0), axis=0)
    lb = (lb - lb[0]).reshape(depth, 1, HG_W)
    cos_p, sin_p = _rope_tables(jnp.arange(seq, dtype=I32))
    cos_s, sin_s = _rope_tables(jnp.tile(past + jnp.arange(seq_s, dtype=I32), nb_s))

    cache_sb_k, cache_sb_v = _feature_major(cache_sb_k), _feature_major(cache_sb_v)
    cache_dsa_k, cache_dsa_v = _feature_major(cache_dsa_k), _feature_major(cache_dsa_v)
    cache_idx_k = _feature_major(cache_idx_k)

    tq = 128
    block_rows = 256
    key_tiles = (min(512, seq), min(1024, seq))
    assert all(seq % kt == 0 and kt % block_rows == 0 for kt in key_tiles)
    topk_p = min(TOPK_MAX, seq // 4)
    topk_s = min(TOPK_MAX, (past + seq_s) // 4)

    w_in_pad = _pad_w_in(w_in)
    xp = x_prompt.reshape(seq, d)
    xs = x_sample.reshape(rows_s, d)
    p_rows, s_rows = [], []
    for l in range(depth):
        w_in_l = w_in_pad[l]
        g_attn = attn_norm[l].reshape(1, d)
        lb_terms = (jnp.log(lb[l]), jnp.log1p(-lb[l]), 1.0 - lb[l])
        ng = jnp.tile(hg_norm[l], N_HEADS).reshape(1, HG_W)
        w_out_l, w_up_l, w_down_l = w_out[l].astype(BF16), w_up[l].astype(BF16), w_down[l].astype(BF16)
        g_ffn = ffn_norm[l].reshape(1, d)
        cb = conv_b[l].reshape(1, d_ff)
        last = l == depth - 1
        final_g = final_norm.reshape(1, d) if last else None

        (hg4, sq, skp, svp, skt, svb, cq, ckp, cvp, ckt, cvo, iq, ikt, ikp, iw) = _inproj(
            xp, g_attn, w_in_l, cos_p, sin_p, block_rows, page=page, key_tiles=key_tiles)
        o_a, hg_new = _hgrn(hg4, lb_terms, ng, jnp.zeros((1, N_HEADS, HG_D, HG_D), F32), 1, seq, block_rows, 16)
        o_b = _sb_prompt(sq, skt, svb, tq)
        o_c = _dsa_prompt(iq, iw, cq, ikt, ckt, cvo, tq, topk_p)
        xp, tail = _ffn(xp, o_a, o_b, o_c, w_out_l, g_ffn, w_up_l, conv_w[l], cb, w_down_l, block_rows, final_g=final_g)
        p_rows.append((skp, svp, ckp, cvp, ikp, hg_new, tail[8 - (conv_w.shape[1] - 1):]))

        (hg4, sq, sk, sv, cq, ck, cv, iq, ik, iw) = _inproj(xs, g_attn, w_in_l, cos_s, sin_s, rows_s)
        o_a, hg_new = _hgrn(hg4, lb_terms, ng, state_hgrn[l], nb_s, seq_s, seq_s, seq_s)
        b3 = lambda a: a.reshape(nb_s, seq_s, a.shape[-1])
        new = lambda a: _new_page(b3(a), page)
        o_b = _sb_sample(page_table, b3(sq), cache_sb_k, cache_sb_v, new(sk), new(sv), l)
        keys, keys_new, vstar, need = _dsa_score_sample(page_table, b3(iq), b3(iw), cache_idx_k, new(ik), l, topk_s)
        o_c = _dsa_attend_sample(page_table, b3(cq), keys, keys_new, vstar, need, cache_dsa_k, cache_dsa_v,
                                 new(ck), new(cv), l)
        prev = (jnp.repeat(state_conv[l][:, 0], seq_s, axis=0), jnp.repeat(state_conv[l][:, 1], seq_s, axis=0))
        xs, a_rows = _ffn(xs, o_a, o_b.reshape(rows_s, ATT_W), o_c.reshape(rows_s, ATT_W), w_out_l, g_ffn, w_up_l,
                          conv_w[l], cb, w_down_l, rows_s, prev=prev, final_g=final_g)
        conv_new = a_rows.reshape(nb_s, seq_s, d_ff)[:, seq_s - (conv_w.shape[1] - 1):]
        s_rows.append((sk, sv, ck, cv, ik, hg_new, conv_new))

    stack = lambda rows, j: jnp.stack([r[j] for r in rows])
    heads = lambda a: _slot_major(a, (1,))
    heads_s = lambda a: a.reshape(depth, nb_s, seq_s, N_HEADS, HEAD_DIM)
    return (xp.reshape(1, seq, d), xs.reshape(nb_s, seq_s, d),
            heads(stack(p_rows, 0)), heads(stack(p_rows, 1)), heads(stack(p_rows, 2)), heads(stack(p_rows, 3)),
            heads(stack(p_rows, 4)),
            stack(p_rows, 5), stack(p_rows, 6).reshape(depth, 1, conv_w.shape[1] - 1, d_ff),
            heads_s(stack(s_rows, 0)), heads_s(stack(s_rows, 1)), heads_s(stack(s_rows, 2)), heads_s(stack(s_rows, 3)),
            stack(s_rows, 4).reshape(depth, nb_s, seq_s, HEAD_DIM),
            stack(s_rows, 5), stack(s_rows, 6))
```
